```python
import jax, jax.numpy as jnp
from jax import lax
import numpy as np

D_MODEL = 1024
BATCH = 4
SEQ = 4096
DEPTH = 1

D_POOL = D_MODEL // 2
POOL_WINDOWS = (2, 4, 8, 16)
POOL_GROUPS = len(POOL_WINDOWS)
POOL_GROUP_DIM = D_POOL // POOL_GROUPS
D_GLA_V = D_MODEL // 2
GLA_HEADS = 4
GLA_DV = D_GLA_V // GLA_HEADS
GLA_DK = GLA_DV // 2
D_GLA_K = GLA_HEADS * GLA_DK
GATE_RANK = 16
GATE_TAU = 16.0
CHUNK = 16
D_MIX = D_POOL + D_GLA_V
D_IN = D_POOL + 2 * D_GLA_K + D_GLA_V + GATE_RANK + D_GLA_V
D_FF = 2816
CONV_WIDTH = 3
EPS = 1e-6

kernel_name = "hybrid_pool_gla_convglu_layer"


def rmsnorm(x, g):
    xf = x.astype(jnp.float32)
    y = xf * lax.rsqrt(jnp.mean(xf * xf, axis=-1, keepdims=True) + EPS)
    return (y * g.astype(jnp.float32)).astype(x.dtype)


def multiscale_pool(u, w_pool, pool_scale):
    B, S, _ = u.shape
    uf = u.astype(jnp.float32)
    cs = jnp.cumsum(uf, axis=1)
    pos = jnp.arange(1, S + 1, dtype=jnp.float32)[:, None]
    diffs = []
    for g, w in enumerate(POOL_WINDOWS):
        sl = slice(g * POOL_GROUP_DIM, (g + 1) * POOL_GROUP_DIM)
        c = cs[..., sl]
        lagged = jnp.pad(c, ((0, 0), (w, 0), (0, 0)))[:, :S]
        mean = (c - lagged) / jnp.minimum(pos, float(w))
        diffs.append(mean - uf[..., sl])
    d = jnp.stack(diffs, axis=2)
    y = jnp.einsum('bsgc,gcd->bsgd', d, w_pool.astype(jnp.float32)).reshape(B, S, D_POOL)
    return (y * pool_scale.astype(jnp.float32)).astype(u.dtype)


def gla_chunked(q, k, v, log_a):
    B, S, H, DK = q.shape
    DV = v.shape[-1]
    N = S // CHUNK

    def to_chunks(t):
        return t.astype(jnp.float32).reshape(B, N, CHUNK, H, t.shape[-1]).transpose(0, 3, 1, 2, 4)

    qc = to_chunks(q) * (DK ** -0.5)
    kc = to_chunks(k)
    vc = to_chunks(v)
    b = jnp.cumsum(to_chunks(log_a), axis=3)

    causal = jnp.tril(jnp.ones((CHUNK, CHUNK), dtype=bool))
    rel = b[..., :, None, :] - b[..., None, :, :]
    rel_decay = jnp.exp(jnp.where(causal[:, :, None], rel, -jnp.inf))
    scores = jnp.einsum('bhnid,bhnjd,bhnijd->bhnij', qc, kc, rel_decay)
    o_intra = jnp.einsum('bhnij,bhnjv->bhniv', scores, vc)

    b_last = b[..., -1, :]
    kv = jnp.einsum('bhncd,bhncv->bhndv', kc * jnp.exp(b_last[..., None, :] - b), vc)
    chunk_decay = jnp.exp(b_last)

    def step(state, inp):
        dec, kv_n = inp
        return dec[..., None] * state + kv_n, state

    state0 = jnp.zeros((B, H, DK, DV), jnp.float32)
    _, s_prev = lax.scan(step, state0, (jnp.moveaxis(chunk_decay, 2, 0), jnp.moveaxis(kv, 2, 0)))
    s_prev = jnp.moveaxis(s_prev, 0, 2)
    o_inter = jnp.einsum('bhncd,bhndv->bhncv', qc * jnp.exp(b), s_prev)

    o = (o_intra + o_inter).transpose(0, 2, 3, 1, 4).reshape(B, S, H, DV)
    return o.astype(q.dtype)


def token_mixer(h, w_in, w_pool, pool_scale, w_gate_up, b_gate, g_gla_norm, w_out):
    B, S, _ = h.shape
    proj = h @ w_in
    cuts = [int(c) for c in np.cumsum([D_POOL, D_GLA_K, D_GLA_K, D_GLA_V, GATE_RANK])]
    u_pool, q, k, v, g_low, r = jnp.split(proj, cuts, axis=-1)

    y_pool = multiscale_pool(u_pool, w_pool, pool_scale)

    gate_logits = (g_low @ w_gate_up + b_gate).astype(jnp.float32)
    log_a = jax.nn.log_sigmoid(gate_logits) / GATE_TAU
    o = gla_chunked(q.reshape(B, S, GLA_HEADS, GLA_DK),
                    k.reshape(B, S, GLA_HEADS, GLA_DK),
                    v.reshape(B, S, GLA_HEADS, GLA_DV),
                    log_a.reshape(B, S, GLA_HEADS, GLA_DK))
    o = rmsnorm(o, g_gla_norm) * jax.nn.silu(r.reshape(B, S, GLA_HEADS, GLA_DV))
    y_gla = o.reshape(B, S, D_GLA_V)

    y = jnp.concatenate([y_pool, y_gla.astype(y_pool.dtype)], axis=-1)
    return y @ w_out


def conv_glu_ffn(h, w_ffn_in, conv_w, conv_b, w_ffn_out):
    up = h @ w_ffn_in
    gate, val = jnp.split(up, 2, axis=-1)
    gate = lax.conv_general_dilated(
        gate, conv_w[:, None, :].astype(gate.dtype), window_strides=(1,),
        padding=[(CONV_WIDTH - 1, 0)], dimension_numbers=('NWC', 'WIO', 'NWC'),
        feature_group_count=D_FF) + conv_b
    return (jax.nn.gelu(gate, approximate=False) * val) @ w_ffn_out


def setup_inputs(seed: int = 0) -> dict:
    key = jax.random.key(seed)
    ks = jax.random.split(key, 16)
    f32 = jnp.float32
    nrm = lambda k, shape, scale: jax.random.normal(k, shape, f32) * scale
    return {
        "x": nrm(ks[0], (BATCH, SEQ, D_MODEL), 1.0),
        "g_pre_mix": 1.0 + nrm(ks[1], (D_MODEL,), 0.05),
        "w_in": nrm(ks[2], (D_MODEL, D_IN), D_MODEL ** -0.5),
        "w_pool": nrm(ks[3], (POOL_GROUPS, POOL_GROUP_DIM, POOL_GROUP_DIM), POOL_GROUP_DIM ** -0.5),
        "pool_scale": 1.0 + nrm(ks[4], (D_POOL,), 0.05),
        "w_gate_up": nrm(ks[5], (GATE_RANK, D_GLA_K), GATE_RANK ** -0.5),
        "b_gate": nrm(ks[6], (D_GLA_K,), 0.1),
        "g_gla_norm": 1.0 + nrm(ks[7], (GLA_DV,), 0.05),
        "w_out": nrm(ks[8], (D_MIX, D_MODEL), D_MIX ** -0.5),
        "g_post_mix": 1.0 + nrm(ks[9], (D_MODEL,), 0.05),
        "g_pre_ffn": 1.0 + nrm(ks[10], (D_MODEL,), 0.05),
        "w_ffn_in": nrm(ks[11], (D_MODEL, 2 * D_FF), D_MODEL ** -0.5),
        "conv_w": nrm(ks[12], (CONV_WIDTH, D_FF), CONV_WIDTH ** -0.5),
        "conv_b": nrm(ks[13], (D_FF,), 0.02),
        "w_ffn_out": nrm(ks[14], (D_FF, D_MODEL), D_FF ** -0.5),
        "g_post_ffn": 1.0 + nrm(ks[15], (D_MODEL,), 0.05),
    }


def reference(x, g_pre_mix, w_in, w_pool, pool_scale, w_gate_up, b_gate, g_gla_norm, w_out,
              g_post_mix, g_pre_ffn, w_ffn_in, conv_w, conv_b, w_ffn_out, g_post_ffn):
    for _ in range(DEPTH):
        h = rmsnorm(x, g_pre_mix)
        mix = token_mixer(h, w_in, w_pool, pool_scale, w_gate_up, b_gate, g_gla_norm, w_out)
        x = x + rmsnorm(mix, g_post_mix)
        h = rmsnorm(x, g_pre_ffn)
        ff = conv_glu_ffn(h, w_ffn_in, conv_w, conv_b, w_ffn_out)
        x = x + rmsnorm(ff, g_post_ffn)
    return x
```

```python
import functools
import math

import jax
import jax.numpy as jnp
from jax import lax
from jax.experimental import pallas as pl
from jax.experimental.pallas import tpu as pltpu

D_MODEL = 1024
D_POOL = 512
POOL_WINDOWS = (2, 4, 8, 16)
POOL_GROUP_DIM = 128
MAX_WINDOW = 16
D_GLA_V = 512
GLA_HEADS = 4
GLA_DV = 128
GLA_DK = 64
D_GLA_K = 256
GATE_RANK = 16
GATE_TAU = 16.0
D_FF = 2816
EPS = 1e-6

LANES = 128
GATE_PAD = LANES
COL_U = 0
COL_Q = COL_U + D_POOL
COL_K = COL_Q + D_GLA_K
COL_V = COL_K + D_GLA_K
COL_R = COL_V + D_GLA_V
COL_G = COL_R + D_GLA_V
D_IN_PACKED = COL_G + GATE_PAD

MIX_TILE = 256
GLA_BLOCK = 128
FFN_TILE = 512
FFN_CHUNK = 256
CONV_TAIL = 8

VMEM_LIMIT_BYTES = 56 * 1024 * 1024

F32 = jnp.float32
BF16 = jnp.bfloat16


def _rmsnorm(x, g):
    return x * lax.rsqrt(jnp.mean(x * x, axis=-1, keepdims=True) + EPS) * g


def _dot(a, b):
    return jnp.dot(a, b, preferred_element_type=F32)


def _mixer_kernel(x_ref, gpre_ref, win_ref, wpool_ref, pscale_ref, wgu_ref, bg_ref, ggla_ref,
                  wout_ref, gpost_ref, o_ref, proj_ref, uext_ref, st_ref, y_ref):
    ts = MIX_TILE
    j = pl.program_id(1)

    @pl.when(j == 0)
    def _():
        uext_ref[0:MAX_WINDOW, :] = jnp.zeros((MAX_WINDOW, D_POOL), F32)
        st_ref[...] = jnp.zeros_like(st_ref)

    x = x_ref[...]
    h = _rmsnorm(x, gpre_ref[...]).astype(BF16)
    proj_ref[...] = _dot(h, win_ref[...])

    uext_ref[MAX_WINDOW:MAX_WINDOW + ts, :] = proj_ref[:, COL_U:COL_U + D_POOL]
    row = lax.broadcasted_iota(jnp.int32, (ts, 1), 0)
    pos = (j * ts + row + 1).astype(F32)
    for g, w in enumerate(POOL_WINDOWS):
        cols = slice(g * POOL_GROUP_DIM, (g + 1) * POOL_GROUP_DIM)
        u_g = uext_ref[MAX_WINDOW:MAX_WINDOW + ts, cols]
        acc = u_g
        for s in range(1, w):
            acc = acc + uext_ref[MAX_WINDOW - s:MAX_WINDOW - s + ts, cols]
        d = acc / jnp.minimum(pos, float(w)) - u_g
        y = _dot(d.astype(BF16), wpool_ref[g]) * pscale_ref[:, cols]
        y_ref[:, cols] = y.astype(BF16)
    uext_ref[0:MAX_WINDOW, :] = uext_ref[ts:ts + MAX_WINDOW, :]

    L = GLA_BLOCK
    ri = lax.broadcasted_iota(jnp.int32, (L, L), 0)
    ci = lax.broadcasted_iota(jnp.int32, (L, L), 1)
    causal = ri >= ci
    tri = jnp.where(causal, 1.0, 0.0).astype(BF16)
    lane_head = lax.broadcasted_iota(jnp.int32, (L, D_GLA_K), 1) // GLA_DK
    st_row_head = lax.broadcasted_iota(jnp.int32, (D_GLA_V, D_GLA_K), 0) // GLA_DV
    st_col_head = lax.broadcasted_iota(jnp.int32, (D_GLA_V, D_GLA_K), 1) // GLA_DK
    st_mask = st_row_head == st_col_head
    ggla = ggla_ref[...]
    for nb in range(ts // L):
        rows = slice(nb * L, (nb + 1) * L)
        q = proj_ref[rows, COL_Q:COL_Q + D_GLA_K] * (GLA_DK ** -0.5)
        k = proj_ref[rows, COL_K:COL_K + D_GLA_K]
        v = proj_ref[rows, COL_V:COL_V + D_GLA_V].astype(BF16)
        r = proj_ref[rows, COL_R:COL_R + D_GLA_V]
        glow = proj_ref[rows, COL_G:COL_G + GATE_PAD].astype(BF16)
        logits = _dot(glow, wgu_ref[...]) + bg_ref[...]
        log_a = (jnp.minimum(logits, 0.0) - jnp.log1p(jnp.exp(-jnp.abs(logits)))) * (1.0 / GATE_TAU)
        la_hi = log_a.astype(BF16)
        la_lo = (log_a - la_hi.astype(F32)).astype(BF16)
        bc = _dot(tri, la_hi) + _dot(tri, la_lo)
        b_last = bc[L - 1:L, :]
        b_mid = bc[L // 2 - 1:L // 2, :]
        q_intra = q * jnp.exp(bc - b_mid)
        k_intra = (k * jnp.exp(b_mid - bc)).astype(BF16)
        q_inter = (q * jnp.exp(bc)).astype(BF16)
        k_hat = (k * jnp.exp(b_last - bc)).astype(BF16)

        q_stack = jnp.concatenate(
            [jnp.where(lane_head == hh, q_intra, 0.0) for hh in range(GLA_HEADS)], axis=0).astype(BF16)
        scores = lax.dot_general(q_stack, k_intra, (((1,), (1,)), ((), ())),
                                 preferred_element_type=F32)
        st = st_ref[...]
        o_inter = lax.dot_general(q_inter, st.astype(BF16), (((1,), (1,)), ((), ())),
                                  preferred_element_type=F32)
        kv_t = lax.dot_general(v, k_hat, (((0,), (0,)), ((), ())),
                               preferred_element_type=F32)
        st_ref[...] = st * jnp.exp(b_last) + jnp.where(st_mask, kv_t, 0.0)

        for hh in range(GLA_HEADS):
            vcols = slice(hh * GLA_DV, (hh + 1) * GLA_DV)
            a_h = jnp.where(causal, scores[hh * L:(hh + 1) * L, :], 0.0).astype(BF16)
            o_h = _dot(a_h, v[:, vcols]) + o_inter[:, vcols]
            o_h = _rmsnorm(o_h, ggla)
            r_h = r[:, vcols]
            o_h = o_h * (r_h * jax.nn.sigmoid(r_h))
            y_ref[rows, D_POOL + hh * GLA_DV:D_POOL + (hh + 1) * GLA_DV] = o_h.astype(BF16)

    mix = _dot(y_ref[...], wout_ref[...])
    o_ref[...] = x + _rmsnorm(mix, gpost_ref[...])


def _ffn_kernel(x_ref, gpre_ref, win_ref, cw_ref, cb_ref, wout_ref, gpost_ref, o_ref,
                tail_ref, acc_ref):
    ts = FFN_TILE
    j = pl.program_id(1)

    @pl.when(j == 0)
    def _():
        tail_ref[...] = jnp.zeros_like(tail_ref)

    x = x_ref[...]
    h = _rmsnorm(x, gpre_ref[...]).astype(BF16)
    row = lax.broadcasted_iota(jnp.int32, (ts, FFN_CHUNK), 0)
    for c in range(D_FF // FFN_CHUNK):
        cols = slice(c * FFN_CHUNK, (c + 1) * FFN_CHUNK)
        vcols = slice(D_FF + c * FFN_CHUNK, D_FF + (c + 1) * FFN_CHUNK)
        gate = _dot(h, win_ref[:, cols])
        val = _dot(h, win_ref[:, vcols])
        prev = tail_ref[:, cols]
        tail_ref[:, cols] = gate[ts - CONV_TAIL:ts, :]
        p1 = prev[CONV_TAIL - 1:CONV_TAIL, :]
        p2 = prev[CONV_TAIL - 2:CONV_TAIL - 1, :]
        g1 = jnp.where(row == 0, p1, pltpu.roll(gate, 1, 0))
        g2 = jnp.where(row == 0, p2, jnp.where(row == 1, p1, pltpu.roll(gate, 2, 0)))
        cw = cw_ref[:, cols]
        gc = g2 * cw[0:1, :] + g1 * cw[1:2, :] + gate * cw[2:3, :] + cb_ref[:, cols]
        act = 0.5 * gc * (1.0 + lax.erf(gc * (1.0 / math.sqrt(2.0)))) * val
        contrib = _dot(act.astype(BF16), wout_ref[cols, :])
        if c == 0:
            acc_ref[...] = contrib
        else:
            acc_ref[...] += contrib
    o_ref[...] = x + _rmsnorm(acc_ref[...], gpost_ref[...])


def _const_spec(shape):
    zeros = (0,) * len(shape)
    return pl.BlockSpec(shape, lambda b, j: zeros, pipeline_mode=pl.Buffered(1))


def _token_mixer(x, g_pre, w_in_p, w_pool, pool_scale, w_gu_p, b_gate, g_gla, w_out, g_post):
    B, S, D = x.shape
    ts = MIX_TILE
    tile_spec = pl.BlockSpec((None, ts, D), lambda b, j: (b, j, 0))
    return pl.pallas_call(
        _mixer_kernel,
        grid=(B, S // ts),
        in_specs=[
            tile_spec,
            _const_spec((1, D)),
            _const_spec((D, D_IN_PACKED)),
            _const_spec((len(POOL_WINDOWS), POOL_GROUP_DIM, POOL_GROUP_DIM)),
            _const_spec((1, D_POOL)),
            _const_spec((GATE_PAD, D_GLA_K)),
            _const_spec((1, D_GLA_K)),
            _const_spec((1, GLA_DV)),
            _const_spec((D, D)),
            _const_spec((1, D)),
        ],
        out_specs=tile_spec,
        out_shape=jax.ShapeDtypeStruct((B, S, D), F32),
        scratch_shapes=[
            pltpu.VMEM((ts, D_IN_PACKED), F32),
            pltpu.VMEM((ts + MAX_WINDOW, D_POOL), F32),
            pltpu.VMEM((D_GLA_V, D_GLA_K), F32),
            pltpu.VMEM((ts, D), BF16),
        ],
        compiler_params=pltpu.CompilerParams(
            dimension_semantics=("arbitrary", "arbitrary"),
            vmem_limit_bytes=VMEM_LIMIT_BYTES),
        name="token_mixer",
    )(x, g_pre, w_in_p, w_pool, pool_scale, w_gu_p, b_gate, g_gla, w_out, g_post)


def _conv_glu_ffn(x, g_pre, w_in, conv_w, conv_b, w_out, g_post):
    B, S, D = x.shape
    ts = FFN_TILE
    tile_spec = pl.BlockSpec((None, ts, D), lambda b, j: (b, j, 0))
    return pl.pallas_call(
        _ffn_kernel,
        grid=(B, S // ts),
        in_specs=[
            tile_spec,
            _const_spec((1, D)),
            _const_spec((D, 2 * D_FF)),
            _const_spec((3, D_FF)),
            _const_spec((1, D_FF)),
            _const_spec((D_FF, D)),
            _const_spec((1, D)),
        ],
        out_specs=tile_spec,
        out_shape=jax.ShapeDtypeStruct((B, S, D), F32),
        scratch_shapes=[
            pltpu.VMEM((CONV_TAIL, D_FF), F32),
            pltpu.VMEM((ts, D), F32),
        ],
        compiler_params=pltpu.CompilerParams(
            dimension_semantics=("arbitrary", "arbitrary"),
            vmem_limit_bytes=VMEM_LIMIT_BYTES),
        name="conv_glu_ffn",
    )(x, g_pre, w_in, conv_w, conv_b, w_out, g_post)


def _pack_w_in(w_in):
    c0 = D_POOL
    c1 = c0 + D_GLA_K
    c2 = c1 + D_GLA_K
    c3 = c2 + D_GLA_V
    c4 = c3 + GATE_RANK
    g_low = jnp.pad(w_in[:, c3:c4], ((0, 0), (0, GATE_PAD - GATE_RANK)))
    return jnp.concatenate([w_in[:, :c3], w_in[:, c4:], g_low], axis=1).astype(BF16)


def kernel(x, g_pre_mix, w_in, w_pool, pool_scale, w_gate_up, b_gate, g_gla_norm, w_out, g_post_mix,
           g_pre_ffn, w_ffn_in, conv_w, conv_b, w_ffn_out, g_post_ffn):
    row = lambda a: a.reshape(1, -1).astype(F32)
    w_gu_p = jnp.pad(w_gate_up, ((0, GATE_PAD - GATE_RANK), (0, 0))).astype(BF16)
    x1 = _token_mixer(x, row(g_pre_mix), _pack_w_in(w_in), w_pool.astype(BF16), row(pool_scale),
                      w_gu_p, row(b_gate), row(g_gla_norm), w_out.astype(BF16), row(g_post_mix))
    return _conv_glu_ffn(x1, row(g_pre_ffn), w_ffn_in.astype(BF16), conv_w.astype(F32),
                         row(conv_b), w_ffn_out.astype(BF16), row(g_post_ffn))
```

```python
import functools
import math

import jax
import jax.numpy as jnp
from jax import lax
from jax.experimental import pallas as pl
from jax.experimental.pallas import tpu as pltpu

D_MODEL = 1024
D_POOL = 512
POOL_WINDOWS = (2, 4, 8, 16)
POOL_GROUP_DIM = 128
MAX_WINDOW = 16
D_GLA_V = 512
GLA_HEADS = 4
GLA_DV = 128
GLA_DK = 64
D_GLA_K = 256
GATE_RANK = 16
GATE_TAU = 16.0
D_FF = 2816
EPS = 1e-6

LANES = 128
GATE_PAD = LANES
COL_U = 0
COL_Q = COL_U + D_POOL
COL_K = COL_Q + D_GLA_K
COL_V = COL_K + D_GLA_K
COL_R = COL_V + D_GLA_V
COL_G = COL_R + D_GLA_V
D_IN_PACKED = COL_G + GATE_PAD

MIX_TILE = 256
GLA_BLOCK = 128
FFN_TILE = 512
FFN_CHUNK = 256
CONV_TAIL = 8

VMEM_LIMIT_BYTES = 56 * 1024 * 1024

F32 = jnp.float32
BF16 = jnp.bfloat16


def _rmsnorm(x, g):
    return x * lax.rsqrt(jnp.mean(x * x, axis=-1, keepdims=True) + EPS) * g


def _dot(a, b):
    return jnp.dot(a, b, preferred_element_type=F32)


def _mixer_kernel(x_ref, gpre_ref, win_ref, wpool_ref, pscale_ref, wgu_ref, bg_ref, ggla_ref,
                  wout_ref, gpost_ref, o_ref, proj_ref, uext_ref, st_ref, y_ref):
    ts = MIX_TILE
    j = pl.program_id(1)

    @pl.when(j == 0)
    def _():
        uext_ref[0:MAX_WINDOW, :] = jnp.zeros((MAX_WINDOW, D_POOL), F32)
        st_ref[...] = jnp.zeros_like(st_ref)

    x = x_ref[...]
    h = _rmsnorm(x, gpre_ref[...]).astype(BF16)
    proj_ref[...] = _dot(h, win_ref[...])

    uext_ref[MAX_WINDOW:MAX_WINDOW + ts, :] = proj_ref[:, COL_U:COL_U + D_POOL]
    row = lax.broadcasted_iota(jnp.int32, (ts, 1), 0)
    pos = (j * ts + row + 1).astype(F32)
    for g, w in enumerate(POOL_WINDOWS):
        cols = slice(g * POOL_GROUP_DIM, (g + 1) * POOL_GROUP_DIM)
        u_g = uext_ref[MAX_WINDOW:MAX_WINDOW + ts, cols]
        acc = u_g
        for s in range(1, w):
            acc = acc + uext_ref[MAX_WINDOW - s:MAX_WINDOW - s + ts, cols]
        d = acc / jnp.minimum(pos, float(w)) - u_g
        y = _dot(d.astype(BF16), wpool_ref[g]) * pscale_ref[:, cols]
        y_ref[:, cols] = y.astype(BF16)
    uext_ref[0:MAX_WINDOW, :] = uext_ref[ts:ts + MAX_WINDOW, :]

    L = GLA_BLOCK
    ri = lax.broadcasted_iota(jnp.int32, (L, L), 0)
    ci = lax.broadcasted_iota(jnp.int32, (L, L), 1)
    causal = ri >= ci
    tri = jnp.where(causal, 1.0, 0.0).astype(BF16)
    lane_head = lax.broadcasted_iota(jnp.int32, (L, D_GLA_K), 1) // GLA_DK
    st_row_head = lax.broadcasted_iota(jnp.int32, (D_GLA_V, D_GLA_K), 0) // GLA_DV
    st_col_head = lax.broadcasted_iota(jnp.int32, (D_GLA_V, D_GLA_K), 1) // GLA_DK
    st_mask = st_row_head == st_col_head
    ggla = ggla_ref[...]
    for nb in range(ts // L):
        rows = slice(nb * L, (nb + 1) * L)
        q = proj_ref[rows, COL_Q:COL_Q + D_GLA_K] * (GLA_DK ** -0.5)
        k = proj_ref[rows, COL_K:COL_K + D_GLA_K]
        v = proj_ref[rows, COL_V:COL_V + D_GLA_V].astype(BF16)
        r = proj_ref[rows, COL_R:COL_R + D_GLA_V]
        glow = proj_ref[rows, COL_G:COL_G + GATE_PAD].astype(BF16)
        logits = _dot(glow, wgu_ref[...]) + bg_ref[...]
        log_a = (jnp.minimum(logits, 0.0) - jnp.log1p(jnp.exp(-jnp.abs(logits)))) * (1.0 / GATE_TAU)
        la_hi = log_a.astype(BF16)
        la_lo = (log_a - la_hi.astype(F32)).astype(BF16)
        bc = _dot(tri, la_hi) + _dot(tri, la_lo)
        b_last = bc[L - 1:L, :]
        b_mid = bc[L // 2 - 1:L // 2, :]
        q_intra = q * jnp.exp(bc - b_mid)
        k_intra = (k * jnp.exp(b_mid - bc)).astype(BF16)
        q_inter = (q * jnp.exp(bc)).astype(BF16)
        k_hat = (k * jnp.exp(b_last - bc)).astype(BF16)

        q_stack = jnp.concatenate(
            [jnp.where(lane_head == hh, q_intra, 0.0) for hh in range(GLA_HEADS)], axis=0).astype(BF16)
        scores = lax.dot_general(q_stack, k_intra, (((1,), (1,)), ((), ())),
                                 preferred_element_type=F32)
        st = st_ref[...]
        o_inter = lax.dot_general(q_inter, st.astype(BF16), (((1,), (1,)), ((), ())),
                                  preferred_element_type=F32)
        kv_t = lax.dot_general(v, k_hat, (((0,), (0,)), ((), ())),
                               preferred_element_type=F32)
        st_ref[...] = st * jnp.exp(b_last) + jnp.where(st_mask, kv_t, 0.0)

        for hh in range(GLA_HEADS):
            vcols = slice(hh * GLA_DV, (hh + 1) * GLA_DV)
            a_h = jnp.where(causal, scores[hh * L:(hh + 1) * L, :], 0.0).astype(BF16)
            o_h = _dot(a_h, v[:, vcols]) + o_inter[:, vcols]
            o_h = _rmsnorm(o_h, ggla)
            r_h = r[:, vcols]
            o_h = o_h * (r_h * jax.nn.sigmoid(r_h))
            y_ref[rows, D_POOL + hh * GLA_DV:D_POOL + (hh + 1) * GLA_DV] = o_h.astype(BF16)

    mix = _dot(y_ref[...], wout_ref[...])
    o_ref[...] = x + _rmsnorm(mix, gpost_ref[...])


def _ffn_kernel(x_ref, gpre_ref, win_ref, cw_ref, cb_ref, wout_ref, gpost_ref, o_ref,
                tail_ref, act_ref):
    ts = FFN_TILE
    j = pl.program_id(1)

    @pl.when(j == 0)
    def _():
        tail_ref[...] = jnp.zeros_like(tail_ref)

    x = x_ref[...]
    h = _rmsnorm(x, gpre_ref[...]).astype(BF16)
    row = lax.broadcasted_iota(jnp.int32, (CONV_TAIL, FFN_CHUNK), 0)
    for c in range(D_FF // FFN_CHUNK):
        cols = slice(c * FFN_CHUNK, (c + 1) * FFN_CHUNK)
        vcols = slice(D_FF + c * FFN_CHUNK, D_FF + (c + 1) * FFN_CHUNK)
        gate = _dot(h, win_ref[:, cols])
        val = _dot(h, win_ref[:, vcols])
        prev = tail_ref[:, cols]
        tail_ref[:, cols] = gate[ts - CONV_TAIL:ts, :]
        p1 = prev[CONV_TAIL - 1:CONV_TAIL, :]
        p2 = prev[CONV_TAIL - 2:CONV_TAIL - 1, :]
        r1 = pltpu.roll(gate, 1, 0)
        r2 = pltpu.roll(gate, 2, 0)
        h1 = jnp.where(row == 0, p1, r1[:CONV_TAIL])
        h2 = jnp.where(row == 0, p2, jnp.where(row == 1, p1, r2[:CONV_TAIL]))
        g1 = jnp.concatenate([h1, r1[CONV_TAIL:]], axis=0)
        g2 = jnp.concatenate([h2, r2[CONV_TAIL:]], axis=0)
        cw = cw_ref[:, cols]
        gc = g2 * cw[0:1, :] + g1 * cw[1:2, :] + gate * cw[2:3, :] + cb_ref[:, cols]
        act = 0.5 * gc * (1.0 + lax.erf(gc * (1.0 / math.sqrt(2.0)))) * val
        act_ref[:, cols] = act.astype(BF16)
    ff = _dot(act_ref[...], wout_ref[...])
    o_ref[...] = x + _rmsnorm(ff, gpost_ref[...])


def _const_spec(shape):
    zeros = (0,) * len(shape)
    return pl.BlockSpec(shape, lambda b, j: zeros, pipeline_mode=pl.Buffered(1))


def _token_mixer(x, g_pre, w_in_p, w_pool, pool_scale, w_gu_p, b_gate, g_gla, w_out, g_post):
    B, S, D = x.shape
    ts = MIX_TILE
    tile_spec = pl.BlockSpec((None, ts, D), lambda b, j: (b, j, 0))
    return pl.pallas_call(
        _mixer_kernel,
        grid=(B, S // ts),
        in_specs=[
            tile_spec,
            _const_spec((1, D)),
            _const_spec((D, D_IN_PACKED)),
            _const_spec((len(POOL_WINDOWS), POOL_GROUP_DIM, POOL_GROUP_DIM)),
            _const_spec((1, D_POOL)),
            _const_spec((GATE_PAD, D_GLA_K)),
            _const_spec((1, D_GLA_K)),
            _const_spec((1, GLA_DV)),
            _const_spec((D, D)),
            _const_spec((1, D)),
        ],
        out_specs=tile_spec,
        out_shape=jax.ShapeDtypeStruct((B, S, D), F32),
        scratch_shapes=[
            pltpu.VMEM((ts, D_IN_PACKED), F32),
            pltpu.VMEM((ts + MAX_WINDOW, D_POOL), F32),
            pltpu.VMEM((D_GLA_V, D_GLA_K), F32),
            pltpu.VMEM((ts, D), BF16),
        ],
        compiler_params=pltpu.CompilerParams(
            dimension_semantics=("arbitrary", "arbitrary"),
            vmem_limit_bytes=VMEM_LIMIT_BYTES),
        name="token_mixer",
    )(x, g_pre, w_in_p, w_pool, pool_scale, w_gu_p, b_gate, g_gla, w_out, g_post)


def _conv_glu_ffn(x, g_pre, w_in, conv_w, conv_b, w_out, g_post):
    B, S, D = x.shape
    ts = FFN_TILE
    tile_spec = pl.BlockSpec((None, ts, D), lambda b, j: (b, j, 0))
    return pl.pallas_call(
        _ffn_kernel,
        grid=(B, S // ts),
        in_specs=[
            tile_spec,
            _const_spec((1, D)),
            _const_spec((D, 2 * D_FF)),
            _const_spec((3, D_FF)),
            _const_spec((1, D_FF)),
            _const_spec((D_FF, D)),
            _const_spec((1, D)),
        ],
        out_specs=tile_spec,
        out_shape=jax.ShapeDtypeStruct((B, S, D), F32),
        scratch_shapes=[
            pltpu.VMEM((CONV_TAIL, D_FF), F32),
            pltpu.VMEM((ts, D_FF), BF16),
        ],
        compiler_params=pltpu.CompilerParams(
            dimension_semantics=("arbitrary", "arbitrary"),
            vmem_limit_bytes=VMEM_LIMIT_BYTES),
        name="conv_glu_ffn",
    )(x, g_pre, w_in, conv_w, conv_b, w_out, g_post)


def _pack_w_in(w_in):
    c0 = D_POOL
    c1 = c0 + D_GLA_K
    c2 = c1 + D_GLA_K
    c3 = c2 + D_GLA_V
    c4 = c3 + GATE_RANK
    g_low = jnp.pad(w_in[:, c3:c4], ((0, 0), (0, GATE_PAD - GATE_RANK)))
    return jnp.concatenate([w_in[:, :c3], w_in[:, c4:], g_low], axis=1).astype(BF16)


def kernel(x, g_pre_mix, w_in, w_pool, pool_scale, w_gate_up, b_gate, g_gla_norm, w_out, g_post_mix,
           g_pre_ffn, w_ffn_in, conv_w, conv_b, w_ffn_out, g_post_ffn):
    row = lambda a: a.reshape(1, -1).astype(F32)
    w_gu_p = jnp.pad(w_gate_up, ((0, GATE_PAD - GATE_RANK), (0, 0))).astype(BF16)
    x1 = _token_mixer(x, row(g_pre_mix), _pack_w_in(w_in), w_pool.astype(BF16), row(pool_scale),
                      w_gu_p, row(b_gate), row(g_gla_norm), w_out.astype(BF16), row(g_post_mix))
    return _conv_glu_ffn(x1, row(g_pre_ffn), w_ffn_in.astype(BF16), conv_w.astype(F32),
                         row(conv_b), w_ffn_out.astype(BF16), row(g_post_ffn))
```

```python
import functools
import math

import jax
import jax.numpy as jnp
from jax import lax
from jax.experimental import pallas as pl
from jax.experimental.pallas import tpu as pltpu

D_MODEL = 1024
D_POOL = 512
POOL_WINDOWS = (2, 4, 8, 16)
POOL_GROUP_DIM = 128
MAX_WINDOW = 16
D_GLA_V = 512
GLA_HEADS = 4
GLA_DV = 128
GLA_DK = 64
D_GLA_K = 256
GATE_RANK = 16
GATE_TAU = 16.0
D_FF = 2816
EPS = 1e-6

LANES = 128
MXU_N = 256
GATE_PAD = LANES
COL_U = 0
COL_Q = COL_U + D_POOL
COL_K = COL_Q + D_GLA_K
COL_V = COL_K + D_GLA_K
COL_R = COL_V + D_GLA_V
COL_G = COL_R + D_GLA_V
D_IN_PACKED = COL_G + GATE_PAD
PROJ_PIECES = ((COL_U, D_POOL), (COL_Q, 2 * D_GLA_K), (COL_V, D_GLA_V + MXU_N),
               (COL_R + MXU_N, D_IN_PACKED - COL_R - MXU_N))

MIX_TILE = 512
GLA_BLOCK = 128
FFN_TILE = 512
FFN_CHUNK = 256
CONV_TAIL = 8

VMEM_LIMIT_BYTES = 56 * 1024 * 1024

F32 = jnp.float32
BF16 = jnp.bfloat16
_NT_DIMS = (((1,), (1,)), ((), ()))
_TN_DIMS = (((0,), (0,)), ((), ()))


def _rmsnorm(x, g):
    return x * lax.rsqrt(jnp.mean(x * x, axis=-1, keepdims=True) + EPS) * g


def _dot(a, b):
    return jnp.dot(a, b, preferred_element_type=F32)


def _mixer_kernel(xn_ref, xp_ref, gpre_ref, win_ref, wpool_ref, pscale_ref, wgu_ref, bg_ref,
                  ggla_ref, wout_ref, gpost_ref, o_ref, h_ref, proj_ref, uext_ref, d_ref, st_ref, y_ref,
                  mix_ref,
                  *, tiles_per_seq, n_tiles):
    ts = MIX_TILE
    L = GLA_BLOCK
    nblk = ts // L
    half = 2 * L
    i = pl.program_id(0)
    slot_a = lax.rem(i, 2)
    slot_b = 1 - slot_a
    j = lax.rem(jnp.maximum(i - 1, 0), tiles_per_seq)

    @pl.when(i == 0)
    def _():
        proj_ref[...] = jnp.zeros_like(proj_ref)
        y_ref[...] = jnp.zeros_like(y_ref)

    @pl.when(j == 0)
    def _():
        uext_ref[0:MAX_WINDOW, :] = jnp.zeros((MAX_WINDOW, D_POOL), F32)
        st_ref[...] = jnp.zeros_like(st_ref)

    def step(do_a, do_b):
        pnew = proj_ref.at[slot_a]
        pj = proj_ref.at[slot_b]
        y_new = y_ref.at[slot_b]
        y_old = y_ref.at[slot_a]
        vals = {}

        def prenorm():
            if do_a:
                h_ref[...] = _rmsnorm(xn_ref[...], gpre_ref[...]).astype(BF16)

        def proj_piece(p):
            if do_a:
                n0, w = PROJ_PIECES[p]
                pnew[:, n0:n0 + w] = jnp.dot(h_ref[...], win_ref[:, n0:n0 + w],
                                             preferred_element_type=F32)

        def out_proj():
            mix_ref[...] = jnp.dot(y_old[...], wout_ref[...], preferred_element_type=F32)

        def out_finish():
            o_ref[...] = xp_ref[...] + _rmsnorm(mix_ref[...], gpost_ref[...])

        def pool_sums():
            uext_ref[MAX_WINDOW:MAX_WINDOW + ts, :] = pj[:, COL_U:COL_U + D_POOL]
            row = lax.broadcasted_iota(jnp.int32, (ts, 1), 0)
            pos = (j * ts + row + 1).astype(F32)
            for g, w in enumerate(POOL_WINDOWS):
                cols = slice(g * POOL_GROUP_DIM, (g + 1) * POOL_GROUP_DIM)
                ext = uext_ref[:, cols]
                acc = ext
                span = 1
                while span < w:
                    acc = acc + pltpu.roll(acc, span, 0)
                    span *= 2
                wsum = acc[MAX_WINDOW:, :]
                d = wsum / jnp.minimum(pos, float(w)) - ext[MAX_WINDOW:, :]
                d_ref[:, cols] = d.astype(BF16)
            uext_ref[0:MAX_WINDOW, :] = uext_ref[ts:ts + MAX_WINDOW, :]

        def pool_maps():
            for pair in range(2):
                cols = slice(pair * MXU_N, (pair + 1) * MXU_N)
                yp = jnp.dot(d_ref[:, cols], wpool_ref[pair], preferred_element_type=F32)
                y_new[:, cols] = (yp * pscale_ref[:, cols]).astype(BF16)

        def gate_stage():
            glow = pj[:, COL_G:COL_G + GATE_PAD].astype(BF16)
            logits = jnp.dot(glow, wgu_ref[...], preferred_element_type=F32) + bg_ref[...]
            log_a = ((jnp.minimum(logits, 0.0) - jnp.log1p(jnp.exp(-jnp.abs(logits))))
                     * (1.0 / GATE_TAU))
            la_hi = log_a.astype(BF16)
            vals["la"] = (la_hi, (log_a - la_hi.astype(F32)).astype(BF16))

        def prefix_stage():
            la_hi, la_lo = vals["la"]
            ri = lax.broadcasted_iota(jnp.int32, (half, half), 0)
            ci = lax.broadcasted_iota(jnp.int32, (half, half), 1)
            blk_start = ri - lax.rem(ri, L)
            tri_bd = jnp.where(ci <= ri, jnp.where(ci >= blk_start, 1.0, 0.0), 0.0).astype(BF16)
            bcs = []
            for hb in range(ts // half):
                hrows = slice(hb * half, (hb + 1) * half)
                bc2 = (jnp.dot(tri_bd, la_hi[hrows, :], preferred_element_type=F32)
                       + jnp.dot(tri_bd, la_lo[hrows, :], preferred_element_type=F32))
                bcs += [bc2[0:L, :], bc2[L:half, :]]
            vals["bcs"] = bcs
            vals["st"] = st_ref[...]

        def scores_stage(nb):
            lane_first = lax.broadcasted_iota(jnp.int32, (L, 2 * GLA_DK), 1) < GLA_DK
            ri2 = lax.broadcasted_iota(jnp.int32, (L, 2 * L), 0)
            ci2 = lax.broadcasted_iota(jnp.int32, (L, 2 * L), 1)
            causal2 = ri2 >= lax.rem(ci2, L)
            st_row_head = lax.broadcasted_iota(jnp.int32, (D_GLA_V, D_GLA_K), 0) // GLA_DV
            st_col_head = lax.broadcasted_iota(jnp.int32, (D_GLA_V, D_GLA_K), 1) // GLA_DK
            st_mask = st_row_head == st_col_head
            rows = slice(nb * L, (nb + 1) * L)
            bc = vals["bcs"][nb]
            b_last = bc[L - 1:L, :]
            b_mid = bc[L // 2 - 1:L // 2, :]
            q = pj[rows, COL_Q:COL_Q + D_GLA_K] * (GLA_DK ** -0.5)
            k = pj[rows, COL_K:COL_K + D_GLA_K]
            q_intra = (q * jnp.exp(bc - b_mid)).astype(BF16)
            k_intra = k * jnp.exp(b_mid - bc)
            q_inter = (q * jnp.exp(bc)).astype(BF16)
            k_hat = (k * jnp.exp(b_last - bc)).astype(BF16)
            a_pairs = []
            for pair in range(GLA_HEADS // 2):
                kcols = slice(pair * 2 * GLA_DK, (pair + 1) * 2 * GLA_DK)
                k_p = k_intra[:, kcols]
                k_pair = jnp.concatenate([jnp.where(lane_first, k_p, 0.0),
                                          jnp.where(lane_first, 0.0, k_p)], axis=0).astype(BF16)
                s_pair = lax.dot_general(q_intra[:, kcols], k_pair, _NT_DIMS,
                                         preferred_element_type=F32)
                a_pairs.append(jnp.where(causal2, s_pair, 0.0).astype(BF16))
            v_blk = pj[rows, COL_V:COL_V + D_GLA_V].astype(BF16)
            kv_t = lax.dot_general(v_blk, k_hat, _TN_DIMS, preferred_element_type=F32)
            st = vals["st"]
            vals["st"] = st * jnp.exp(b_last) + jnp.where(st_mask, kv_t, 0.0)
            vals[("blk", nb)] = (a_pairs, q_inter, st.astype(BF16))

        def output_stage(nb):
            lane_low = lax.broadcasted_iota(jnp.int32, (L, MXU_N), 1) < GLA_DV
            ggla = ggla_ref[...]
            a_pairs, q_inter, st_b = vals[("blk", nb)]
            rows = slice(nb * L, (nb + 1) * L)
            for pair in range(GLA_HEADS // 2):
                pcols = slice(pair * MXU_N, (pair + 1) * MXU_N)
                kcols = slice(pair * 2 * GLA_DK, (pair + 1) * 2 * GLA_DK)
                v_pair = pj[rows, COL_V + pair * MXU_N:COL_V + (pair + 1) * MXU_N]
                v_bd = jnp.concatenate([jnp.where(lane_low, v_pair, 0.0),
                                        jnp.where(lane_low, 0.0, v_pair)], axis=0).astype(BF16)
                o_inter = lax.dot_general(q_inter[:, kcols], st_b[pcols, kcols], _NT_DIMS,
                                          preferred_element_type=F32)
                o_pair = jnp.dot(a_pairs[pair], v_bd, preferred_element_type=F32) + o_inter
                for hh in range(2):
                    hcols = slice(hh * GLA_DV, (hh + 1) * GLA_DV)
                    o_h = _rmsnorm(o_pair[:, hcols], ggla)
                    c0 = COL_R + pair * MXU_N + hh * GLA_DV
                    r_h = pj[rows, c0:c0 + GLA_DV]
                    o_h = o_h * (r_h * jax.nn.sigmoid(r_h))
                    y0 = D_POOL + pair * MXU_N + hh * GLA_DV
                    y_new[rows, y0:y0 + GLA_DV] = o_h.astype(BF16)

        if do_b:
            gate_stage()
        out_proj()
        prenorm()
        if do_b:
            pool_sums()
        out_finish()
        proj_piece(0)
        if do_b:
            prefix_stage()
        proj_piece(1)
        if do_b:
            pool_maps()
            for nb in range(nblk):
                scores_stage(nb)
            st_ref[...] = vals["st"]
        proj_piece(2)
        if do_b:
            for nb in range(nblk):
                output_stage(nb)
        proj_piece(3)

    @pl.when(i < n_tiles)
    def _():
        step(True, True)

    @pl.when(i == n_tiles)
    def _():
        step(False, True)

    @pl.when(i == n_tiles + 1)
    def _():
        step(False, False)


def _ffn_kernel(x_ref, gpre_ref, win_ref, cw_ref, cb_ref, wout_ref, gpost_ref, o_ref,
                tail_ref, act_ref):
    ts = FFN_TILE
    j = pl.program_id(1)

    @pl.when(j == 0)
    def _():
        tail_ref[...] = jnp.zeros_like(tail_ref)

    x = x_ref[...]
    h = _rmsnorm(x, gpre_ref[...]).astype(BF16)
    row = lax.broadcasted_iota(jnp.int32, (CONV_TAIL, FFN_CHUNK), 0)
    for c in range(D_FF // FFN_CHUNK):
        cols = slice(c * FFN_CHUNK, (c + 1) * FFN_CHUNK)
        vcols = slice(D_FF + c * FFN_CHUNK, D_FF + (c + 1) * FFN_CHUNK)
        gate = _dot(h, win_ref[:, cols])
        val = _dot(h, win_ref[:, vcols])
        prev = tail_ref[:, cols]
        tail_ref[:, cols] = gate[ts - CONV_TAIL:ts, :]
        p1 = prev[CONV_TAIL - 1:CONV_TAIL, :]
        p2 = prev[CONV_TAIL - 2:CONV_TAIL - 1, :]
        r1 = pltpu.roll(gate, 1, 0)
        r2 = pltpu.roll(gate, 2, 0)
        h1 = jnp.where(row == 0, p1, r1[:CONV_TAIL])
        h2 = jnp.where(row == 0, p2, jnp.where(row == 1, p1, r2[:CONV_TAIL]))
        g1 = jnp.concatenate([h1, r1[CONV_TAIL:]], axis=0)
        g2 = jnp.concatenate([h2, r2[CONV_TAIL:]], axis=0)
        cw = cw_ref[:, cols]
        gc = g2 * cw[0:1, :] + g1 * cw[1:2, :] + gate * cw[2:3, :] + cb_ref[:, cols]
        act = 0.5 * gc * (1.0 + lax.erf(gc * (1.0 / math.sqrt(2.0)))) * val
        act_ref[:, cols] = act.astype(BF16)
    ff = _dot(act_ref[...], wout_ref[...])
    o_ref[...] = x + _rmsnorm(ff, gpost_ref[...])


def _const_spec(shape, grid_rank):
    zeros = (0,) * len(shape)
    if grid_rank == 1:
        index_map = lambda i: zeros
    else:
        index_map = lambda b, j: zeros
    return pl.BlockSpec(shape, index_map, pipeline_mode=pl.Buffered(1))


def _token_mixer(x, g_pre, w_in_p, w_pool_p, pool_scale, w_gu_p, b_gate, g_gla, w_out, g_post):
    B, S, D = x.shape
    ts = MIX_TILE
    n_tiles = (B * S) // ts
    x2 = x.reshape(B * S, D)
    kern = functools.partial(_mixer_kernel, tiles_per_seq=S // ts, n_tiles=n_tiles)
    cs = lambda shape: _const_spec(shape, 1)
    out = pl.pallas_call(
        kern,
        grid=(n_tiles + 2,),
        in_specs=[
            pl.BlockSpec((ts, D), lambda i: (jnp.minimum(i, n_tiles - 1), 0)),
            pl.BlockSpec((ts, D), lambda i: (jnp.maximum(i - 2, 0), 0)),
            cs((1, D)),
            cs((D, D_IN_PACKED)),
            cs((2, MXU_N, MXU_N)),
            cs((1, D_POOL)),
            cs((GATE_PAD, D_GLA_K)),
            cs((1, D_GLA_K)),
            cs((1, GLA_DV)),
            cs((D, D)),
            cs((1, D)),
        ],
        out_specs=pl.BlockSpec((ts, D), lambda i: (jnp.maximum(i - 2, 0), 0)),
        out_shape=jax.ShapeDtypeStruct((B * S, D), F32),
        scratch_shapes=[
            pltpu.VMEM((ts, D), BF16),
            pltpu.VMEM((2, ts, D_IN_PACKED), F32),
            pltpu.VMEM((ts + MAX_WINDOW, D_POOL), F32),
            pltpu.VMEM((ts, D_POOL), BF16),
            pltpu.VMEM((D_GLA_V, D_GLA_K), F32),
            pltpu.VMEM((2, ts, D), BF16),
            pltpu.VMEM((ts, D), F32),
        ],
        compiler_params=pltpu.CompilerParams(
            dimension_semantics=("arbitrary",),
            vmem_limit_bytes=VMEM_LIMIT_BYTES),
        name="token_mixer",
    )(x2, x2, g_pre, w_in_p, w_pool_p, pool_scale, w_gu_p, b_gate, g_gla, w_out, g_post)
    return out.reshape(B, S, D)


def _conv_glu_ffn(x, g_pre, w_in, conv_w, conv_b, w_out, g_post):
    B, S, D = x.shape
    ts = FFN_TILE
    tile_spec = pl.BlockSpec((None, ts, D), lambda b, j: (b, j, 0))
    cs = lambda shape: _const_spec(shape, 2)
    return pl.pallas_call(
        _ffn_kernel,
        grid=(B, S // ts),
        in_specs=[
            tile_spec,
            cs((1, D)),
            cs((D, 2 * D_FF)),
            cs((3, D_FF)),
            cs((1, D_FF)),
            cs((D_FF, D)),
            cs((1, D)),
        ],
        out_specs=tile_spec,
        out_shape=jax.ShapeDtypeStruct((B, S, D), F32),
        scratch_shapes=[
            pltpu.VMEM((CONV_TAIL, D_FF), F32),
            pltpu.VMEM((ts, D_FF), BF16),
        ],
        compiler_params=pltpu.CompilerParams(
            dimension_semantics=("arbitrary", "arbitrary"),
            vmem_limit_bytes=VMEM_LIMIT_BYTES),
        name="conv_glu_ffn",
    )(x, g_pre, w_in, conv_w, conv_b, w_out, g_post)


def _pack_w_in(w_in):
    c0 = D_POOL
    c1 = c0 + D_GLA_K
    c2 = c1 + D_GLA_K
    c3 = c2 + D_GLA_V
    c4 = c3 + GATE_RANK
    g_low = jnp.pad(w_in[:, c3:c4], ((0, 0), (0, GATE_PAD - GATE_RANK)))
    return jnp.concatenate([w_in[:, :c3], w_in[:, c4:], g_low], axis=1).astype(BF16)


def _pack_w_pool(w_pool):
    z = jnp.zeros((POOL_GROUP_DIM, POOL_GROUP_DIM), w_pool.dtype)
    pairs = [jnp.block([[w_pool[2 * p], z], [z, w_pool[2 * p + 1]]]) for p in range(2)]
    return jnp.stack(pairs).astype(BF16)


def kernel(x, g_pre_mix, w_in, w_pool, pool_scale, w_gate_up, b_gate, g_gla_norm, w_out, g_post_mix,
           g_pre_ffn, w_ffn_in, conv_w, conv_b, w_ffn_out, g_post_ffn):
    row = lambda a: a.reshape(1, -1).astype(F32)
    w_gu_p = jnp.pad(w_gate_up, ((0, GATE_PAD - GATE_RANK), (0, 0))).astype(BF16)
    x1 = _token_mixer(x, row(g_pre_mix), _pack_w_in(w_in), _pack_w_pool(w_pool), row(pool_scale),
                      w_gu_p, row(b_gate), row(g_gla_norm), w_out.astype(BF16), row(g_post_mix))
    return _conv_glu_ffn(x1, row(g_pre_ffn), w_ffn_in.astype(BF16), conv_w.astype(F32),
                         row(conv_b), w_ffn_out.astype(BF16), row(g_post_ffn))
```

```python
import functools
import math

import jax
import jax.numpy as jnp
from jax import lax
from jax.experimental import pallas as pl
from jax.experimental.pallas import tpu as pltpu

D_MODEL = 1024
D_POOL = 512
POOL_WINDOWS = (2, 4, 8, 16)
POOL_GROUP_DIM = 128
MAX_WINDOW = 16
D_GLA_V = 512
GLA_HEADS = 4
GLA_DV = 128
GLA_DK = 64
D_GLA_K = 256
GATE_RANK = 16
GATE_TAU = 16.0
D_FF = 2816
EPS = 1e-6

LANES = 128
MXU_N = 256
GATE_PAD = LANES
COL_U = 0
COL_Q = COL_U + D_POOL
COL_K = COL_Q + D_GLA_K
COL_V = COL_K + D_GLA_K
COL_R = COL_V + D_GLA_V
COL_G = COL_R + D_GLA_V
D_IN_PACKED = COL_G + GATE_PAD
PROJ_PIECES = ((COL_U, D_POOL), (COL_Q, 2 * D_GLA_K), (COL_V, D_GLA_V + MXU_N),
               (COL_R + MXU_N, D_IN_PACKED - COL_R - MXU_N))

TILE = 256
GLA_BLOCK = 128
FFN_CHUNK = 256
CONV_TAIL = 8

VMEM_LIMIT_BYTES = 56 * 1024 * 1024

F32 = jnp.float32
BF16 = jnp.bfloat16
_NT_DIMS = (((1,), (1,)), ((), ()))
_TN_DIMS = (((0,), (0,)), ((), ()))


def _rmsnorm(x, g):
    return x * lax.rsqrt(jnp.mean(x * x, axis=-1, keepdims=True) + EPS) * g


def _layer_kernel(xn_ref, xp_ref, gpre_ref, win_ref, wpool_ref, pscale_ref, wgu_ref, bg_ref,
                  ggla_ref, wout_ref, gpost_ref, gpre2_ref, wfin_ref, cw_ref, cb_ref, wfout_ref,
                  gpost2_ref, o_ref, h_ref, proj_ref, uext_ref, d_ref, st_ref, y_ref, mix_ref,
                  x1_ref, h2_ref, tail_ref, act_ref, *, tiles_per_seq, n_tiles):
    ts = TILE
    L = GLA_BLOCK
    nblk = ts // L
    half = 2 * L
    i = pl.program_id(0)
    slot_a = lax.rem(i, 2)
    slot_b = 1 - slot_a
    j = lax.rem(jnp.maximum(i - 1, 0), tiles_per_seq)
    jf = lax.rem(jnp.maximum(i - 2, 0), tiles_per_seq)

    @pl.when(j == 0)
    def _():
        uext_ref[0:MAX_WINDOW, :] = jnp.zeros((MAX_WINDOW, D_POOL), F32)
        st_ref[...] = jnp.zeros_like(st_ref)

    @pl.when(jf == 0)
    def _():
        tail_ref[...] = jnp.zeros_like(tail_ref)

    def step(do_a, do_b, do_cf):
        pnew = proj_ref.at[slot_a]
        pj = proj_ref.at[slot_b]
        y_new = y_ref.at[slot_b]
        y_old = y_ref.at[slot_a]
        vals = {}

        def prenorm():
            h_ref[...] = _rmsnorm(xn_ref[...], gpre_ref[...]).astype(BF16)

        def proj_piece(p):
            n0, w = PROJ_PIECES[p]
            pnew[:, n0:n0 + w] = jnp.dot(h_ref[...], win_ref[:, n0:n0 + w],
                                         preferred_element_type=F32)

        def out_proj():
            mix_ref[...] = jnp.dot(y_old[...], wout_ref[...], preferred_element_type=F32)

        def out_finish():
            x1 = xp_ref[...] + _rmsnorm(mix_ref[...], gpost_ref[...])
            x1_ref[...] = x1
            h2_ref[...] = _rmsnorm(x1, gpre2_ref[...]).astype(BF16)

        def ffn_chunk(c):
            row = lax.broadcasted_iota(jnp.int32, (CONV_TAIL, FFN_CHUNK), 0)
            cols = slice(c * FFN_CHUNK, (c + 1) * FFN_CHUNK)
            vcols = slice(D_FF + c * FFN_CHUNK, D_FF + (c + 1) * FFN_CHUNK)
            gate = jnp.dot(h2_ref[...], wfin_ref[:, cols], preferred_element_type=F32)
            val = jnp.dot(h2_ref[...], wfin_ref[:, vcols], preferred_element_type=F32)
            prev = tail_ref[:, cols]
            tail_ref[:, cols] = gate[ts - CONV_TAIL:ts, :]
            p1 = prev[CONV_TAIL - 1:CONV_TAIL, :]
            p2 = prev[CONV_TAIL - 2:CONV_TAIL - 1, :]
            r1 = pltpu.roll(gate, 1, 0)
            r2 = pltpu.roll(gate, 2, 0)
            h1 = jnp.where(row == 0, p1, r1[:CONV_TAIL])
            h2 = jnp.where(row == 0, p2, jnp.where(row == 1, p1, r2[:CONV_TAIL]))
            g1 = jnp.concatenate([h1, r1[CONV_TAIL:]], axis=0)
            g2 = jnp.concatenate([h2, r2[CONV_TAIL:]], axis=0)
            cw = cw_ref[:, cols]
            gc = g2 * cw[0:1, :] + g1 * cw[1:2, :] + gate * cw[2:3, :] + cb_ref[:, cols]
            act = 0.5 * gc * (1.0 + lax.erf(gc * (1.0 / math.sqrt(2.0)))) * val
            act_ref[:, cols] = act.astype(BF16)

        def ffn_down():
            vals["ff"] = jnp.dot(act_ref[...], wfout_ref[...], preferred_element_type=F32)

        def ffn_finish():
            o_ref[...] = x1_ref[...] + _rmsnorm(vals["ff"], gpost2_ref[...])

        def pool_sums():
            uext_ref[MAX_WINDOW:MAX_WINDOW + ts, :] = pj[:, COL_U:COL_U + D_POOL]
            row = lax.broadcasted_iota(jnp.int32, (MAX_WINDOW, 1), 0)
            pos = (j * ts + row + 1).astype(F32)
            for g, w in enumerate(POOL_WINDOWS):
                cols = slice(g * POOL_GROUP_DIM, (g + 1) * POOL_GROUP_DIM)
                ext = uext_ref[:, cols]
                acc = ext
                span = 1
                while span < w:
                    acc = acc + pltpu.roll(acc, span, 0)
                    span *= 2
                head = acc[MAX_WINDOW:2 * MAX_WINDOW, :] / jnp.minimum(pos, float(w))
                body = acc[2 * MAX_WINDOW:, :] * (1.0 / w)
                d = jnp.concatenate([head, body], axis=0) - ext[MAX_WINDOW:, :]
                d_ref[:, cols] = d.astype(BF16)
            uext_ref[0:MAX_WINDOW, :] = uext_ref[ts:ts + MAX_WINDOW, :]

        def pool_maps():
            for pair in range(2):
                cols = slice(pair * MXU_N, (pair + 1) * MXU_N)
                yp = jnp.dot(d_ref[:, cols], wpool_ref[pair], preferred_element_type=F32)
                y_new[:, cols] = (yp * pscale_ref[:, cols]).astype(BF16)

        def gate_stage():
            glow = pj[:, COL_G:COL_G + GATE_PAD].astype(BF16)
            logits = jnp.dot(glow, wgu_ref[...], preferred_element_type=F32) + bg_ref[...]
            log_a = ((jnp.minimum(logits, 0.0) - jnp.log(1.0 + jnp.exp(-jnp.abs(logits))))
                     * (1.0 / GATE_TAU))
            la_hi = log_a.astype(BF16)
            vals["la"] = (la_hi, (log_a - la_hi.astype(F32)).astype(BF16))

        def prefix_stage():
            la_hi, la_lo = vals["la"]
            ri = lax.broadcasted_iota(jnp.int32, (half, half), 0)
            ci = lax.broadcasted_iota(jnp.int32, (half, half), 1)
            blk_start = ri - lax.rem(ri, L)
            tri_bd = jnp.where(ci <= ri, jnp.where(ci >= blk_start, 1.0, 0.0), 0.0).astype(BF16)
            bcs = []
            for hb in range(ts // half):
                hrows = slice(hb * half, (hb + 1) * half)
                bc2 = (jnp.dot(tri_bd, la_hi[hrows, :], preferred_element_type=F32)
                       + jnp.dot(tri_bd, la_lo[hrows, :], preferred_element_type=F32))
                bcs += [bc2[0:L, :], bc2[L:half, :]]
            vals["bcs"] = bcs
            vals["st"] = [st_ref[0], st_ref[1]]

        def scores_stage(nb):
            lane_first = lax.broadcasted_iota(jnp.int32, (L, 2 * GLA_DK), 1) < GLA_DK
            ri2 = lax.broadcasted_iota(jnp.int32, (L, 2 * L), 0)
            ci2 = lax.broadcasted_iota(jnp.int32, (L, 2 * L), 1)
            causal2 = ri2 >= lax.rem(ci2, L)
            st_row_first = lax.broadcasted_iota(jnp.int32, (2 * GLA_DV, 2 * GLA_DK), 0) < GLA_DV
            st_col_first = lax.broadcasted_iota(jnp.int32, (2 * GLA_DV, 2 * GLA_DK), 1) < GLA_DK
            st_mask = st_row_first == st_col_first
            rows = slice(nb * L, (nb + 1) * L)
            bc = vals["bcs"][nb]
            b_last = bc[L - 1:L, :]
            b_mid = bc[L // 2 - 1:L // 2, :]
            q = pj[rows, COL_Q:COL_Q + D_GLA_K] * (GLA_DK ** -0.5)
            k = pj[rows, COL_K:COL_K + D_GLA_K]
            e_fwd = jnp.exp(bc - b_mid)
            q_mid = q * e_fwd
            k_intra = k * (1.0 / e_fwd)
            q_intra = q_mid.astype(BF16)
            q_inter = (q_mid * jnp.exp(b_mid)).astype(BF16)
            k_hat = (k_intra * jnp.exp(b_last - b_mid)).astype(BF16)
            a_pairs = []
            for pair in range(GLA_HEADS // 2):
                kcols = slice(pair * 2 * GLA_DK, (pair + 1) * 2 * GLA_DK)
                k_p = k_intra[:, kcols]
                k_pair = jnp.concatenate([jnp.where(lane_first, k_p, 0.0),
                                          jnp.where(lane_first, 0.0, k_p)], axis=0).astype(BF16)
                s_pair = lax.dot_general(q_intra[:, kcols], k_pair, _NT_DIMS,
                                         preferred_element_type=F32)
                a_pairs.append(jnp.where(causal2, s_pair, 0.0).astype(BF16))
            v_blk = pj[rows, COL_V:COL_V + D_GLA_V].astype(BF16)
            kv_t = lax.dot_general(v_blk, k_hat, _TN_DIMS, preferred_element_type=F32)
            decay = jnp.exp(b_last)
            st_in = vals["st"]
            st_out = []
            for pair in range(GLA_HEADS // 2):
                pcols = slice(pair * MXU_N, (pair + 1) * MXU_N)
                kcols = slice(pair * 2 * GLA_DK, (pair + 1) * 2 * GLA_DK)
                st_out.append(st_in[pair] * decay[:, kcols]
                              + jnp.where(st_mask, kv_t[pcols, kcols], 0.0))
            vals["st"] = st_out
            vals[("blk", nb)] = (a_pairs, q_inter, [st_p.astype(BF16) for st_p in st_in])

        def output_stage(nb):
            lane_low = lax.broadcasted_iota(jnp.int32, (L, MXU_N), 1) < GLA_DV
            ggla = ggla_ref[...]
            a_pairs, q_inter, st_b = vals[("blk", nb)]
            rows = slice(nb * L, (nb + 1) * L)
            for pair in range(GLA_HEADS // 2):
                pcols = slice(pair * MXU_N, (pair + 1) * MXU_N)
                kcols = slice(pair * 2 * GLA_DK, (pair + 1) * 2 * GLA_DK)
                v_pair = pj[rows, COL_V + pair * MXU_N:COL_V + (pair + 1) * MXU_N]
                v_bd = jnp.concatenate([jnp.where(lane_low, v_pair, 0.0),
                                        jnp.where(lane_low, 0.0, v_pair)], axis=0).astype(BF16)
                o_inter = lax.dot_general(q_inter[:, kcols], st_b[pair], _NT_DIMS,
                                          preferred_element_type=F32)
                o_pair = jnp.dot(a_pairs[pair], v_bd, preferred_element_type=F32) + o_inter
                for hh in range(2):
                    hcols = slice(hh * GLA_DV, (hh + 1) * GLA_DV)
                    o_h = _rmsnorm(o_pair[:, hcols], ggla)
                    c0 = COL_R + pair * MXU_N + hh * GLA_DV
                    r_h = pj[rows, c0:c0 + GLA_DV]
                    o_h = o_h * (r_h * jax.nn.sigmoid(r_h))
                    y0 = D_POOL + pair * MXU_N + hh * GLA_DV
                    y_new[rows, y0:y0 + GLA_DV] = o_h.astype(BF16)

        if do_b:
            gate_stage()
        if do_cf:
            out_proj()
        if do_a:
            prenorm()
        if do_b:
            pool_sums()
        if do_a:
            proj_piece(0)
        if do_cf:
            out_finish()
        if do_b:
            prefix_stage()
        if do_a:
            proj_piece(1)
        if do_cf:
            ffn_chunk(0)
        if do_b:
            pool_maps()
            for nb in range(nblk):
                scores_stage(nb)
            st_ref[0] = vals["st"][0]
            st_ref[1] = vals["st"][1]
        if do_cf:
            ffn_chunk(1)
            ffn_chunk(2)
            ffn_chunk(3)
        if do_b:
            for nb in range(nblk):
                output_stage(nb)
        if do_cf:
            for c in range(4, D_FF // FFN_CHUNK):
                ffn_chunk(c)
        if do_a:
            proj_piece(2)
        if do_cf:
            ffn_down()
        if do_a:
            proj_piece(3)
        if do_cf:
            ffn_finish()

    @pl.when(i == 0)
    def _():
        step(True, False, False)

    @pl.when(i == 1)
    def _():
        step(True, True, False)

    @pl.when(jnp.logical_and(i >= 2, i < n_tiles))
    def _():
        step(True, True, True)

    @pl.when(i == n_tiles)
    def _():
        step(False, True, True)

    @pl.when(i == n_tiles + 1)
    def _():
        step(False, False, True)


def _const_spec(shape):
    zeros = (0,) * len(shape)
    return pl.BlockSpec(shape, lambda i: zeros, pipeline_mode=pl.Buffered(1))


def _decoder_layer(x, g_pre, w_in_p, w_pool_p, pool_scale, w_gu_p, b_gate, g_gla, w_out, g_post,
                   g_pre2, w_ffn_in, conv_w, conv_b, w_ffn_out, g_post2):
    B, S, D = x.shape
    ts = TILE
    n_tiles = (B * S) // ts
    x2 = x.reshape(B * S, D)
    kern = functools.partial(_layer_kernel, tiles_per_seq=S // ts, n_tiles=n_tiles)
    cs = _const_spec
    out = pl.pallas_call(
        kern,
        grid=(n_tiles + 2,),
        in_specs=[
            pl.BlockSpec((ts, D), lambda i: (jnp.minimum(i, n_tiles - 1), 0)),
            pl.BlockSpec((ts, D), lambda i: (jnp.maximum(i - 2, 0), 0)),
            cs((1, D)),
            cs((D, D_IN_PACKED)),
            cs((2, MXU_N, MXU_N)),
            cs((1, D_POOL)),
            cs((GATE_PAD, D_GLA_K)),
            cs((1, D_GLA_K)),
            cs((1, GLA_DV)),
            cs((D, D)),
            cs((1, D)),
            cs((1, D)),
            cs((D, 2 * D_FF)),
            cs((3, D_FF)),
            cs((1, D_FF)),
            cs((D_FF, D)),
            cs((1, D)),
        ],
        out_specs=pl.BlockSpec((ts, D), lambda i: (jnp.maximum(i - 2, 0), 0)),
        out_shape=jax.ShapeDtypeStruct((B * S, D), F32),
        scratch_shapes=[
            pltpu.VMEM((ts, D), BF16),
            pltpu.VMEM((2, ts, D_IN_PACKED), F32),
            pltpu.VMEM((ts + MAX_WINDOW, D_POOL), F32),
            pltpu.VMEM((ts, D_POOL), BF16),
            pltpu.VMEM((GLA_HEADS // 2, 2 * GLA_DV, 2 * GLA_DK), F32),
            pltpu.VMEM((2, ts, D), BF16),
            pltpu.VMEM((ts, D), F32),
            pltpu.VMEM((ts, D), F32),
            pltpu.VMEM((ts, D), BF16),
            pltpu.VMEM((CONV_TAIL, D_FF), F32),
            pltpu.VMEM((ts, D_FF), BF16),
        ],
        compiler_params=pltpu.CompilerParams(
            dimension_semantics=("arbitrary",),
            vmem_limit_bytes=VMEM_LIMIT_BYTES),
        name="decoder_layer",
    )(x2, x2, g_pre, w_in_p, w_pool_p, pool_scale, w_gu_p, b_gate, g_gla, w_out, g_post,
      g_pre2, w_ffn_in, conv_w, conv_b, w_ffn_out, g_post2)
    return out.reshape(B, S, D)


def _pack_w_in(w_in):
    c0 = D_POOL
    c1 = c0 + D_GLA_K
    c2 = c1 + D_GLA_K
    c3 = c2 + D_GLA_V
    c4 = c3 + GATE_RANK
    g_low = jnp.pad(w_in[:, c3:c4], ((0, 0), (0, GATE_PAD - GATE_RANK)))
    return jnp.concatenate([w_in[:, :c3], w_in[:, c4:], g_low], axis=1).astype(BF16)


def _pack_w_pool(w_pool):
    z = jnp.zeros((POOL_GROUP_DIM, POOL_GROUP_DIM), w_pool.dtype)
    pairs = [jnp.block([[w_pool[2 * p], z], [z, w_pool[2 * p + 1]]]) for p in range(2)]
    return jnp.stack(pairs).astype(BF16)


def kernel(x, g_pre_mix, w_in, w_pool, pool_scale, w_gate_up, b_gate, g_gla_norm, w_out, g_post_mix,
           g_pre_ffn, w_ffn_in, conv_w, conv_b, w_ffn_out, g_post_ffn):
    row = lambda a: a.reshape(1, -1).astype(F32)
    w_gu_p = jnp.pad(w_gate_up, ((0, GATE_PAD - GATE_RANK), (0, 0))).astype(BF16)
    return _decoder_layer(
        x, row(g_pre_mix), _pack_w_in(w_in), _pack_w_pool(w_pool), row(pool_scale), w_gu_p,
        row(b_gate), row(g_gla_norm), w_out.astype(BF16), row(g_post_mix), row(g_pre_ffn),
        w_ffn_in.astype(BF16), conv_w.astype(F32), row(conv_b), w_ffn_out.astype(BF16),
        row(g_post_ffn))
```

```python
import functools
import math

import jax
import jax.numpy as jnp
from jax import lax
from jax.experimental import pallas as pl
from jax.experimental.pallas import tpu as pltpu

D_MODEL = 1024
D_POOL = 512
POOL_WINDOWS = (2, 4, 8, 16)
POOL_GROUP_DIM = 128
MAX_WINDOW = 16
D_GLA_V = 512
GLA_HEADS = 4
GLA_DV = 128
GLA_DK = 64
D_GLA_K = 256
GATE_RANK = 16
GATE_TAU = 16.0
D_FF = 2816
EPS = 1e-6

LANES = 128
MXU_N = 256
GATE_PAD = LANES
D_IN = D_POOL + 2 * D_GLA_K + 2 * D_GLA_V + GATE_RANK
COL_U = 0
COL_Q = COL_U + D_POOL
COL_K = COL_Q + D_GLA_K
COL_V = COL_K + D_GLA_K
COL_G = COL_V + D_GLA_V
COL_R = COL_G + GATE_PAD
D_PROJ = COL_R + D_GLA_V
W_IN_R = COL_G + GATE_RANK
PROJ_PIECES = ((COL_U, D_POOL), (COL_Q, 2 * D_GLA_K), (COL_V, D_GLA_V + GATE_PAD))

TILE = 256
GLA_BLOCK = 128
FFN_CHUNK = 256
CONV_TAIL = 8

VMEM_LIMIT_BYTES = 56 * 1024 * 1024

F32 = jnp.float32
BF16 = jnp.bfloat16
_NT_DIMS = (((1,), (1,)), ((), ()))
_TN_DIMS = (((0,), (0,)), ((), ()))


def _rmsnorm(x, g):
    return x * lax.rsqrt(jnp.mean(x * x, axis=-1, keepdims=True) + EPS) * g


def _layer_kernel(xn_ref, xp_ref, gpre_ref, win_ref, wr_ref, wpool_ref, pscale_ref, wgu_ref, bg_ref,
                  ggla_ref, wout_ref, gpost_ref, gpre2_ref, wfin_ref, cw_ref, cb_ref, wfout_ref,
                  gpost2_ref, o_ref, h_ref, proj_ref, uext_ref, d_ref, st_ref, y_ref, mix_ref,
                  x1_ref, h2_ref, tail_ref, act_ref, *, tiles_per_seq, n_tiles):
    ts = TILE
    L = GLA_BLOCK
    nblk = ts // L
    half = 2 * L
    i = pl.program_id(0)
    slot_a = lax.rem(i, 2)
    slot_b = 1 - slot_a
    j = lax.rem(jnp.maximum(i - 1, 0), tiles_per_seq)
    jf = lax.rem(jnp.maximum(i - 2, 0), tiles_per_seq)

    @pl.when(j == 0)
    def _():
        uext_ref[0:MAX_WINDOW, :] = jnp.zeros((MAX_WINDOW, D_POOL), F32)
        st_ref[...] = jnp.zeros_like(st_ref)

    @pl.when(jf == 0)
    def _():
        tail_ref[...] = jnp.zeros_like(tail_ref)

    def step(do_a, do_b, do_cf):
        pnew = proj_ref.at[slot_a]
        pj = proj_ref.at[slot_b]
        y_new = y_ref.at[slot_b]
        y_old = y_ref.at[slot_a]
        vals = {}

        def prenorm():
            h_ref[...] = _rmsnorm(xn_ref[...], gpre_ref[...]).astype(BF16)

        def proj_piece(p):
            n0, w = PROJ_PIECES[p]
            pnew[:, n0:n0 + w] = jnp.dot(h_ref[...], win_ref[:, n0:n0 + w],
                                         preferred_element_type=F32)

        def proj_r():
            pnew[:, COL_R:COL_R + D_GLA_V] = jnp.dot(h_ref[...], wr_ref[...],
                                                     preferred_element_type=F32)

        def out_proj():
            mix_ref[...] = jnp.dot(y_old[...], wout_ref[...], preferred_element_type=F32)

        def out_finish():
            x1 = xp_ref[...] + _rmsnorm(mix_ref[...], gpost_ref[...])
            x1_ref[...] = x1
            h2_ref[...] = _rmsnorm(x1, gpre2_ref[...]).astype(BF16)

        def ffn_chunk(c):
            row = lax.broadcasted_iota(jnp.int32, (CONV_TAIL, FFN_CHUNK), 0)
            cols = slice(c * FFN_CHUNK, (c + 1) * FFN_CHUNK)
            vcols = slice(D_FF + c * FFN_CHUNK, D_FF + (c + 1) * FFN_CHUNK)
            gate = jnp.dot(h2_ref[...], wfin_ref[:, cols], preferred_element_type=F32)
            val = jnp.dot(h2_ref[...], wfin_ref[:, vcols], preferred_element_type=F32)
            prev = tail_ref[:, cols]
            tail_ref[:, cols] = gate[ts - CONV_TAIL:ts, :]
            p1 = prev[CONV_TAIL - 1:CONV_TAIL, :]
            p2 = prev[CONV_TAIL - 2:CONV_TAIL - 1, :]
            r1 = pltpu.roll(gate, 1, 0)
            r2 = pltpu.roll(gate, 2, 0)
            h1 = jnp.where(row == 0, p1, r1[:CONV_TAIL])
            h2 = jnp.where(row == 0, p2, jnp.where(row == 1, p1, r2[:CONV_TAIL]))
            g1 = jnp.concatenate([h1, r1[CONV_TAIL:]], axis=0)
            g2 = jnp.concatenate([h2, r2[CONV_TAIL:]], axis=0)
            cw = cw_ref[:, cols]
            gc = g2 * cw[0:1, :] + g1 * cw[1:2, :] + gate * cw[2:3, :] + cb_ref[:, cols]
            act = 0.5 * gc * (1.0 + lax.erf(gc * (1.0 / math.sqrt(2.0)))) * val
            act_ref[:, cols] = act.astype(BF16)

        def ffn_down():
            vals["ff"] = jnp.dot(act_ref[...], wfout_ref[...], preferred_element_type=F32)

        def ffn_finish():
            o_ref[...] = x1_ref[...] + _rmsnorm(vals["ff"], gpost2_ref[...])

        def pool_sums():
            uext_ref[MAX_WINDOW:MAX_WINDOW + ts, :] = pj[:, COL_U:COL_U + D_POOL]
            row = lax.broadcasted_iota(jnp.int32, (MAX_WINDOW, 1), 0)
            pos = (j * ts + row + 1).astype(F32)
            for g, w in enumerate(POOL_WINDOWS):
                cols = slice(g * POOL_GROUP_DIM, (g + 1) * POOL_GROUP_DIM)
                ext = uext_ref[:, cols]
                acc = ext
                span = 1
                while span < w:
                    acc = acc + pltpu.roll(acc, span, 0)
                    span *= 2
                head = acc[MAX_WINDOW:2 * MAX_WINDOW, :] / jnp.minimum(pos, float(w))
                body = acc[2 * MAX_WINDOW:, :] * (1.0 / w)
                d = jnp.concatenate([head, body], axis=0) - ext[MAX_WINDOW:, :]
                d_ref[:, cols] = d.astype(BF16)
            uext_ref[0:MAX_WINDOW, :] = uext_ref[ts:ts + MAX_WINDOW, :]

        def pool_maps():
            for pair in range(2):
                cols = slice(pair * MXU_N, (pair + 1) * MXU_N)
                yp = jnp.dot(d_ref[:, cols], wpool_ref[pair], preferred_element_type=F32)
                y_new[:, cols] = (yp * pscale_ref[:, cols]).astype(BF16)

        def gate_stage():
            glow = pj[:, COL_G:COL_G + GATE_PAD].astype(BF16)
            logits = jnp.dot(glow, wgu_ref[...], preferred_element_type=F32) + bg_ref[...]
            log_a = ((jnp.minimum(logits, 0.0) - jnp.log(1.0 + jnp.exp(-jnp.abs(logits))))
                     * (1.0 / GATE_TAU))
            la_hi = log_a.astype(BF16)
            vals["la"] = (la_hi, (log_a - la_hi.astype(F32)).astype(BF16))

        def prefix_stage():
            la_hi, la_lo = vals["la"]
            ri = lax.broadcasted_iota(jnp.int32, (half, half), 0)
            ci = lax.broadcasted_iota(jnp.int32, (half, half), 1)
            blk_start = ri - lax.rem(ri, L)
            tri_bd = jnp.where(ci <= ri, jnp.where(ci >= blk_start, 1.0, 0.0), 0.0).astype(BF16)
            bcs = []
            for hb in range(ts // half):
                hrows = slice(hb * half, (hb + 1) * half)
                bc2 = (jnp.dot(tri_bd, la_hi[hrows, :], preferred_element_type=F32)
                       + jnp.dot(tri_bd, la_lo[hrows, :], preferred_element_type=F32))
                bcs += [bc2[0:L, :], bc2[L:half, :]]
            vals["bcs"] = bcs
            vals["st"] = [st_ref[0], st_ref[1]]

        def scores_stage(nb):
            lane_first = lax.broadcasted_iota(jnp.int32, (L, 2 * GLA_DK), 1) < GLA_DK
            ri2 = lax.broadcasted_iota(jnp.int32, (L, 2 * L), 0)
            ci2 = lax.broadcasted_iota(jnp.int32, (L, 2 * L), 1)
            causal2 = ri2 >= lax.rem(ci2, L)
            st_row_first = lax.broadcasted_iota(jnp.int32, (2 * GLA_DV, 2 * GLA_DK), 0) < GLA_DV
            st_col_first = lax.broadcasted_iota(jnp.int32, (2 * GLA_DV, 2 * GLA_DK), 1) < GLA_DK
            st_mask = st_row_first == st_col_first
            rows = slice(nb * L, (nb + 1) * L)
            bc = vals["bcs"][nb]
            b_last = bc[L - 1:L, :]
            b_mid = bc[L // 2 - 1:L // 2, :]
            q = pj[rows, COL_Q:COL_Q + D_GLA_K] * (GLA_DK ** -0.5)
            k = pj[rows, COL_K:COL_K + D_GLA_K]
            e_fwd = jnp.exp(bc - b_mid)
            q_mid = q * e_fwd
            k_intra = k * (1.0 / e_fwd)
            q_intra = q_mid.astype(BF16)
            q_inter = (q_mid * jnp.exp(b_mid)).astype(BF16)
            k_hat = (k_intra * jnp.exp(b_last - b_mid)).astype(BF16)
            a_pairs = []
            for pair in range(GLA_HEADS // 2):
                kcols = slice(pair * 2 * GLA_DK, (pair + 1) * 2 * GLA_DK)
                k_p = k_intra[:, kcols]
                k_pair = jnp.concatenate([jnp.where(lane_first, k_p, 0.0),
                                          jnp.where(lane_first, 0.0, k_p)], axis=0).astype(BF16)
                s_pair = lax.dot_general(q_intra[:, kcols], k_pair, _NT_DIMS,
                                         preferred_element_type=F32)
                a_pairs.append(jnp.where(causal2, s_pair, 0.0).astype(BF16))
            v_blk = pj[rows, COL_V:COL_V + D_GLA_V].astype(BF16)
            kv_t = lax.dot_general(v_blk, k_hat, _TN_DIMS, preferred_element_type=F32)
            decay = jnp.exp(b_last)
            st_in = vals["st"]
            st_out = []
            for pair in range(GLA_HEADS // 2):
                pcols = slice(pair * MXU_N, (pair + 1) * MXU_N)
                kcols = slice(pair * 2 * GLA_DK, (pair + 1) * 2 * GLA_DK)
                st_out.append(st_in[pair] * decay[:, kcols]
                              + jnp.where(st_mask, kv_t[pcols, kcols], 0.0))
            vals["st"] = st_out
            vals[("blk", nb)] = (a_pairs, q_inter, [st_p.astype(BF16) for st_p in st_in])

        def output_stage(nb):
            lane_low = lax.broadcasted_iota(jnp.int32, (L, MXU_N), 1) < GLA_DV
            ggla = ggla_ref[...]
            a_pairs, q_inter, st_b = vals[("blk", nb)]
            rows = slice(nb * L, (nb + 1) * L)
            for pair in range(GLA_HEADS // 2):
                pcols = slice(pair * MXU_N, (pair + 1) * MXU_N)
                kcols = slice(pair * 2 * GLA_DK, (pair + 1) * 2 * GLA_DK)
                v_pair = pj[rows, COL_V + pair * MXU_N:COL_V + (pair + 1) * MXU_N]
                v_bd = jnp.concatenate([jnp.where(lane_low, v_pair, 0.0),
                                        jnp.where(lane_low, 0.0, v_pair)], axis=0).astype(BF16)
                o_inter = lax.dot_general(q_inter[:, kcols], st_b[pair], _NT_DIMS,
                                          preferred_element_type=F32)
                o_pair = jnp.dot(a_pairs[pair], v_bd, preferred_element_type=F32) + o_inter
                for hh in range(2):
                    hcols = slice(hh * GLA_DV, (hh + 1) * GLA_DV)
                    o_h = _rmsnorm(o_pair[:, hcols], ggla)
                    c0 = COL_R + pair * MXU_N + hh * GLA_DV
                    r_h = pj[rows, c0:c0 + GLA_DV]
                    o_h = o_h * (r_h * jax.nn.sigmoid(r_h))
                    y0 = D_POOL + pair * MXU_N + hh * GLA_DV
                    y_new[rows, y0:y0 + GLA_DV] = o_h.astype(BF16)

        if do_b:
            gate_stage()
        if do_cf:
            out_proj()
        if do_a:
            prenorm()
        if do_b:
            pool_sums()
        if do_a:
            proj_piece(0)
        if do_cf:
            out_finish()
        if do_b:
            prefix_stage()
        if do_a:
            proj_piece(1)
        if do_cf:
            ffn_chunk(0)
        if do_b:
            pool_maps()
            for nb in range(nblk):
                scores_stage(nb)
            st_ref[0] = vals["st"][0]
            st_ref[1] = vals["st"][1]
        if do_cf:
            ffn_chunk(1)
            ffn_chunk(2)
            ffn_chunk(3)
        if do_b:
            for nb in range(nblk):
                output_stage(nb)
        if do_cf:
            for c in range(4, D_FF // FFN_CHUNK):
                ffn_chunk(c)
        if do_a:
            proj_piece(2)
        if do_cf:
            ffn_down()
        if do_a:
            proj_r()
        if do_cf:
            ffn_finish()

    @pl.when(i == 0)
    def _():
        step(True, False, False)

    @pl.when(i == 1)
    def _():
        step(True, True, False)

    @pl.when(jnp.logical_and(i >= 2, i < n_tiles))
    def _():
        step(True, True, True)

    @pl.when(i == n_tiles)
    def _():
        step(False, True, True)

    @pl.when(i == n_tiles + 1)
    def _():
        step(False, False, True)


def _const_spec(shape):
    zeros = (0,) * len(shape)
    return pl.BlockSpec(shape, lambda i: zeros, pipeline_mode=pl.Buffered(1))


def _decoder_layer(x, g_pre, w_in_b, w_r_b, w_pool_p, pool_scale, w_gu_p, b_gate, g_gla, w_out, g_post,
                   g_pre2, w_ffn_in, conv_w, conv_b, w_ffn_out, g_post2):
    B, S, D = x.shape
    ts = TILE
    n_tiles = (B * S) // ts
    x2 = x.reshape(B * S, D)
    kern = functools.partial(_layer_kernel, tiles_per_seq=S // ts, n_tiles=n_tiles)
    cs = _const_spec
    out = pl.pallas_call(
        kern,
        grid=(n_tiles + 2,),
        in_specs=[
            pl.BlockSpec((ts, D), lambda i: (jnp.minimum(i, n_tiles - 1), 0)),
            pl.BlockSpec((ts, D), lambda i: (jnp.maximum(i - 2, 0), 0)),
            cs((1, D)),
            cs((D, D_IN)),
            cs((D, D_GLA_V)),
            cs((2, MXU_N, MXU_N)),
            cs((1, D_POOL)),
            cs((GATE_PAD, D_GLA_K)),
            cs((1, D_GLA_K)),
            cs((1, GLA_DV)),
            cs((D, D)),
            cs((1, D)),
            cs((1, D)),
            cs((D, 2 * D_FF)),
            cs((3, D_FF)),
            cs((1, D_FF)),
            cs((D_FF, D)),
            cs((1, D)),
        ],
        out_specs=pl.BlockSpec((ts, D), lambda i: (jnp.maximum(i - 2, 0), 0)),
        out_shape=jax.ShapeDtypeStruct((B * S, D), F32),
        scratch_shapes=[
            pltpu.VMEM((ts, D), BF16),
            pltpu.VMEM((2, ts, D_PROJ), F32),
            pltpu.VMEM((ts + MAX_WINDOW, D_POOL), F32),
            pltpu.VMEM((ts, D_POOL), BF16),
            pltpu.VMEM((GLA_HEADS // 2, 2 * GLA_DV, 2 * GLA_DK), F32),
            pltpu.VMEM((2, ts, D), BF16),
            pltpu.VMEM((ts, D), F32),
            pltpu.VMEM((ts, D), F32),
            pltpu.VMEM((ts, D), BF16),
            pltpu.VMEM((CONV_TAIL, D_FF), F32),
            pltpu.VMEM((ts, D_FF), BF16),
        ],
        compiler_params=pltpu.CompilerParams(
            dimension_semantics=("arbitrary",),
            vmem_limit_bytes=VMEM_LIMIT_BYTES),
        name="decoder_layer",
    )(x2, x2, g_pre, w_in_b, w_r_b, w_pool_p, pool_scale, w_gu_p, b_gate, g_gla, w_out, g_post,
      g_pre2, w_ffn_in, conv_w, conv_b, w_ffn_out, g_post2)
    return out.reshape(B, S, D)


def _pack_w_pool(w_pool):
    z = jnp.zeros((POOL_GROUP_DIM, POOL_GROUP_DIM), w_pool.dtype)
    pairs = [jnp.block([[w_pool[2 * p], z], [z, w_pool[2 * p + 1]]]) for p in range(2)]
    return jnp.stack(pairs).astype(BF16)


def kernel(x, g_pre_mix, w_in, w_pool, pool_scale, w_gate_up, b_gate, g_gla_norm, w_out, g_post_mix,
           g_pre_ffn, w_ffn_in, conv_w, conv_b, w_ffn_out, g_post_ffn):
    row = lambda a: a.reshape(1, -1).astype(F32)
    w_gu_p = jnp.pad(w_gate_up, ((0, GATE_PAD - GATE_RANK), (0, 0))).astype(BF16)
    return _decoder_layer(
        x, row(g_pre_mix), w_in.astype(BF16), w_in[:, W_IN_R:].astype(BF16), _pack_w_pool(w_pool),
        row(pool_scale), w_gu_p,
        row(b_gate), row(g_gla_norm), w_out.astype(BF16), row(g_post_mix), row(g_pre_ffn),
        w_ffn_in.astype(BF16), conv_w.astype(F32), row(conv_b), w_ffn_out.astype(BF16),
        row(g_post_ffn))
```

```python
import functools
import math

import jax
import jax.numpy as jnp
from jax import lax
from jax.experimental import pallas as pl
from jax.experimental.pallas import tpu as pltpu

D_MODEL = 1024
D_POOL = 512
POOL_WINDOWS = (2, 4, 8, 16)
POOL_GROUP_DIM = 128
MAX_WINDOW = 16
D_GLA_V = 512
GLA_HEADS = 4
GLA_DV = 128
GLA_DK = 64
D_GLA_K = 256
GATE_RANK = 16
GATE_TAU = 16.0
D_FF = 2816
EPS = 1e-6

LANES = 128
MXU_N = 256
GATE_PAD = LANES
D_IN = D_POOL + 2 * D_GLA_K + 2 * D_GLA_V + GATE_RANK
COL_U = 0
COL_Q = COL_U + D_POOL
COL_K = COL_Q + D_GLA_K
COL_V = COL_K + D_GLA_K
COL_G = COL_V + D_GLA_V
COL_R = COL_G + GATE_PAD
D_PROJ = COL_R + D_GLA_V
W_IN_R = COL_G + GATE_RANK
PROJ_PIECES = ((COL_U, D_POOL), (COL_Q, 2 * D_GLA_K), (COL_V, D_GLA_V + GATE_PAD))

TILE = 256
GLA_BLOCK = 128
FFN_CHUNK = 256
CONV_TAIL = 8

W_IN_ROWS = 128
STAGE_ROWS = 256
STAGE_COLS = 256

VMEM_LIMIT_BYTES = 56 * 1024 * 1024

F32 = jnp.float32
BF16 = jnp.bfloat16
_NT_DIMS = (((1,), (1,)), ((), ()))
_TN_DIMS = (((0,), (0,)), ((), ()))


def _rmsnorm(x, g):
    return x * lax.rsqrt(jnp.mean(x * x, axis=-1, keepdims=True) + EPS) * g


def _layer_kernel(xn_ref, xp_ref, gpre_ref, win_hbm, wr_hbm, wpool_ref, pscale_ref, wgu_ref, bg_ref,
                  ggla_ref, wout_hbm, gpost_ref, gpre2_ref, wfin_hbm, cw_ref, cb_ref, wfout_hbm,
                  gpost2_ref, o_ref, win_ref, wr_ref, wout_ref, wfin_ref, wfout_ref, stage_in, stage_rows,
                  stage_cols, stage_sem, h_ref, proj_ref, uext_ref, d_ref, st_ref, y_ref, mix_ref,
                  x1_ref, h2_ref, tail_ref, act_ref, *, tiles_per_seq, n_tiles):
    ts = TILE
    L = GLA_BLOCK
    nblk = ts // L
    half = 2 * L
    i = pl.program_id(0)
    slot_a = lax.rem(i, 2)
    slot_b = 1 - slot_a
    j = lax.rem(jnp.maximum(i - 1, 0), tiles_per_seq)
    jf = lax.rem(jnp.maximum(i - 2, 0), tiles_per_seq)

    @pl.when(j == 0)
    def _():
        uext_ref[0:MAX_WINDOW, :] = jnp.zeros((MAX_WINDOW, D_POOL), F32)
        st_ref[...] = jnp.zeros_like(st_ref)

    @pl.when(jf == 0)
    def _():
        tail_ref[...] = jnp.zeros_like(tail_ref)

    def load_weights():
        jobs = []
        for r in range(D_MODEL // W_IN_ROWS):
            rows = pl.ds(r * W_IN_ROWS, W_IN_ROWS)
            jobs.append((win_hbm.at[rows, :], stage_in, win_ref.at[rows, :]))
        for r in range(D_MODEL // STAGE_ROWS):
            rows = pl.ds(r * STAGE_ROWS, STAGE_ROWS)
            jobs.append((wr_hbm.at[rows, :], lambda slot: stage_rows.at[slot, :, 0:D_GLA_V],
                         wr_ref.at[rows, :]))
        for r in range(D_MODEL // STAGE_ROWS):
            rows = pl.ds(r * STAGE_ROWS, STAGE_ROWS)
            jobs.append((wout_hbm.at[rows, :], stage_rows, wout_ref.at[rows, :]))
        for c in range(2 * D_FF // STAGE_COLS):
            cols = pl.ds(c * STAGE_COLS, STAGE_COLS)
            jobs.append((wfin_hbm.at[:, cols], stage_cols, wfin_ref.at[:, cols]))
        for r in range(D_FF // STAGE_ROWS):
            rows = pl.ds(r * STAGE_ROWS, STAGE_ROWS)
            jobs.append((wfout_hbm.at[rows, :], stage_rows, wfout_ref.at[rows, :]))

        def staging(k):
            stage = jobs[k][1]
            slot = k % 2
            return stage(slot) if callable(stage) else stage.at[slot]

        def copy(k):
            return pltpu.make_async_copy(jobs[k][0], staging(k), stage_sem.at[k % 2])

        copy(0).start()
        copy(1).start()
        for k in range(len(jobs)):
            copy(k).wait()
            jobs[k][2][...] = staging(k)[...].astype(BF16)
            if k + 2 < len(jobs):
                copy(k + 2).start()

    def step(do_a, do_b, do_cf):
        pnew = proj_ref.at[slot_a]
        pj = proj_ref.at[slot_b]
        y_new = y_ref.at[slot_b]
        y_old = y_ref.at[slot_a]
        vals = {}

        def prenorm():
            h_ref[...] = _rmsnorm(xn_ref[...], gpre_ref[...]).astype(BF16)

        def proj_piece(p):
            n0, w = PROJ_PIECES[p]
            pnew[:, n0:n0 + w] = jnp.dot(h_ref[...], win_ref[:, n0:n0 + w],
                                         preferred_element_type=F32)

        def proj_r():
            pnew[:, COL_R:COL_R + D_GLA_V] = jnp.dot(h_ref[...], wr_ref[...],
                                                     preferred_element_type=F32)

        def out_proj():
            mix_ref[...] = jnp.dot(y_old[...], wout_ref[...], preferred_element_type=F32)

        def out_finish():
            x1 = xp_ref[...] + _rmsnorm(mix_ref[...], gpost_ref[...])
            x1_ref[...] = x1
            h2_ref[...] = _rmsnorm(x1, gpre2_ref[...]).astype(BF16)

        def ffn_chunk(c):
            row = lax.broadcasted_iota(jnp.int32, (CONV_TAIL, FFN_CHUNK), 0)
            cols = slice(c * FFN_CHUNK, (c + 1) * FFN_CHUNK)
            vcols = slice(D_FF + c * FFN_CHUNK, D_FF + (c + 1) * FFN_CHUNK)
            gate = jnp.dot(h2_ref[...], wfin_ref[:, cols], preferred_element_type=F32)
            val = jnp.dot(h2_ref[...], wfin_ref[:, vcols], preferred_element_type=F32)
            prev = tail_ref[:, cols]
            tail_ref[:, cols] = gate[ts - CONV_TAIL:ts, :]
            p1 = prev[CONV_TAIL - 1:CONV_TAIL, :]
            p2 = prev[CONV_TAIL - 2:CONV_TAIL - 1, :]
            r1 = pltpu.roll(gate, 1, 0)
            r2 = pltpu.roll(gate, 2, 0)
            h1 = jnp.where(row == 0, p1, r1[:CONV_TAIL])
            h2 = jnp.where(row == 0, p2, jnp.where(row == 1, p1, r2[:CONV_TAIL]))
            g1 = jnp.concatenate([h1, r1[CONV_TAIL:]], axis=0)
            g2 = jnp.concatenate([h2, r2[CONV_TAIL:]], axis=0)
            cw = cw_ref[:, cols]
            gc = g2 * cw[0:1, :] + g1 * cw[1:2, :] + gate * cw[2:3, :] + cb_ref[:, cols]
            act = 0.5 * gc * (1.0 + lax.erf(gc * (1.0 / math.sqrt(2.0)))) * val
            act_ref[:, cols] = act.astype(BF16)

        def ffn_down():
            vals["ff"] = jnp.dot(act_ref[...], wfout_ref[...], preferred_element_type=F32)

        def ffn_finish():
            o_ref[...] = x1_ref[...] + _rmsnorm(vals["ff"], gpost2_ref[...])

        def pool_sums():
            uext_ref[MAX_WINDOW:MAX_WINDOW + ts, :] = pj[:, COL_U:COL_U + D_POOL]
            row = lax.broadcasted_iota(jnp.int32, (MAX_WINDOW, 1), 0)
            pos = (j * ts + row + 1).astype(F32)
            for g, w in enumerate(POOL_WINDOWS):
                cols = slice(g * POOL_GROUP_DIM, (g + 1) * POOL_GROUP_DIM)
                ext = uext_ref[:, cols]
                acc = ext
                span = 1
                while span < w:
                    acc = acc + pltpu.roll(acc, span, 0)
                    span *= 2
                head = acc[MAX_WINDOW:2 * MAX_WINDOW, :] / jnp.minimum(pos, float(w))
                body = acc[2 * MAX_WINDOW:, :] * (1.0 / w)
                d = jnp.concatenate([head, body], axis=0) - ext[MAX_WINDOW:, :]
                d_ref[:, cols] = d.astype(BF16)
            uext_ref[0:MAX_WINDOW, :] = uext_ref[ts:ts + MAX_WINDOW, :]

        def pool_maps():
            for pair in range(2):
                cols = slice(pair * MXU_N, (pair + 1) * MXU_N)
                yp = jnp.dot(d_ref[:, cols], wpool_ref[pair], preferred_element_type=F32)
                y_new[:, cols] = (yp * pscale_ref[:, cols]).astype(BF16)

        def gate_stage():
            glow = pj[:, COL_G:COL_G + GATE_PAD].astype(BF16)
            logits = jnp.dot(glow, wgu_ref[...], preferred_element_type=F32) + bg_ref[...]
            log_a = ((jnp.minimum(logits, 0.0) - jnp.log(1.0 + jnp.exp(-jnp.abs(logits))))
                     * (1.0 / GATE_TAU))
            la_hi = log_a.astype(BF16)
            vals["la"] = (la_hi, (log_a - la_hi.astype(F32)).astype(BF16))

        def prefix_stage():
            la_hi, la_lo = vals["la"]
            ri = lax.broadcasted_iota(jnp.int32, (half, half), 0)
            ci = lax.broadcasted_iota(jnp.int32, (half, half), 1)
            blk_start = ri - lax.rem(ri, L)
            tri_bd = jnp.where(ci <= ri, jnp.where(ci >= blk_start, 1.0, 0.0), 0.0).astype(BF16)
            bcs = []
            for hb in range(ts // half):
                hrows = slice(hb * half, (hb + 1) * half)
                bc2 = (jnp.dot(tri_bd, la_hi[hrows, :], preferred_element_type=F32)
                       + jnp.dot(tri_bd, la_lo[hrows, :], preferred_element_type=F32))
                bcs += [bc2[0:L, :], bc2[L:half, :]]
            vals["bcs"] = bcs
            vals["st"] = [st_ref[0], st_ref[1]]

        def scores_stage(nb):
            lane_first = lax.broadcasted_iota(jnp.int32, (L, 2 * GLA_DK), 1) < GLA_DK
            ri2 = lax.broadcasted_iota(jnp.int32, (L, 2 * L), 0)
            ci2 = lax.broadcasted_iota(jnp.int32, (L, 2 * L), 1)
            causal2 = ri2 >= lax.rem(ci2, L)
            st_row_first = lax.broadcasted_iota(jnp.int32, (2 * GLA_DV, 2 * GLA_DK), 0) < GLA_DV
            st_col_first = lax.broadcasted_iota(jnp.int32, (2 * GLA_DV, 2 * GLA_DK), 1) < GLA_DK
            st_mask = st_row_first == st_col_first
            rows = slice(nb * L, (nb + 1) * L)
            bc = vals["bcs"][nb]
            b_last = bc[L - 1:L, :]
            b_mid = bc[L // 2 - 1:L // 2, :]
            q = pj[rows, COL_Q:COL_Q + D_GLA_K] * (GLA_DK ** -0.5)
            k = pj[rows, COL_K:COL_K + D_GLA_K]
            e_fwd = jnp.exp(bc - b_mid)
            q_mid = q * e_fwd
            k_intra = k * (1.0 / e_fwd)
            q_intra = q_mid.astype(BF16)
            q_inter = (q_mid * jnp.exp(b_mid)).astype(BF16)
            k_hat = (k_intra * jnp.exp(b_last - b_mid)).astype(BF16)
            a_pairs = []
            for pair in range(GLA_HEADS // 2):
                kcols = slice(pair * 2 * GLA_DK, (pair + 1) * 2 * GLA_DK)
                k_p = k_intra[:, kcols]
                k_pair = jnp.concatenate([jnp.where(lane_first, k_p, 0.0),
                                          jnp.where(lane_first, 0.0, k_p)], axis=0).astype(BF16)
                s_pair = lax.dot_general(q_intra[:, kcols], k_pair, _NT_DIMS,
                                         preferred_element_type=F32)
                a_pairs.append(jnp.where(causal2, s_pair, 0.0).astype(BF16))
            v_blk = pj[rows, COL_V:COL_V + D_GLA_V].astype(BF16)
            kv_t = lax.dot_general(v_blk, k_hat, _TN_DIMS, preferred_element_type=F32)
            decay = jnp.exp(b_last)
            st_in = vals["st"]
            st_out = []
            for pair in range(GLA_HEADS // 2):
                pcols = slice(pair * MXU_N, (pair + 1) * MXU_N)
                kcols = slice(pair * 2 * GLA_DK, (pair + 1) * 2 * GLA_DK)
                st_out.append(st_in[pair] * decay[:, kcols]
                              + jnp.where(st_mask, kv_t[pcols, kcols], 0.0))
            vals["st"] = st_out
            vals[("blk", nb)] = (a_pairs, q_inter, [st_p.astype(BF16) for st_p in st_in])

        def output_stage(nb):
            lane_low = lax.broadcasted_iota(jnp.int32, (L, MXU_N), 1) < GLA_DV
            ggla = ggla_ref[...]
            a_pairs, q_inter, st_b = vals[("blk", nb)]
            rows = slice(nb * L, (nb + 1) * L)
            for pair in range(GLA_HEADS // 2):
                pcols = slice(pair * MXU_N, (pair + 1) * MXU_N)
                kcols = slice(pair * 2 * GLA_DK, (pair + 1) * 2 * GLA_DK)
                v_pair = pj[rows, COL_V + pair * MXU_N:COL_V + (pair + 1) * MXU_N]
                v_bd = jnp.concatenate([jnp.where(lane_low, v_pair, 0.0),
                                        jnp.where(lane_low, 0.0, v_pair)], axis=0).astype(BF16)
                o_inter = lax.dot_general(q_inter[:, kcols], st_b[pair], _NT_DIMS,
                                          preferred_element_type=F32)
                o_pair = jnp.dot(a_pairs[pair], v_bd, preferred_element_type=F32) + o_inter
                for hh in range(2):
                    hcols = slice(hh * GLA_DV, (hh + 1) * GLA_DV)
                    o_h = _rmsnorm(o_pair[:, hcols], ggla)
                    c0 = COL_R + pair * MXU_N + hh * GLA_DV
                    r_h = pj[rows, c0:c0 + GLA_DV]
                    o_h = o_h * (r_h * jax.nn.sigmoid(r_h))
                    y0 = D_POOL + pair * MXU_N + hh * GLA_DV
                    y_new[rows, y0:y0 + GLA_DV] = o_h.astype(BF16)

        if do_b:
            gate_stage()
        if do_cf:
            out_proj()
        if do_a:
            prenorm()
        if do_b:
            pool_sums()
        if do_a:
            proj_piece(0)
        if do_cf:
            out_finish()
        if do_b:
            prefix_stage()
        if do_a:
            proj_piece(1)
        if do_cf:
            ffn_chunk(0)
        if do_b:
            pool_maps()
            for nb in range(nblk):
                scores_stage(nb)
            st_ref[0] = vals["st"][0]
            st_ref[1] = vals["st"][1]
        if do_cf:
            ffn_chunk(1)
            ffn_chunk(2)
            ffn_chunk(3)
        if do_b:
            for nb in range(nblk):
                output_stage(nb)
        if do_cf:
            for c in range(4, D_FF // FFN_CHUNK):
                ffn_chunk(c)
        if do_a:
            proj_piece(2)
        if do_cf:
            ffn_down()
        if do_a:
            proj_r()
        if do_cf:
            ffn_finish()

    @pl.when(i == 0)
    def _():
        load_weights()
        step(True, False, False)

    @pl.when(i == 1)
    def _():
        step(True, True, False)

    @pl.when(jnp.logical_and(i >= 2, i < n_tiles))
    def _():
        step(True, True, True)

    @pl.when(i == n_tiles)
    def _():
        step(False, True, True)

    @pl.when(i == n_tiles + 1)
    def _():
        step(False, False, True)


def _const_spec(shape):
    zeros = (0,) * len(shape)
    return pl.BlockSpec(shape, lambda i: zeros, pipeline_mode=pl.Buffered(1))


def _decoder_layer(x, g_pre, w_in_b, w_r_b, w_pool_p, pool_scale, w_gu_p, b_gate, g_gla, w_out, g_post,
                   g_pre2, w_ffn_in, conv_w, conv_b, w_ffn_out, g_post2):
    B, S, D = x.shape
    ts = TILE
    n_tiles = (B * S) // ts
    x2 = x.reshape(B * S, D)
    kern = functools.partial(_layer_kernel, tiles_per_seq=S // ts, n_tiles=n_tiles)
    cs = _const_spec
    hbm = pl.BlockSpec(memory_space=pl.ANY)
    out = pl.pallas_call(
        kern,
        grid=(n_tiles + 2,),
        in_specs=[
            pl.BlockSpec((ts, D), lambda i: (jnp.minimum(i, n_tiles - 1), 0)),
            pl.BlockSpec((ts, D), lambda i: (jnp.maximum(i - 2, 0), 0)),
            cs((1, D)),
            hbm,
            hbm,
            cs((2, MXU_N, MXU_N)),
            cs((1, D_POOL)),
            cs((GATE_PAD, D_GLA_K)),
            cs((1, D_GLA_K)),
            cs((1, GLA_DV)),
            hbm,
            cs((1, D)),
            cs((1, D)),
            hbm,
            cs((3, D_FF)),
            cs((1, D_FF)),
            hbm,
            cs((1, D)),
        ],
        out_specs=pl.BlockSpec((ts, D), lambda i: (jnp.maximum(i - 2, 0), 0)),
        out_shape=jax.ShapeDtypeStruct((B * S, D), F32),
        scratch_shapes=[
            pltpu.VMEM((D, D_IN), BF16),
            pltpu.VMEM((D, D_GLA_V), BF16),
            pltpu.VMEM((D, D), BF16),
            pltpu.VMEM((D, 2 * D_FF), BF16),
            pltpu.VMEM((D_FF, D), BF16),
            pltpu.VMEM((2, W_IN_ROWS, D_IN), F32),
            pltpu.VMEM((2, STAGE_ROWS, D), F32),
            pltpu.VMEM((2, D, STAGE_COLS), F32),
            pltpu.SemaphoreType.DMA((2,)),
            pltpu.VMEM((ts, D), BF16),
            pltpu.VMEM((2, ts, D_PROJ), F32),
            pltpu.VMEM((ts + MAX_WINDOW, D_POOL), F32),
            pltpu.VMEM((ts, D_POOL), BF16),
            pltpu.VMEM((GLA_HEADS // 2, 2 * GLA_DV, 2 * GLA_DK), F32),
            pltpu.VMEM((2, ts, D), BF16),
            pltpu.VMEM((ts, D), F32),
            pltpu.VMEM((ts, D), F32),
            pltpu.VMEM((ts, D), BF16),
            pltpu.VMEM((CONV_TAIL, D_FF), F32),
            pltpu.VMEM((ts, D_FF), BF16),
        ],
        compiler_params=pltpu.CompilerParams(
            dimension_semantics=("arbitrary",),
            vmem_limit_bytes=VMEM_LIMIT_BYTES),
        name="decoder_layer",
    )(x2, x2, g_pre, w_in_b, w_r_b, w_pool_p, pool_scale, w_gu_p, b_gate, g_gla, w_out, g_post,
      g_pre2, w_ffn_in, conv_w, conv_b, w_ffn_out, g_post2)
    return out.reshape(B, S, D)


def _pack_w_pool(w_pool):
    z = jnp.zeros((POOL_GROUP_DIM, POOL_GROUP_DIM), w_pool.dtype)
    pairs = [jnp.block([[w_pool[2 * p], z], [z, w_pool[2 * p + 1]]]) for p in range(2)]
    return jnp.stack(pairs).astype(BF16)


def kernel(x, g_pre_mix, w_in, w_pool, pool_scale, w_gate_up, b_gate, g_gla_norm, w_out, g_post_mix,
           g_pre_ffn, w_ffn_in, conv_w, conv_b, w_ffn_out, g_post_ffn):
    row = lambda a: a.reshape(1, -1).astype(F32)
    w_gu_p = jnp.pad(w_gate_up, ((0, GATE_PAD - GATE_RANK), (0, 0))).astype(BF16)
    return _decoder_layer(
        x, row(g_pre_mix), w_in.astype(F32), w_in[:, W_IN_R:].astype(F32), _pack_w_pool(w_pool),
        row(pool_scale), w_gu_p,
        row(b_gate), row(g_gla_norm), w_out.astype(F32), row(g_post_mix), row(g_pre_ffn),
        w_ffn_in.astype(F32), conv_w.astype(F32), row(conv_b), w_ffn_out.astype(F32),
        row(g_post_ffn))
```

```python
import functools
import math

import jax
import jax.numpy as jnp
from jax import lax
from jax.experimental import pallas as pl
from jax.experimental.pallas import tpu as pltpu

D_MODEL = 1024
D_POOL = 512
POOL_WINDOWS = (2, 4, 8, 16)
POOL_GROUP_DIM = 128
MAX_WINDOW = 16
D_GLA_V = 512
GLA_HEADS = 4
GLA_DV = 128
GLA_DK = 64
D_GLA_K = 256
GATE_RANK = 16
GATE_TAU = 16.0
D_FF = 2816
EPS = 1e-6

LANES = 128
MXU_N = 256
GATE_PAD = LANES
D_IN = D_POOL + 2 * D_GLA_K + 2 * D_GLA_V + GATE_RANK
COL_U = 0
COL_Q = COL_U + D_POOL
COL_K = COL_Q + D_GLA_K
COL_V = COL_K + D_GLA_K
COL_G = COL_V + D_GLA_V
COL_R = COL_G + GATE_PAD
D_PROJ = COL_R + D_GLA_V
W_IN_R = COL_G + GATE_RANK
PROJ_PIECES = ((COL_U, D_POOL), (COL_Q, 2 * D_GLA_K), (COL_V, D_GLA_V + GATE_PAD))

TILE = 256
GLA_BLOCK = 128
FFN_CHUNK = 256
CONV_TAIL = 8

STAGE_SLOTS = 3
W_IN_ROWS = 128
W_UP_ROWS = 64
STAGE_ROWS = 256

VMEM_LIMIT_BYTES = 56 * 1024 * 1024

F32 = jnp.float32
BF16 = jnp.bfloat16
_NT_DIMS = (((1,), (1,)), ((), ()))
_TN_DIMS = (((0,), (0,)), ((), ()))


def _rmsnorm(x, g):
    return x * lax.rsqrt(jnp.mean(x * x, axis=-1, keepdims=True) + EPS) * g


def _layer_kernel(xn_ref, xp_ref, gpre_ref, win_hbm, wpool_ref, pscale_ref, wgu_ref, bg_ref,
                  ggla_ref, wout_hbm, gpost_ref, gpre2_ref, wfin_hbm, cw_ref, cb_ref, wfout_hbm,
                  gpost2_ref, o_ref, win_ref, wr_ref, wout_ref, wfin_ref, wfout_ref, stage_in, stage_rows,
                  stage_up, stage_sem, h_ref, proj_ref, uext_ref, d_ref, st_ref, y_ref, mix_ref,
                  x1_ref, h2_ref, tail_ref, act_ref, *, tiles_per_seq, n_tiles):
    ts = TILE
    L = GLA_BLOCK
    nblk = ts // L
    half = 2 * L
    i = pl.program_id(0)
    slot_a = lax.rem(i, 2)
    slot_b = 1 - slot_a
    j = lax.rem(jnp.maximum(i - 1, 0), tiles_per_seq)
    jf = lax.rem(jnp.maximum(i - 2, 0), tiles_per_seq)

    @pl.when(j == 0)
    def _():
        uext_ref[0:MAX_WINDOW, :] = jnp.zeros((MAX_WINDOW, D_POOL), F32)
        st_ref[...] = jnp.zeros_like(st_ref)

    @pl.when(jf == 0)
    def _():
        tail_ref[...] = jnp.zeros_like(tail_ref)

    def load_weights():
        def cast_in(dst_rows, chunk):
            win_ref[dst_rows, :] = chunk.astype(BF16)
            wr_ref[dst_rows, :] = chunk[:, W_IN_R:].astype(BF16)

        def cast_to(dst_ref):
            def cast(dst_rows, chunk):
                dst_ref[dst_rows, :] = chunk.astype(BF16)
            return cast

        jobs = []
        for hbm_ref, stage, cast in ((win_hbm, stage_in, cast_in), (wout_hbm, stage_rows, cast_to(wout_ref)),
                                     (wfin_hbm, stage_up, cast_to(wfin_ref)),
                                     (wfout_hbm, stage_rows, cast_to(wfout_ref))):
            chunk_rows = stage.shape[1]
            for r in range(hbm_ref.shape[0] // chunk_rows):
                jobs.append((hbm_ref, pl.ds(r * chunk_rows, chunk_rows), stage, cast))

        def copy(k):
            hbm_ref, rows, stage, _ = jobs[k]
            slot = k % STAGE_SLOTS
            return pltpu.make_async_copy(hbm_ref.at[rows, :], stage.at[slot], stage_sem.at[slot])

        for k in range(STAGE_SLOTS):
            copy(k).start()
        for k in range(len(jobs)):
            _, rows, stage, cast = jobs[k]
            copy(k).wait()
            cast(rows, stage[k % STAGE_SLOTS])
            if k + STAGE_SLOTS < len(jobs):
                copy(k + STAGE_SLOTS).start()

    def step(do_a, do_b, do_cf):
        pnew = proj_ref.at[slot_a]
        pj = proj_ref.at[slot_b]
        y_new = y_ref.at[slot_b]
        y_old = y_ref.at[slot_a]
        vals = {}

        def prenorm():
            h_ref[...] = _rmsnorm(xn_ref[...], gpre_ref[...]).astype(BF16)

        def proj_piece(p):
            n0, w = PROJ_PIECES[p]
            pnew[:, n0:n0 + w] = jnp.dot(h_ref[...], win_ref[:, n0:n0 + w],
                                         preferred_element_type=F32)

        def proj_r():
            pnew[:, COL_R:COL_R + D_GLA_V] = jnp.dot(h_ref[...], wr_ref[...],
                                                     preferred_element_type=F32)

        def out_proj():
            mix_ref[...] = jnp.dot(y_old[...], wout_ref[...], preferred_element_type=F32)

        def out_finish():
            x1 = xp_ref[...] + _rmsnorm(mix_ref[...], gpost_ref[...])
            x1_ref[...] = x1
            h2_ref[...] = _rmsnorm(x1, gpre2_ref[...]).astype(BF16)

        def ffn_chunk(c):
            row = lax.broadcasted_iota(jnp.int32, (CONV_TAIL, FFN_CHUNK), 0)
            cols = slice(c * FFN_CHUNK, (c + 1) * FFN_CHUNK)
            vcols = slice(D_FF + c * FFN_CHUNK, D_FF + (c + 1) * FFN_CHUNK)
            gate = jnp.dot(h2_ref[...], wfin_ref[:, cols], preferred_element_type=F32)
            val = jnp.dot(h2_ref[...], wfin_ref[:, vcols], preferred_element_type=F32)
            prev = tail_ref[:, cols]
            tail_ref[:, cols] = gate[ts - CONV_TAIL:ts, :]
            p1 = prev[CONV_TAIL - 1:CONV_TAIL, :]
            p2 = prev[CONV_TAIL - 2:CONV_TAIL - 1, :]
            r1 = pltpu.roll(gate, 1, 0)
            r2 = pltpu.roll(gate, 2, 0)
            h1 = jnp.where(row == 0, p1, r1[:CONV_TAIL])
            h2 = jnp.where(row == 0, p2, jnp.where(row == 1, p1, r2[:CONV_TAIL]))
            g1 = jnp.concatenate([h1, r1[CONV_TAIL:]], axis=0)
            g2 = jnp.concatenate([h2, r2[CONV_TAIL:]], axis=0)
            cw = cw_ref[:, cols]
            gc = g2 * cw[0:1, :] + g1 * cw[1:2, :] + gate * cw[2:3, :] + cb_ref[:, cols]
            act = 0.5 * gc * (1.0 + lax.erf(gc * (1.0 / math.sqrt(2.0)))) * val
            act_ref[:, cols] = act.astype(BF16)

        def ffn_down():
            vals["ff"] = jnp.dot(act_ref[...], wfout_ref[...], preferred_element_type=F32)

        def ffn_finish():
            o_ref[...] = x1_ref[...] + _rmsnorm(vals["ff"], gpost2_ref[...])

        def pool_sums():
            uext_ref[MAX_WINDOW:MAX_WINDOW + ts, :] = pj[:, COL_U:COL_U + D_POOL]
            row = lax.broadcasted_iota(jnp.int32, (MAX_WINDOW, 1), 0)
            pos = (j * ts + row + 1).astype(F32)
            for g, w in enumerate(POOL_WINDOWS):
                cols = slice(g * POOL_GROUP_DIM, (g + 1) * POOL_GROUP_DIM)
                ext = uext_ref[:, cols]
                acc = ext
                span = 1
                while span < w:
                    acc = acc + pltpu.roll(acc, span, 0)
                    span *= 2
                head = acc[MAX_WINDOW:2 * MAX_WINDOW, :] / jnp.minimum(pos, float(w))
                body = acc[2 * MAX_WINDOW:, :] * (1.0 / w)
                d = jnp.concatenate([head, body], axis=0) - ext[MAX_WINDOW:, :]
                d_ref[:, cols] = d.astype(BF16)
            uext_ref[0:MAX_WINDOW, :] = uext_ref[ts:ts + MAX_WINDOW, :]

        def pool_maps():
            for pair in range(2):
                cols = slice(pair * MXU_N, (pair + 1) * MXU_N)
                yp = jnp.dot(d_ref[:, cols], wpool_ref[pair], preferred_element_type=F32)
                y_new[:, cols] = (yp * pscale_ref[:, cols]).astype(BF16)

        def gate_stage():
            glow = pj[:, COL_G:COL_G + GATE_PAD].astype(BF16)
            logits = jnp.dot(glow, wgu_ref[...], preferred_element_type=F32) + bg_ref[...]
            log_a = ((jnp.minimum(logits, 0.0) - jnp.log(1.0 + jnp.exp(-jnp.abs(logits))))
                     * (1.0 / GATE_TAU))
            la_hi = log_a.astype(BF16)
            vals["la"] = (la_hi, (log_a - la_hi.astype(F32)).astype(BF16))

        def prefix_stage():
            la_hi, la_lo = vals["la"]
            ri = lax.broadcasted_iota(jnp.int32, (half, half), 0)
            ci = lax.broadcasted_iota(jnp.int32, (half, half), 1)
            blk_start = ri - lax.rem(ri, L)
            tri_bd = jnp.where(ci <= ri, jnp.where(ci >= blk_start, 1.0, 0.0), 0.0).astype(BF16)
            bcs = []
            for hb in range(ts // half):
                hrows = slice(hb * half, (hb + 1) * half)
                bc2 = (jnp.dot(tri_bd, la_hi[hrows, :], preferred_element_type=F32)
                       + jnp.dot(tri_bd, la_lo[hrows, :], preferred_element_type=F32))
                bcs += [bc2[0:L, :], bc2[L:half, :]]
            vals["bcs"] = bcs
            vals["st"] = [st_ref[0], st_ref[1]]

        def scores_stage(nb):
            lane_first = lax.broadcasted_iota(jnp.int32, (L, 2 * GLA_DK), 1) < GLA_DK
            ri2 = lax.broadcasted_iota(jnp.int32, (L, 2 * L), 0)
            ci2 = lax.broadcasted_iota(jnp.int32, (L, 2 * L), 1)
            causal2 = ri2 >= lax.rem(ci2, L)
            st_row_first = lax.broadcasted_iota(jnp.int32, (2 * GLA_DV, 2 * GLA_DK), 0) < GLA_DV
            st_col_first = lax.broadcasted_iota(jnp.int32, (2 * GLA_DV, 2 * GLA_DK), 1) < GLA_DK
            st_mask = st_row_first == st_col_first
            rows = slice(nb * L, (nb + 1) * L)
            bc = vals["bcs"][nb]
            b_last = bc[L - 1:L, :]
            b_mid = bc[L // 2 - 1:L // 2, :]
            q = pj[rows, COL_Q:COL_Q + D_GLA_K] * (GLA_DK ** -0.5)
            k = pj[rows, COL_K:COL_K + D_GLA_K]
            e_fwd = jnp.exp(bc - b_mid)
            q_mid = q * e_fwd
            k_intra = k * (1.0 / e_fwd)
            q_intra = q_mid.astype(BF16)
            q_inter = (q_mid * jnp.exp(b_mid)).astype(BF16)
            k_hat = (k_intra * jnp.exp(b_last - b_mid)).astype(BF16)
            a_pairs = []
            for pair in range(GLA_HEADS // 2):
                kcols = slice(pair * 2 * GLA_DK, (pair + 1) * 2 * GLA_DK)
                k_p = k_intra[:, kcols]
                k_pair = jnp.concatenate([jnp.where(lane_first, k_p, 0.0),
                                          jnp.where(lane_first, 0.0, k_p)], axis=0).astype(BF16)
                s_pair = lax.dot_general(q_intra[:, kcols], k_pair, _NT_DIMS,
                                         preferred_element_type=F32)
                a_pairs.append(jnp.where(causal2, s_pair, 0.0).astype(BF16))
            v_blk = pj[rows, COL_V:COL_V + D_GLA_V].astype(BF16)
            kv_t = lax.dot_general(v_blk, k_hat, _TN_DIMS, preferred_element_type=F32)
            decay = jnp.exp(b_last)
            st_in = vals["st"]
            st_out = []
            for pair in range(GLA_HEADS // 2):
                pcols = slice(pair * MXU_N, (pair + 1) * MXU_N)
                kcols = slice(pair * 2 * GLA_DK, (pair + 1) * 2 * GLA_DK)
                st_out.append(st_in[pair] * decay[:, kcols]
                              + jnp.where(st_mask, kv_t[pcols, kcols], 0.0))
            vals["st"] = st_out
            vals[("blk", nb)] = (a_pairs, q_inter, [st_p.astype(BF16) for st_p in st_in])

        def output_stage(nb):
            lane_low = lax.broadcasted_iota(jnp.int32, (L, MXU_N), 1) < GLA_DV
            ggla = ggla_ref[...]
            a_pairs, q_inter, st_b = vals[("blk", nb)]
            rows = slice(nb * L, (nb + 1) * L)
            for pair in range(GLA_HEADS // 2):
                pcols = slice(pair * MXU_N, (pair + 1) * MXU_N)
                kcols = slice(pair * 2 * GLA_DK, (pair + 1) * 2 * GLA_DK)
                v_pair = pj[rows, COL_V + pair * MXU_N:COL_V + (pair + 1) * MXU_N]
                v_bd = jnp.concatenate([jnp.where(lane_low, v_pair, 0.0),
                                        jnp.where(lane_low, 0.0, v_pair)], axis=0).astype(BF16)
                o_inter = lax.dot_general(q_inter[:, kcols], st_b[pair], _NT_DIMS,
                                          preferred_element_type=F32)
                o_pair = jnp.dot(a_pairs[pair], v_bd, preferred_element_type=F32) + o_inter
                for hh in range(2):
                    hcols = slice(hh * GLA_DV, (hh + 1) * GLA_DV)
                    o_h = _rmsnorm(o_pair[:, hcols], ggla)
                    c0 = COL_R + pair * MXU_N + hh * GLA_DV
                    r_h = pj[rows, c0:c0 + GLA_DV]
                    o_h = o_h * (r_h * jax.nn.sigmoid(r_h))
                    y0 = D_POOL + pair * MXU_N + hh * GLA_DV
                    y_new[rows, y0:y0 + GLA_DV] = o_h.astype(BF16)

        if do_b:
            gate_stage()
        if do_cf:
            out_proj()
        if do_a:
            prenorm()
        if do_b:
            pool_sums()
        if do_a:
            proj_piece(0)
        if do_cf:
            out_finish()
        if do_b:
            prefix_stage()
        if do_a:
            proj_piece(1)
        if do_cf:
            ffn_chunk(0)
        if do_b:
            pool_maps()
            for nb in range(nblk):
                scores_stage(nb)
            st_ref[0] = vals["st"][0]
            st_ref[1] = vals["st"][1]
        if do_cf:
            ffn_chunk(1)
            ffn_chunk(2)
            ffn_chunk(3)
        if do_b:
            for nb in range(nblk):
                output_stage(nb)
        if do_cf:
            for c in range(4, D_FF // FFN_CHUNK):
                ffn_chunk(c)
        if do_a:
            proj_piece(2)
        if do_cf:
            ffn_down()
        if do_a:
            proj_r()
        if do_cf:
            ffn_finish()

    @pl.when(i == 0)
    def _():
        load_weights()
        step(True, False, False)

    @pl.when(i == 1)
    def _():
        step(True, True, False)

    @pl.when(jnp.logical_and(i >= 2, i < n_tiles))
    def _():
        step(True, True, True)

    @pl.when(i == n_tiles)
    def _():
        step(False, True, True)

    @pl.when(i == n_tiles + 1)
    def _():
        step(False, False, True)


def _const_spec(shape):
    zeros = (0,) * len(shape)
    return pl.BlockSpec(shape, lambda i: zeros, pipeline_mode=pl.Buffered(1))


def _decoder_layer(x, g_pre, w_in, w_pool_p, pool_scale, w_gu_p, b_gate, g_gla, w_out, g_post,
                   g_pre2, w_ffn_in, conv_w, conv_b, w_ffn_out, g_post2):
    B, S, D = x.shape
    ts = TILE
    n_tiles = (B * S) // ts
    x2 = x.reshape(B * S, D)
    kern = functools.partial(_layer_kernel, tiles_per_seq=S // ts, n_tiles=n_tiles)
    cs = _const_spec
    hbm = pl.BlockSpec(memory_space=pl.ANY)
    out = pl.pallas_call(
        kern,
        grid=(n_tiles + 2,),
        in_specs=[
            pl.BlockSpec((ts, D), lambda i: (jnp.minimum(i, n_tiles - 1), 0)),
            pl.BlockSpec((ts, D), lambda i: (jnp.maximum(i - 2, 0), 0)),
            cs((1, D)),
            hbm,
            cs((2, MXU_N, MXU_N)),
            cs((1, D_POOL)),
            cs((GATE_PAD, D_GLA_K)),
            cs((1, D_GLA_K)),
            cs((1, GLA_DV)),
            hbm,
            cs((1, D)),
            cs((1, D)),
            hbm,
            cs((3, D_FF)),
            cs((1, D_FF)),
            hbm,
            cs((1, D)),
        ],
        out_specs=pl.BlockSpec((ts, D), lambda i: (jnp.maximum(i - 2, 0), 0)),
        out_shape=jax.ShapeDtypeStruct((B * S, D), F32),
        scratch_shapes=[
            pltpu.VMEM((D, D_IN), BF16),
            pltpu.VMEM((D, D_GLA_V), BF16),
            pltpu.VMEM((D, D), BF16),
            pltpu.VMEM((D, 2 * D_FF), BF16),
            pltpu.VMEM((D_FF, D), BF16),
            pltpu.VMEM((STAGE_SLOTS, W_IN_ROWS, D_IN), F32),
            pltpu.VMEM((STAGE_SLOTS, STAGE_ROWS, D), F32),
            pltpu.VMEM((STAGE_SLOTS, W_UP_ROWS, 2 * D_FF), F32),
            pltpu.SemaphoreType.DMA((STAGE_SLOTS,)),
            pltpu.VMEM((ts, D), BF16),
            pltpu.VMEM((2, ts, D_PROJ), F32),
            pltpu.VMEM((ts + MAX_WINDOW, D_POOL), F32),
            pltpu.VMEM((ts, D_POOL), BF16),
            pltpu.VMEM((GLA_HEADS // 2, 2 * GLA_DV, 2 * GLA_DK), F32),
            pltpu.VMEM((2, ts, D), BF16),
            pltpu.VMEM((ts, D), F32),
            pltpu.VMEM((ts, D), F32),
            pltpu.VMEM((ts, D), BF16),
            pltpu.VMEM((CONV_TAIL, D_FF), F32),
            pltpu.VMEM((ts, D_FF), BF16),
        ],
        compiler_params=pltpu.CompilerParams(
            dimension_semantics=("arbitrary",),
            vmem_limit_bytes=VMEM_LIMIT_BYTES),
        name="decoder_layer",
    )(x2, x2, g_pre, w_in, w_pool_p, pool_scale, w_gu_p, b_gate, g_gla, w_out, g_post,
      g_pre2, w_ffn_in, conv_w, conv_b, w_ffn_out, g_post2)
    return out.reshape(B, S, D)


def _pack_w_pool(w_pool):
    z = jnp.zeros((POOL_GROUP_DIM, POOL_GROUP_DIM), w_pool.dtype)
    pairs = [jnp.block([[w_pool[2 * p], z], [z, w_pool[2 * p + 1]]]) for p in range(2)]
    return jnp.stack(pairs).astype(BF16)


def kernel(x, g_pre_mix, w_in, w_pool, pool_scale, w_gate_up, b_gate, g_gla_norm, w_out, g_post_mix,
           g_pre_ffn, w_ffn_in, conv_w, conv_b, w_ffn_out, g_post_ffn):
    row = lambda a: a.reshape(1, -1).astype(F32)
    w_gu_p = jnp.pad(w_gate_up, ((0, GATE_PAD - GATE_RANK), (0, 0))).astype(BF16)
    return _decoder_layer(
        x, row(g_pre_mix), w_in.astype(F32), _pack_w_pool(w_pool),
        row(pool_scale), w_gu_p,
        row(b_gate), row(g_gla_norm), w_out.astype(F32), row(g_post_mix), row(g_pre_ffn),
        w_ffn_in.astype(F32), conv_w.astype(F32), row(conv_b), w_ffn_out.astype(F32),
        row(g_post_ffn))
```

```python
import functools
import math

import jax
import jax.numpy as jnp
from jax import lax
from jax.experimental import pallas as pl
from jax.experimental.pallas import tpu as pltpu

D_MODEL = 1024
D_POOL = 512
POOL_WINDOWS = (2, 4, 8, 16)
POOL_GROUP_DIM = 128
MAX_WINDOW = 16
D_GLA_V = 512
GLA_HEADS = 4
GLA_DV = 128
GLA_DK = 64
D_GLA_K = 256
GATE_RANK = 16
GATE_TAU = 16.0
D_FF = 2816
EPS = 1e-6

LANES = 128
MXU_N = 256
GATE_PAD = LANES
D_IN = D_POOL + 2 * D_GLA_K + 2 * D_GLA_V + GATE_RANK
COL_U = 0
COL_Q = COL_U + D_POOL
COL_K = COL_Q + D_GLA_K
COL_V = COL_K + D_GLA_K
COL_G = COL_V + D_GLA_V
COL_R = COL_G + GATE_PAD
D_PROJ = COL_R + D_GLA_V
W_IN_R = COL_G + GATE_RANK
PROJ_PIECES = ((COL_U, D_POOL), (COL_Q, 2 * D_GLA_K), (COL_V, D_GLA_V + GATE_PAD))

TILE = 256
GLA_BLOCK = 128
FFN_CHUNK = 256
CONV_TAIL = 8

STAGE_SLOTS = 3
W_IN_ROWS = 128
W_UP_ROWS = 64
STAGE_ROWS = 256

VMEM_LIMIT_BYTES = 56 * 1024 * 1024

F32 = jnp.float32
BF16 = jnp.bfloat16
_NT_DIMS = (((1,), (1,)), ((), ()))
_TN_DIMS = (((0,), (0,)), ((), ()))


def _rmsnorm(x, g):
    return x * lax.rsqrt(jnp.mean(x * x, axis=-1, keepdims=True) + EPS) * g


def _layer_kernel(xn_ref, xp_ref, gpre_ref, win_hbm, wpool_ref, pscale_ref, wgu_ref, bg_ref,
                  ggla_ref, wout_hbm, gpost_ref, gpre2_ref, wfin_hbm, cw_ref, cb_ref, wfout_hbm,
                  gpost2_ref, o_ref, win_ref, wr_ref, wout_ref, wfin_ref, wfout_ref, stage_in, stage_rows,
                  stage_up, stage_sem, h_ref, proj_ref, utail_ref, d_ref, st_ref, y_ref,
                  x1_ref, h2_ref, tail_ref, act_ref, *, tiles_per_seq, n_tiles):
    ts = TILE
    L = GLA_BLOCK
    nblk = ts // L
    half = 2 * L
    i = pl.program_id(0)
    slot_a = lax.rem(i, 2)
    slot_b = 1 - slot_a
    j = lax.rem(jnp.maximum(i - 1, 0), tiles_per_seq)
    jf = lax.rem(jnp.maximum(i - 2, 0), tiles_per_seq)

    @pl.when(j == 0)
    def _():
        utail_ref[...] = jnp.zeros_like(utail_ref)
        st_ref[...] = jnp.zeros_like(st_ref)

    @pl.when(jf == 0)
    def _():
        tail_ref[...] = jnp.zeros_like(tail_ref)

    def load_weights():
        def cast_in(dst_rows, chunk):
            win_ref[dst_rows, :] = chunk.astype(BF16)
            wr_ref[dst_rows, :] = chunk[:, W_IN_R:].astype(BF16)

        def cast_to(dst_ref):
            def cast(dst_rows, chunk):
                dst_ref[dst_rows, :] = chunk.astype(BF16)
            return cast

        jobs = []
        for hbm_ref, stage, cast in ((win_hbm, stage_in, cast_in), (wout_hbm, stage_rows, cast_to(wout_ref)),
                                     (wfin_hbm, stage_up, cast_to(wfin_ref)),
                                     (wfout_hbm, stage_rows, cast_to(wfout_ref))):
            chunk_rows = stage.shape[1]
            for r in range(hbm_ref.shape[0] // chunk_rows):
                jobs.append((hbm_ref, pl.ds(r * chunk_rows, chunk_rows), stage, cast))

        def copy(k):
            hbm_ref, rows, stage, _ = jobs[k]
            slot = k % STAGE_SLOTS
            return pltpu.make_async_copy(hbm_ref.at[rows, :], stage.at[slot], stage_sem.at[slot])

        for k in range(STAGE_SLOTS):
            copy(k).start()
        for k in range(len(jobs)):
            _, rows, stage, cast = jobs[k]
            copy(k).wait()
            cast(rows, stage[k % STAGE_SLOTS])
            if k + STAGE_SLOTS < len(jobs):
                copy(k + STAGE_SLOTS).start()

    def step(do_a, do_b, do_cf):
        pnew = proj_ref.at[slot_a]
        pj = proj_ref.at[slot_b]
        y_new = y_ref.at[slot_b]
        y_old = y_ref.at[slot_a]
        vals = {}

        def prenorm():
            h_ref[...] = _rmsnorm(xn_ref[...], gpre_ref[...]).astype(BF16)

        def proj_piece(p):
            n0, w = PROJ_PIECES[p]
            pnew[:, n0:n0 + w] = jnp.dot(h_ref[...], win_ref[:, n0:n0 + w],
                                         preferred_element_type=F32)

        def proj_r():
            pnew[:, COL_R:COL_R + D_GLA_V] = jnp.dot(h_ref[...], wr_ref[...],
                                                     preferred_element_type=F32)

        def out_proj():
            vals["mix"] = jnp.dot(y_old[...], wout_ref[...], preferred_element_type=F32)

        def out_finish():
            x1 = xp_ref[...] + _rmsnorm(vals["mix"], gpost_ref[...])
            x1_ref[...] = x1
            h2_ref[...] = _rmsnorm(x1, gpre2_ref[...]).astype(BF16)

        def ffn_chunk(c):
            row = lax.broadcasted_iota(jnp.int32, (CONV_TAIL, FFN_CHUNK), 0)
            cols = slice(c * FFN_CHUNK, (c + 1) * FFN_CHUNK)
            vcols = slice(D_FF + c * FFN_CHUNK, D_FF + (c + 1) * FFN_CHUNK)
            gate = jnp.dot(h2_ref[...], wfin_ref[:, cols], preferred_element_type=F32)
            val = jnp.dot(h2_ref[...], wfin_ref[:, vcols], preferred_element_type=F32)
            prev = tail_ref[:, cols]
            tail_ref[:, cols] = gate[ts - CONV_TAIL:ts, :]
            p1 = prev[CONV_TAIL - 1:CONV_TAIL, :]
            p2 = prev[CONV_TAIL - 2:CONV_TAIL - 1, :]
            r1 = pltpu.roll(gate, 1, 0)
            r2 = pltpu.roll(gate, 2, 0)
            h1 = jnp.where(row == 0, p1, r1[:CONV_TAIL])
            h2 = jnp.where(row == 0, p2, jnp.where(row == 1, p1, r2[:CONV_TAIL]))
            g1 = jnp.concatenate([h1, r1[CONV_TAIL:]], axis=0)
            g2 = jnp.concatenate([h2, r2[CONV_TAIL:]], axis=0)
            cw = cw_ref[:, cols]
            gc = g2 * cw[0:1, :] + g1 * cw[1:2, :] + gate * cw[2:3, :] + cb_ref[:, cols]
            act = 0.5 * gc * (1.0 + lax.erf(gc * (1.0 / math.sqrt(2.0)))) * val
            act_ref[:, cols] = act.astype(BF16)

        def ffn_down():
            vals["ff"] = jnp.dot(act_ref[...], wfout_ref[...], preferred_element_type=F32)

        def ffn_finish():
            o_ref[...] = x1_ref[...] + _rmsnorm(vals["ff"], gpost2_ref[...])

        def pool_sums():
            row = lax.broadcasted_iota(jnp.int32, (MAX_WINDOW, 1), 0)
            pos = (j * ts + row + 1).astype(F32)
            for g, w in enumerate(POOL_WINDOWS):
                cols = slice(g * POOL_GROUP_DIM, (g + 1) * POOL_GROUP_DIM)
                u_g = pj[:, COL_U + g * POOL_GROUP_DIM:COL_U + (g + 1) * POOL_GROUP_DIM]
                ext = jnp.concatenate([utail_ref[:, cols], u_g], axis=0)
                utail_ref[:, cols] = u_g[ts - MAX_WINDOW:, :]
                acc = ext
                span = 1
                while span < w:
                    acc = acc + pltpu.roll(acc, span, 0)
                    span *= 2
                head = acc[MAX_WINDOW:2 * MAX_WINDOW, :] / jnp.minimum(pos, float(w))
                body = acc[2 * MAX_WINDOW:, :] * (1.0 / w)
                d = jnp.concatenate([head, body], axis=0) - u_g
                d_ref[:, cols] = d.astype(BF16)

        def pool_maps():
            for pair in range(2):
                cols = slice(pair * MXU_N, (pair + 1) * MXU_N)
                yp = jnp.dot(d_ref[:, cols], wpool_ref[pair], preferred_element_type=F32)
                y_new[:, cols] = (yp * pscale_ref[:, cols]).astype(BF16)

        def gate_stage():
            glow = pj[:, COL_G:COL_G + GATE_PAD].astype(BF16)
            logits = jnp.dot(glow, wgu_ref[...], preferred_element_type=F32) + bg_ref[...]
            log_a = ((jnp.minimum(logits, 0.0) - jnp.log(1.0 + jnp.exp(-jnp.abs(logits))))
                     * (1.0 / GATE_TAU))
            la_hi = log_a.astype(BF16)
            vals["la"] = (la_hi, (log_a - la_hi.astype(F32)).astype(BF16))

        def prefix_stage():
            la_hi, la_lo = vals["la"]
            ri = lax.broadcasted_iota(jnp.int32, (half, half), 0)
            ci = lax.broadcasted_iota(jnp.int32, (half, half), 1)
            blk_start = ri - lax.rem(ri, L)
            tri_bd = jnp.where(ci <= ri, jnp.where(ci >= blk_start, 1.0, 0.0), 0.0).astype(BF16)
            bcs = []
            for hb in range(ts // half):
                hrows = slice(hb * half, (hb + 1) * half)
                bc2 = (jnp.dot(tri_bd, la_hi[hrows, :], preferred_element_type=F32)
                       + jnp.dot(tri_bd, la_lo[hrows, :], preferred_element_type=F32))
                bcs += [bc2[0:L, :], bc2[L:half, :]]
            vals["bcs"] = bcs
            vals["st"] = [st_ref[0], st_ref[1]]

        def scores_stage(nb):
            lane_first = lax.broadcasted_iota(jnp.int32, (L, 2 * GLA_DK), 1) < GLA_DK
            ri2 = lax.broadcasted_iota(jnp.int32, (L, 2 * L), 0)
            ci2 = lax.broadcasted_iota(jnp.int32, (L, 2 * L), 1)
            causal2 = ri2 >= lax.rem(ci2, L)
            st_row_first = lax.broadcasted_iota(jnp.int32, (2 * GLA_DV, 2 * GLA_DK), 0) < GLA_DV
            st_col_first = lax.broadcasted_iota(jnp.int32, (2 * GLA_DV, 2 * GLA_DK), 1) < GLA_DK
            st_mask = st_row_first == st_col_first
            rows = slice(nb * L, (nb + 1) * L)
            bc = vals["bcs"][nb]
            b_last = bc[L - 1:L, :]
            b_mid = bc[L // 2 - 1:L // 2, :]
            q = pj[rows, COL_Q:COL_Q + D_GLA_K] * (GLA_DK ** -0.5)
            k = pj[rows, COL_K:COL_K + D_GLA_K]
            e_fwd = jnp.exp(bc - b_mid)
            q_mid = q * e_fwd
            k_intra = k * (1.0 / e_fwd)
            q_intra = q_mid.astype(BF16)
            q_inter = (q_mid * jnp.exp(b_mid)).astype(BF16)
            k_hat = (k_intra * jnp.exp(b_last - b_mid)).astype(BF16)
            a_pairs = []
            for pair in range(GLA_HEADS // 2):
                kcols = slice(pair * 2 * GLA_DK, (pair + 1) * 2 * GLA_DK)
                k_p = k_intra[:, kcols]
                k_pair = jnp.concatenate([jnp.where(lane_first, k_p, 0.0),
                                          jnp.where(lane_first, 0.0, k_p)], axis=0).astype(BF16)
                s_pair = lax.dot_general(q_intra[:, kcols], k_pair, _NT_DIMS,
                                         preferred_element_type=F32)
                a_pairs.append(jnp.where(causal2, s_pair, 0.0).astype(BF16))
            v_blk = pj[rows, COL_V:COL_V + D_GLA_V].astype(BF16)
            kv_t = lax.dot_general(v_blk, k_hat, _TN_DIMS, preferred_element_type=F32)
            decay = jnp.exp(b_last)
            st_in = vals["st"]
            st_out = []
            for pair in range(GLA_HEADS // 2):
                pcols = slice(pair * MXU_N, (pair + 1) * MXU_N)
                kcols = slice(pair * 2 * GLA_DK, (pair + 1) * 2 * GLA_DK)
                st_out.append(st_in[pair] * decay[:, kcols]
                              + jnp.where(st_mask, kv_t[pcols, kcols], 0.0))
            vals["st"] = st_out
            vals[("blk", nb)] = (a_pairs, q_inter, [st_p.astype(BF16) for st_p in st_in])

        def output_stage(nb):
            lane_low = lax.broadcasted_iota(jnp.int32, (L, MXU_N), 1) < GLA_DV
            ggla = ggla_ref[...]
            a_pairs, q_inter, st_b = vals[("blk", nb)]
            rows = slice(nb * L, (nb + 1) * L)
            for pair in range(GLA_HEADS // 2):
                pcols = slice(pair * MXU_N, (pair + 1) * MXU_N)
                kcols = slice(pair * 2 * GLA_DK, (pair + 1) * 2 * GLA_DK)
                v_pair = pj[rows, COL_V + pair * MXU_N:COL_V + (pair + 1) * MXU_N]
                v_bd = jnp.concatenate([jnp.where(lane_low, v_pair, 0.0),
                                        jnp.where(lane_low, 0.0, v_pair)], axis=0).astype(BF16)
                o_inter = lax.dot_general(q_inter[:, kcols], st_b[pair], _NT_DIMS,
                                          preferred_element_type=F32)
                o_pair = jnp.dot(a_pairs[pair], v_bd, preferred_element_type=F32) + o_inter
                for hh in range(2):
                    hcols = slice(hh * GLA_DV, (hh + 1) * GLA_DV)
                    o_h = _rmsnorm(o_pair[:, hcols], ggla)
                    c0 = COL_R + pair * MXU_N + hh * GLA_DV
                    r_h = pj[rows, c0:c0 + GLA_DV]
                    o_h = o_h * (r_h * jax.nn.sigmoid(r_h))
                    y0 = D_POOL + pair * MXU_N + hh * GLA_DV
                    y_new[rows, y0:y0 + GLA_DV] = o_h.astype(BF16)

        if do_b:
            gate_stage()
        if do_cf:
            out_proj()
        if do_a:
            prenorm()
        if do_b:
            pool_sums()
        if do_a:
            proj_piece(0)
        if do_cf:
            out_finish()
        if do_b:
            prefix_stage()
        if do_a:
            proj_piece(1)
        if do_cf:
            ffn_chunk(0)
        if do_b:
            pool_maps()
            for nb in range(nblk):
                scores_stage(nb)
            st_ref[0] = vals["st"][0]
            st_ref[1] = vals["st"][1]
        if do_cf:
            ffn_chunk(1)
            ffn_chunk(2)
            ffn_chunk(3)
        if do_b:
            for nb in range(nblk):
                output_stage(nb)
        if do_cf:
            for c in range(4, D_FF // FFN_CHUNK):
                ffn_chunk(c)
        if do_a:
            proj_piece(2)
        if do_cf:
            ffn_down()
        if do_a:
            proj_r()
        if do_cf:
            ffn_finish()

    @pl.when(i == 0)
    def _():
        load_weights()
        step(True, False, False)

    @pl.when(i == 1)
    def _():
        step(True, True, False)

    @pl.when(jnp.logical_and(i >= 2, i < n_tiles))
    def _():
        step(True, True, True)

    @pl.when(i == n_tiles)
    def _():
        step(False, True, True)

    @pl.when(i == n_tiles + 1)
    def _():
        step(False, False, True)


def _const_spec(shape):
    zeros = (0,) * len(shape)
    return pl.BlockSpec(shape, lambda i: zeros, pipeline_mode=pl.Buffered(1))


def _decoder_layer(x, g_pre, w_in, w_pool_p, pool_scale, w_gu_p, b_gate, g_gla, w_out, g_post,
                   g_pre2, w_ffn_in, conv_w, conv_b, w_ffn_out, g_post2):
    B, S, D = x.shape
    ts = TILE
    n_tiles = (B * S) // ts
    x2 = x.reshape(B * S, D)
    kern = functools.partial(_layer_kernel, tiles_per_seq=S // ts, n_tiles=n_tiles)
    cs = _const_spec
    hbm = pl.BlockSpec(memory_space=pl.ANY)
    out = pl.pallas_call(
        kern,
        grid=(n_tiles + 2,),
        in_specs=[
            pl.BlockSpec((ts, D), lambda i: (jnp.minimum(i, n_tiles - 1), 0)),
            pl.BlockSpec((ts, D), lambda i: (jnp.maximum(i - 2, 0), 0)),
            cs((1, D)),
            hbm,
            cs((2, MXU_N, MXU_N)),
            cs((1, D_POOL)),
            cs((GATE_PAD, D_GLA_K)),
            cs((1, D_GLA_K)),
            cs((1, GLA_DV)),
            hbm,
            cs((1, D)),
            cs((1, D)),
            hbm,
            cs((3, D_FF)),
            cs((1, D_FF)),
            hbm,
            cs((1, D)),
        ],
        out_specs=pl.BlockSpec((ts, D), lambda i: (jnp.maximum(i - 2, 0), 0)),
        out_shape=jax.ShapeDtypeStruct((B * S, D), F32),
        scratch_shapes=[
            pltpu.VMEM((D, D_IN), BF16),
            pltpu.VMEM((D, D_GLA_V), BF16),
            pltpu.VMEM((D, D), BF16),
            pltpu.VMEM((D, 2 * D_FF), BF16),
            pltpu.VMEM((D_FF, D), BF16),
            pltpu.VMEM((STAGE_SLOTS, W_IN_ROWS, D_IN), F32),
            pltpu.VMEM((STAGE_SLOTS, STAGE_ROWS, D), F32),
            pltpu.VMEM((STAGE_SLOTS, W_UP_ROWS, 2 * D_FF), F32),
            pltpu.SemaphoreType.DMA((STAGE_SLOTS,)),
            pltpu.VMEM((ts, D), BF16),
            pltpu.VMEM((2, ts, D_PROJ), F32),
            pltpu.VMEM((MAX_WINDOW, D_POOL), F32),
            pltpu.VMEM((ts, D_POOL), BF16),
            pltpu.VMEM((GLA_HEADS // 2, 2 * GLA_DV, 2 * GLA_DK), F32),
            pltpu.VMEM((2, ts, D), BF16),
            pltpu.VMEM((ts, D), F32),
            pltpu.VMEM((ts, D), BF16),
            pltpu.VMEM((CONV_TAIL, D_FF), F32),
            pltpu.VMEM((ts, D_FF), BF16),
        ],
        compiler_params=pltpu.CompilerParams(
            dimension_semantics=("arbitrary",),
            vmem_limit_bytes=VMEM_LIMIT_BYTES),
        name="decoder_layer",
    )(x2, x2, g_pre, w_in, w_pool_p, pool_scale, w_gu_p, b_gate, g_gla, w_out, g_post,
      g_pre2, w_ffn_in, conv_w, conv_b, w_ffn_out, g_post2)
    return out.reshape(B, S, D)


def _pack_w_pool(w_pool):
    z = jnp.zeros((POOL_GROUP_DIM, POOL_GROUP_DIM), w_pool.dtype)
    pairs = [jnp.block([[w_pool[2 * p], z], [z, w_pool[2 * p + 1]]]) for p in range(2)]
    return jnp.stack(pairs).astype(BF16)


def kernel(x, g_pre_mix, w_in, w_pool, pool_scale, w_gate_up, b_gate, g_gla_norm, w_out, g_post_mix,
           g_pre_ffn, w_ffn_in, conv_w, conv_b, w_ffn_out, g_post_ffn):
    row = lambda a: a.reshape(1, -1).astype(F32)
    w_gu_p = jnp.pad(w_gate_up, ((0, GATE_PAD - GATE_RANK), (0, 0))).astype(BF16)
    return _decoder_layer(
        x, row(g_pre_mix), w_in.astype(F32), _pack_w_pool(w_pool),
        row(pool_scale), w_gu_p,
        row(b_gate), row(g_gla_norm), w_out.astype(F32), row(g_post_mix), row(g_pre_ffn),
        w_ffn_in.astype(F32), conv_w.astype(F32), row(conv_b), w_ffn_out.astype(F32),
        row(g_post_ffn))
```

```python
import functools
import math

import jax
import jax.numpy as jnp
from jax import lax
from jax.experimental import pallas as pl
from jax.experimental.pallas import tpu as pltpu

D_MODEL = 1024
D_POOL = 512
POOL_WINDOWS = (2, 4, 8, 16)
POOL_GROUP_DIM = 128
MAX_WINDOW = 16
D_GLA_V = 512
GLA_HEADS = 4
GLA_DV = 128
GLA_DK = 64
D_GLA_K = 256
GATE_RANK = 16
GATE_TAU = 16.0
D_FF = 2816
EPS = 1e-6

LANES = 128
MXU_N = 256
GATE_PAD = LANES
D_IN = D_POOL + 2 * D_GLA_K + 2 * D_GLA_V + GATE_RANK
COL_U = 0
COL_Q = COL_U + D_POOL
COL_K = COL_Q + D_GLA_K
COL_V = COL_K + D_GLA_K
COL_G = COL_V + D_GLA_V
COL_R = COL_G + GATE_PAD
D_PROJ = COL_R + D_GLA_V
W_IN_R = COL_G + GATE_RANK
PROJ_PIECES = ((COL_U, D_POOL), (COL_Q, 2 * D_GLA_K), (COL_V, D_GLA_V + GATE_PAD))

TILE = 256
TILES_PER_STEP = 2
GLA_BLOCK = 128
FFN_CHUNK = 256
CONV_TAIL = 8

STAGE_SLOTS = 3
W_IN_ROWS = 64
W_UP_ROWS = 32
STAGE_ROWS = 128

VMEM_LIMIT_BYTES = 56 * 1024 * 1024

F32 = jnp.float32
BF16 = jnp.bfloat16
_NT_DIMS = (((1,), (1,)), ((), ()))
_TN_DIMS = (((0,), (0,)), ((), ()))


def _rmsnorm(x, g):
    return x * lax.rsqrt(jnp.mean(x * x, axis=-1, keepdims=True) + EPS) * g


def _layer_kernel(xn_ref, xp_ref, gpre_ref, win_hbm, wpool_ref, pscale_ref, wgu_ref, bg_ref,
                  ggla_ref, wout_hbm, gpost_ref, gpre2_ref, wfin_hbm, cw_ref, cb_ref, wfout_hbm,
                  gpost2_ref, o_ref, win_ref, wr_ref, wout_ref, wfin_ref, wfout_ref,
                  h_ref, proj_ref, utail_ref, d_ref, st_ref, y_ref,
                  x1_ref, h2_ref, tail_ref, act_ref, *, tiles_per_seq, n_blocks):
    ts = TILE
    L = GLA_BLOCK
    nblk = ts // L
    half = 2 * L
    i = pl.program_id(0)

    def load_weights(stage_in, stage_rows, stage_up, stage_sem):
        def cast_in(dst_rows, chunk):
            win_ref[dst_rows, :] = chunk.astype(BF16)
            wr_ref[dst_rows, :] = chunk[:, W_IN_R:].astype(BF16)

        def cast_to(dst_ref):
            def cast(dst_rows, chunk):
                dst_ref[dst_rows, :] = chunk.astype(BF16)
            return cast

        jobs = []
        for hbm_ref, stage, cast in ((win_hbm, stage_in, cast_in), (wout_hbm, stage_rows, cast_to(wout_ref)),
                                     (wfin_hbm, stage_up, cast_to(wfin_ref)),
                                     (wfout_hbm, stage_rows, cast_to(wfout_ref))):
            chunk_rows = stage.shape[1]
            for r in range(hbm_ref.shape[0] // chunk_rows):
                jobs.append((hbm_ref, pl.ds(r * chunk_rows, chunk_rows), stage, cast))

        def copy(k):
            hbm_ref, rows, stage, _ = jobs[k]
            slot = k % STAGE_SLOTS
            return pltpu.make_async_copy(hbm_ref.at[rows, :], stage.at[slot], stage_sem.at[slot])

        for k in range(STAGE_SLOTS):
            copy(k).start()
        for k in range(len(jobs)):
            _, rows, stage, cast = jobs[k]
            copy(k).wait()
            cast(rows, stage[k % STAGE_SLOTS])
            if k + STAGE_SLOTS < len(jobs):
                copy(k + STAGE_SLOTS).start()

    def step(sub, do_a, do_b, do_cf):
        t = TILES_PER_STEP * i + sub
        slot_a = sub % 2
        slot_b = 1 - slot_a
        rows_t = slice(sub * ts, (sub + 1) * ts)
        j = lax.rem(jnp.maximum(t - 1, 0), tiles_per_seq)
        jf = lax.rem(jnp.maximum(t - 2, 0), tiles_per_seq)
        pnew = proj_ref.at[slot_a]
        pj = proj_ref.at[slot_b]
        y_new = y_ref.at[slot_b]
        y_old = y_ref.at[slot_a]
        vals = {}

        if do_b:
            @pl.when(j == 0)
            def _():
                utail_ref[...] = jnp.zeros_like(utail_ref)
                st_ref[...] = jnp.zeros_like(st_ref)

        if do_cf:
            @pl.when(jf == 0)
            def _():
                tail_ref[...] = jnp.zeros_like(tail_ref)

        def prenorm():
            h_ref[...] = _rmsnorm(xn_ref[rows_t, :], gpre_ref[...]).astype(BF16)

        def proj_piece(p):
            n0, w = PROJ_PIECES[p]
            pnew[:, n0:n0 + w] = jnp.dot(h_ref[...], win_ref[:, n0:n0 + w],
                                         preferred_element_type=F32)

        def proj_r():
            pnew[:, COL_R:COL_R + D_GLA_V] = jnp.dot(h_ref[...], wr_ref[...],
                                                     preferred_element_type=F32)

        def out_proj():
            vals["mix"] = jnp.dot(y_old[...], wout_ref[...], preferred_element_type=F32)

        def out_finish():
            x1 = xp_ref[rows_t, :] + _rmsnorm(vals["mix"], gpost_ref[...])
            x1_ref[...] = x1
            h2_ref[...] = _rmsnorm(x1, gpre2_ref[...]).astype(BF16)

        def ffn_chunk(c):
            row = lax.broadcasted_iota(jnp.int32, (CONV_TAIL, FFN_CHUNK), 0)
            cols = slice(c * FFN_CHUNK, (c + 1) * FFN_CHUNK)
            vcols = slice(D_FF + c * FFN_CHUNK, D_FF + (c + 1) * FFN_CHUNK)
            gate = jnp.dot(h2_ref[...], wfin_ref[:, cols], preferred_element_type=F32)
            val = jnp.dot(h2_ref[...], wfin_ref[:, vcols], preferred_element_type=F32)
            prev = tail_ref[:, cols]
            tail_ref[:, cols] = gate[ts - CONV_TAIL:ts, :]
            p1 = prev[CONV_TAIL - 1:CONV_TAIL, :]
            p2 = prev[CONV_TAIL - 2:CONV_TAIL - 1, :]
            r1 = pltpu.roll(gate, 1, 0)
            r2 = pltpu.roll(gate, 2, 0)
            h1 = jnp.where(row == 0, p1, r1[:CONV_TAIL])
            h2 = jnp.where(row == 0, p2, jnp.where(row == 1, p1, r2[:CONV_TAIL]))
            g1 = jnp.concatenate([h1, r1[CONV_TAIL:]], axis=0)
            g2 = jnp.concatenate([h2, r2[CONV_TAIL:]], axis=0)
            cw = cw_ref[:, cols]
            gc = g2 * cw[0:1, :] + g1 * cw[1:2, :] + gate * cw[2:3, :] + cb_ref[:, cols]
            act = 0.5 * gc * (1.0 + lax.erf(gc * (1.0 / math.sqrt(2.0)))) * val
            act_ref[:, cols] = act.astype(BF16)

        def ffn_down():
            vals["ff"] = jnp.dot(act_ref[...], wfout_ref[...], preferred_element_type=F32)

        def ffn_finish():
            o_ref[rows_t, :] = x1_ref[...] + _rmsnorm(vals["ff"], gpost2_ref[...])

        def pool_sums():
            row = lax.broadcasted_iota(jnp.int32, (MAX_WINDOW, 1), 0)
            pos = (j * ts + row + 1).astype(F32)
            for g, w in enumerate(POOL_WINDOWS):
                cols = slice(g * POOL_GROUP_DIM, (g + 1) * POOL_GROUP_DIM)
                u_g = pj[:, COL_U + g * POOL_GROUP_DIM:COL_U + (g + 1) * POOL_GROUP_DIM]
                ext = jnp.concatenate([utail_ref[:, cols], u_g], axis=0)
                utail_ref[:, cols] = u_g[ts - MAX_WINDOW:, :]
                acc = ext
                span = 1
                while span < w:
                    acc = acc + pltpu.roll(acc, span, 0)
                    span *= 2
                head = acc[MAX_WINDOW:2 * MAX_WINDOW, :] / jnp.minimum(pos, float(w))
                body = acc[2 * MAX_WINDOW:, :] * (1.0 / w)
                d = jnp.concatenate([head, body], axis=0) - u_g
                d_ref[:, cols] = d.astype(BF16)

        def pool_maps():
            for pair in range(2):
                cols = slice(pair * MXU_N, (pair + 1) * MXU_N)
                yp = jnp.dot(d_ref[:, cols], wpool_ref[pair], preferred_element_type=F32)
                y_new[:, cols] = (yp * pscale_ref[:, cols]).astype(BF16)

        def gate_stage():
            glow = pj[:, COL_G:COL_G + GATE_PAD].astype(BF16)
            logits = jnp.dot(glow, wgu_ref[...], preferred_element_type=F32) + bg_ref[...]
            log_a = ((jnp.minimum(logits, 0.0) - jnp.log(1.0 + jnp.exp(-jnp.abs(logits))))
                     * (1.0 / GATE_TAU))
            la_hi = log_a.astype(BF16)
            vals["la"] = (la_hi, (log_a - la_hi.astype(F32)).astype(BF16))

        def prefix_stage():
            la_hi, la_lo = vals["la"]
            ri = lax.broadcasted_iota(jnp.int32, (half, half), 0)
            ci = lax.broadcasted_iota(jnp.int32, (half, half), 1)
            blk_start = ri - lax.rem(ri, L)
            tri_bd = jnp.where(ci <= ri, jnp.where(ci >= blk_start, 1.0, 0.0), 0.0).astype(BF16)
            bcs = []
            for hb in range(ts // half):
                hrows = slice(hb * half, (hb + 1) * half)
                bc2 = (jnp.dot(tri_bd, la_hi[hrows, :], preferred_element_type=F32)
                       + jnp.dot(tri_bd, la_lo[hrows, :], preferred_element_type=F32))
                bcs += [bc2[0:L, :], bc2[L:half, :]]
            vals["bcs"] = bcs
            vals["st"] = [st_ref[0], st_ref[1]]

        def scores_stage(nb):
            lane_first = lax.broadcasted_iota(jnp.int32, (L, 2 * GLA_DK), 1) < GLA_DK
            ri2 = lax.broadcasted_iota(jnp.int32, (L, 2 * L), 0)
            ci2 = lax.broadcasted_iota(jnp.int32, (L, 2 * L), 1)
            causal2 = ri2 >= lax.rem(ci2, L)
            st_row_first = lax.broadcasted_iota(jnp.int32, (2 * GLA_DV, 2 * GLA_DK), 0) < GLA_DV
            st_col_first = lax.broadcasted_iota(jnp.int32, (2 * GLA_DV, 2 * GLA_DK), 1) < GLA_DK
            st_mask = st_row_first == st_col_first
            rows = slice(nb * L, (nb + 1) * L)
            bc = vals["bcs"][nb]
            b_last = bc[L - 1:L, :]
            b_mid = bc[L // 2 - 1:L // 2, :]
            q = pj[rows, COL_Q:COL_Q + D_GLA_K] * (GLA_DK ** -0.5)
            k = pj[rows, COL_K:COL_K + D_GLA_K]
            e_fwd = jnp.exp(bc - b_mid)
            q_mid = q * e_fwd
            k_intra = k * (1.0 / e_fwd)
            q_intra = q_mid.astype(BF16)
            q_inter = (q_mid * jnp.exp(b_mid)).astype(BF16)
            k_hat = (k_intra * jnp.exp(b_last - b_mid)).astype(BF16)
            a_pairs = []
            for pair in range(GLA_HEADS // 2):
                kcols = slice(pair * 2 * GLA_DK, (pair + 1) * 2 * GLA_DK)
                k_p = k_intra[:, kcols]
                k_pair = jnp.concatenate([jnp.where(lane_first, k_p, 0.0),
                                          jnp.where(lane_first, 0.0, k_p)], axis=0).astype(BF16)
                s_pair = lax.dot_general(q_intra[:, kcols], k_pair, _NT_DIMS,
                                         preferred_element_type=F32)
                a_pairs.append(jnp.where(causal2, s_pair, 0.0).astype(BF16))
            v_blk = pj[rows, COL_V:COL_V + D_GLA_V].astype(BF16)
            kv_t = lax.dot_general(v_blk, k_hat, _TN_DIMS, preferred_element_type=F32)
            decay = jnp.exp(b_last)
            st_in = vals["st"]
            st_out = []
            for pair in range(GLA_HEADS // 2):
                pcols = slice(pair * MXU_N, (pair + 1) * MXU_N)
                kcols = slice(pair * 2 * GLA_DK, (pair + 1) * 2 * GLA_DK)
                st_out.append(st_in[pair] * decay[:, kcols]
                              + jnp.where(st_mask, kv_t[pcols, kcols], 0.0))
            vals["st"] = st_out
            vals[("blk", nb)] = (a_pairs, q_inter, [st_p.astype(BF16) for st_p in st_in])

        def output_stage(nb):
            lane_low = lax.broadcasted_iota(jnp.int32, (L, MXU_N), 1) < GLA_DV
            ggla = ggla_ref[...]
            a_pairs, q_inter, st_b = vals[("blk", nb)]
            rows = slice(nb * L, (nb + 1) * L)
            for pair in range(GLA_HEADS // 2):
                pcols = slice(pair * MXU_N, (pair + 1) * MXU_N)
                kcols = slice(pair * 2 * GLA_DK, (pair + 1) * 2 * GLA_DK)
                v_pair = pj[rows, COL_V + pair * MXU_N:COL_V + (pair + 1) * MXU_N]
                v_bd = jnp.concatenate([jnp.where(lane_low, v_pair, 0.0),
                                        jnp.where(lane_low, 0.0, v_pair)], axis=0).astype(BF16)
                o_inter = lax.dot_general(q_inter[:, kcols], st_b[pair], _NT_DIMS,
                                          preferred_element_type=F32)
                o_pair = jnp.dot(a_pairs[pair], v_bd, preferred_element_type=F32) + o_inter
                for hh in range(2):
                    hcols = slice(hh * GLA_DV, (hh + 1) * GLA_DV)
                    o_h = _rmsnorm(o_pair[:, hcols], ggla)
                    c0 = COL_R + pair * MXU_N + hh * GLA_DV
                    r_h = pj[rows, c0:c0 + GLA_DV]
                    o_h = o_h * (r_h * jax.nn.sigmoid(r_h))
                    y0 = D_POOL + pair * MXU_N + hh * GLA_DV
                    y_new[rows, y0:y0 + GLA_DV] = o_h.astype(BF16)

        if do_b:
            gate_stage()
        if do_cf:
            out_proj()
        if do_a:
            prenorm()
        if do_b:
            pool_sums()
        if do_a:
            proj_piece(0)
        if do_cf:
            out_finish()
        if do_b:
            prefix_stage()
        if do_a:
            proj_piece(1)
        if do_cf:
            ffn_chunk(0)
        if do_b:
            pool_maps()
            for nb in range(nblk):
                scores_stage(nb)
            st_ref[0] = vals["st"][0]
            st_ref[1] = vals["st"][1]
        if do_cf:
            ffn_chunk(1)
            ffn_chunk(2)
            ffn_chunk(3)
        if do_b:
            for nb in range(nblk):
                output_stage(nb)
        if do_cf:
            for c in range(4, D_FF // FFN_CHUNK):
                ffn_chunk(c)
        if do_a:
            proj_piece(2)
        if do_cf:
            ffn_down()
        if do_a:
            proj_r()
        if do_cf:
            ffn_finish()

    @pl.when(i == 0)
    def _():
        pl.run_scoped(load_weights,
                      pltpu.VMEM((STAGE_SLOTS, W_IN_ROWS, D_IN), F32),
                      pltpu.VMEM((STAGE_SLOTS, STAGE_ROWS, D_MODEL), F32),
                      pltpu.VMEM((STAGE_SLOTS, W_UP_ROWS, 2 * D_FF), F32),
                      pltpu.SemaphoreType.DMA((STAGE_SLOTS,)))
        step(0, True, False, False)
        step(1, True, True, False)

    @pl.when(jnp.logical_and(i >= 1, i < n_blocks))
    def _():
        step(0, True, True, True)
        step(1, True, True, True)

    @pl.when(i == n_blocks)
    def _():
        step(0, False, True, True)
        step(1, False, False, True)


def _const_spec(shape):
    zeros = (0,) * len(shape)
    return pl.BlockSpec(shape, lambda i: zeros, pipeline_mode=pl.Buffered(1))


def _decoder_layer(x, g_pre, w_in, w_pool_p, pool_scale, w_gu_p, b_gate, g_gla, w_out, g_post,
                   g_pre2, w_ffn_in, conv_w, conv_b, w_ffn_out, g_post2):
    B, S, D = x.shape
    ts = TILE
    bs = TILES_PER_STEP * ts
    n_blocks = (B * S) // bs
    x2 = x.reshape(B * S, D)
    kern = functools.partial(_layer_kernel, tiles_per_seq=S // ts, n_blocks=n_blocks)
    cs = _const_spec
    hbm = pl.BlockSpec(memory_space=pl.ANY)
    out = pl.pallas_call(
        kern,
        grid=(n_blocks + 1,),
        in_specs=[
            pl.BlockSpec((bs, D), lambda i: (jnp.minimum(i, n_blocks - 1), 0)),
            pl.BlockSpec((bs, D), lambda i: (jnp.maximum(i - 1, 0), 0)),
            cs((1, D)),
            hbm,
            cs((2, MXU_N, MXU_N)),
            cs((1, D_POOL)),
            cs((GATE_PAD, D_GLA_K)),
            cs((1, D_GLA_K)),
            cs((1, GLA_DV)),
            hbm,
            cs((1, D)),
            cs((1, D)),
            hbm,
            cs((3, D_FF)),
            cs((1, D_FF)),
            hbm,
            cs((1, D)),
        ],
        out_specs=pl.BlockSpec((bs, D), lambda i: (jnp.maximum(i - 1, 0), 0)),
        out_shape=jax.ShapeDtypeStruct((B * S, D), F32),
        scratch_shapes=[
            pltpu.VMEM((D, D_IN), BF16),
            pltpu.VMEM((D, D_GLA_V), BF16),
            pltpu.VMEM((D, D), BF16),
            pltpu.VMEM((D, 2 * D_FF), BF16),
            pltpu.VMEM((D_FF, D), BF16),
            pltpu.VMEM((ts, D), BF16),
            pltpu.VMEM((2, ts, D_PROJ), F32),
            pltpu.VMEM((MAX_WINDOW, D_POOL), F32),
            pltpu.VMEM((ts, D_POOL), BF16),
            pltpu.VMEM((GLA_HEADS // 2, 2 * GLA_DV, 2 * GLA_DK), F32),
            pltpu.VMEM((2, ts, D), BF16),
            pltpu.VMEM((ts, D), F32),
            pltpu.VMEM((ts, D), BF16),
            pltpu.VMEM((CONV_TAIL, D_FF), F32),
            pltpu.VMEM((ts, D_FF), BF16),
        ],
        compiler_params=pltpu.CompilerParams(
            dimension_semantics=("arbitrary",),
            vmem_limit_bytes=VMEM_LIMIT_BYTES),
        name="decoder_layer",
    )(x2, x2, g_pre, w_in, w_pool_p, pool_scale, w_gu_p, b_gate, g_gla, w_out, g_post,
      g_pre2, w_ffn_in, conv_w, conv_b, w_ffn_out, g_post2)
    return out.reshape(B, S, D)


def _pack_w_pool(w_pool):
    z = jnp.zeros((POOL_GROUP_DIM, POOL_GROUP_DIM), w_pool.dtype)
    pairs = [jnp.block([[w_pool[2 * p], z], [z, w_pool[2 * p + 1]]]) for p in range(2)]
    return jnp.stack(pairs).astype(BF16)


def kernel(x, g_pre_mix, w_in, w_pool, pool_scale, w_gate_up, b_gate, g_gla_norm, w_out, g_post_mix,
           g_pre_ffn, w_ffn_in, conv_w, conv_b, w_ffn_out, g_post_ffn):
    row = lambda a: a.reshape(1, -1).astype(F32)
    w_gu_p = jnp.pad(w_gate_up, ((0, GATE_PAD - GATE_RANK), (0, 0))).astype(BF16)
    return _decoder_layer(
        x, row(g_pre_mix), w_in.astype(F32), _pack_w_pool(w_pool),
        row(pool_scale), w_gu_p,
        row(b_gate), row(g_gla_norm), w_out.astype(F32), row(g_post_mix), row(g_pre_ffn),
        w_ffn_in.astype(F32), conv_w.astype(F32), row(conv_b), w_ffn_out.astype(F32),
        row(g_post_ffn))
```

```python
import functools
import math

import jax
import jax.numpy as jnp
from jax import lax
from jax.experimental import pallas as pl
from jax.experimental.pallas import tpu as pltpu

D_MODEL = 1024
D_POOL = 512
POOL_WINDOWS = (2, 4, 8, 16)
POOL_GROUP_DIM = 128
MAX_WINDOW = 16
D_GLA_V = 512
GLA_HEADS = 4
GLA_DV = 128
GLA_DK = 64
D_GLA_K = 256
GATE_RANK = 16
GATE_TAU = 16.0
D_FF = 2816
EPS = 1e-6

LANES = 128
MXU_N = 256
GATE_PAD = LANES
D_IN = D_POOL + 2 * D_GLA_K + 2 * D_GLA_V + GATE_RANK
COL_U = 0
COL_Q = COL_U + D_POOL
COL_K = COL_Q + D_GLA_K
COL_V = COL_K + D_GLA_K
COL_G = COL_V + D_GLA_V
COL_R = COL_G + GATE_PAD
D_PROJ = COL_R + D_GLA_V
W_IN_R = COL_G + GATE_RANK
PROJ_PIECES = ((COL_U, D_POOL), (COL_Q, 2 * D_GLA_K), (COL_V, D_GLA_V + GATE_PAD))

TILE = 256
TILES_PER_STEP = 2
GLA_BLOCK = 128
GLA_DIAG = 32
FFN_CHUNK = 256
CONV_TAIL = 8

STAGE_SLOTS = 3
W_IN_ROWS = 64
W_UP_ROWS = 32
STAGE_ROWS = 128

VMEM_LIMIT_BYTES = 56 * 1024 * 1024

F32 = jnp.float32
BF16 = jnp.bfloat16
_NT_DIMS = (((1,), (1,)), ((), ()))
_TN_DIMS = (((0,), (0,)), ((), ()))


def _rmsnorm(x, g):
    return x * lax.rsqrt(jnp.mean(x * x, axis=-1, keepdims=True) + EPS) * g


def _layer_kernel(xn_ref, xp_ref, gpre_ref, win_hbm, wpool_ref, pscale_ref, wgu_ref, bg_ref,
                  ggla_ref, wout_hbm, gpost_ref, gpre2_ref, wfin_hbm, cw_ref, cb_ref, wfout_hbm,
                  gpost2_ref, o_ref, win_ref, wr_ref, wout_ref, wfin_ref, wfout_ref,
                  h_ref, proj_ref, utail_ref, d_ref, st_ref, y_ref,
                  x1_ref, h2_ref, tail_ref, act_ref, *, tiles_per_seq, n_blocks):
    ts = TILE
    L = GLA_BLOCK
    nblk = ts // L
    half = 2 * L
    i = pl.program_id(0)

    def load_weights(stage_in, stage_rows, stage_up, stage_sem):
        def cast_in(dst_rows, chunk):
            win_ref[dst_rows, :] = chunk.astype(BF16)
            wr_ref[dst_rows, :] = chunk[:, W_IN_R:].astype(BF16)

        def cast_to(dst_ref):
            def cast(dst_rows, chunk):
                dst_ref[dst_rows, :] = chunk.astype(BF16)
            return cast

        jobs = []
        for hbm_ref, stage, cast in ((win_hbm, stage_in, cast_in), (wout_hbm, stage_rows, cast_to(wout_ref)),
                                     (wfin_hbm, stage_up, cast_to(wfin_ref)),
                                     (wfout_hbm, stage_rows, cast_to(wfout_ref))):
            chunk_rows = stage.shape[1]
            for r in range(hbm_ref.shape[0] // chunk_rows):
                jobs.append((hbm_ref, pl.ds(r * chunk_rows, chunk_rows), stage, cast))

        def copy(k):
            hbm_ref, rows, stage, _ = jobs[k]
            slot = k % STAGE_SLOTS
            return pltpu.make_async_copy(hbm_ref.at[rows, :], stage.at[slot], stage_sem.at[slot])

        for k in range(STAGE_SLOTS):
            copy(k).start()
        for k in range(len(jobs)):
            _, rows, stage, cast = jobs[k]
            copy(k).wait()
            cast(rows, stage[k % STAGE_SLOTS])
            if k + STAGE_SLOTS < len(jobs):
                copy(k + STAGE_SLOTS).start()

    def step(sub, do_a, do_b, do_cf):
        t = TILES_PER_STEP * i + sub
        slot_a = sub % 2
        slot_b = 1 - slot_a
        rows_t = slice(sub * ts, (sub + 1) * ts)
        j = lax.rem(jnp.maximum(t - 1, 0), tiles_per_seq)
        jf = lax.rem(jnp.maximum(t - 2, 0), tiles_per_seq)
        pnew = proj_ref.at[slot_a]
        pj = proj_ref.at[slot_b]
        y_new = y_ref.at[slot_b]
        y_old = y_ref.at[slot_a]
        vals = {}

        if do_b:
            @pl.when(j == 0)
            def _():
                utail_ref[...] = jnp.zeros_like(utail_ref)
                st_ref[...] = jnp.zeros_like(st_ref)

        if do_cf:
            @pl.when(jf == 0)
            def _():
                tail_ref[...] = jnp.zeros_like(tail_ref)

        def prenorm():
            h_ref[...] = _rmsnorm(xn_ref[rows_t, :], gpre_ref[...]).astype(BF16)

        def proj_piece(p):
            n0, w = PROJ_PIECES[p]
            pnew[:, n0:n0 + w] = jnp.dot(h_ref[...], win_ref[:, n0:n0 + w],
                                         preferred_element_type=F32)

        def proj_r():
            pnew[:, COL_R:COL_R + D_GLA_V] = jnp.dot(h_ref[...], wr_ref[...],
                                                     preferred_element_type=F32)

        def out_proj():
            vals["mix"] = jnp.dot(y_old[...], wout_ref[...], preferred_element_type=F32)

        def out_finish():
            x1 = xp_ref[rows_t, :] + _rmsnorm(vals["mix"], gpost_ref[...])
            x1_ref[...] = x1
            h2_ref[...] = _rmsnorm(x1, gpre2_ref[...]).astype(BF16)

        def ffn_chunk(c):
            row = lax.broadcasted_iota(jnp.int32, (CONV_TAIL, FFN_CHUNK), 0)
            cols = slice(c * FFN_CHUNK, (c + 1) * FFN_CHUNK)
            vcols = slice(D_FF + c * FFN_CHUNK, D_FF + (c + 1) * FFN_CHUNK)
            gate = jnp.dot(h2_ref[...], wfin_ref[:, cols], preferred_element_type=F32)
            val = jnp.dot(h2_ref[...], wfin_ref[:, vcols], preferred_element_type=F32)
            prev = tail_ref[:, cols]
            tail_ref[:, cols] = gate[ts - CONV_TAIL:ts, :]
            p1 = prev[CONV_TAIL - 1:CONV_TAIL, :]
            p2 = prev[CONV_TAIL - 2:CONV_TAIL - 1, :]
            r1 = pltpu.roll(gate, 1, 0)
            r2 = pltpu.roll(gate, 2, 0)
            h1 = jnp.where(row == 0, p1, r1[:CONV_TAIL])
            h2 = jnp.where(row == 0, p2, jnp.where(row == 1, p1, r2[:CONV_TAIL]))
            g1 = jnp.concatenate([h1, r1[CONV_TAIL:]], axis=0)
            g2 = jnp.concatenate([h2, r2[CONV_TAIL:]], axis=0)
            cw = cw_ref[:, cols]
            gc = g2 * cw[0:1, :] + g1 * cw[1:2, :] + gate * cw[2:3, :] + cb_ref[:, cols]
            act = 0.5 * gc * (1.0 + lax.erf(gc * (1.0 / math.sqrt(2.0)))) * val
            act_ref[:, cols] = act.astype(BF16)

        def ffn_down():
            vals["ff"] = jnp.dot(act_ref[...], wfout_ref[...], preferred_element_type=F32)

        def ffn_finish():
            o_ref[rows_t, :] = x1_ref[...] + _rmsnorm(vals["ff"], gpost2_ref[...])

        def pool_sums():
            row = lax.broadcasted_iota(jnp.int32, (MAX_WINDOW, 1), 0)
            pos = (j * ts + row + 1).astype(F32)
            for g, w in enumerate(POOL_WINDOWS):
                cols = slice(g * POOL_GROUP_DIM, (g + 1) * POOL_GROUP_DIM)
                u_g = pj[:, COL_U + g * POOL_GROUP_DIM:COL_U + (g + 1) * POOL_GROUP_DIM]
                ext = jnp.concatenate([utail_ref[:, cols], u_g], axis=0)
                utail_ref[:, cols] = u_g[ts - MAX_WINDOW:, :]
                acc = ext
                span = 1
                while span < w:
                    acc = acc + pltpu.roll(acc, span, 0)
                    span *= 2
                head = acc[MAX_WINDOW:2 * MAX_WINDOW, :] / jnp.minimum(pos, float(w))
                body = acc[2 * MAX_WINDOW:, :] * (1.0 / w)
                d = jnp.concatenate([head, body], axis=0) - u_g
                d_ref[:, cols] = d.astype(BF16)

        def pool_maps():
            for pair in range(2):
                cols = slice(pair * MXU_N, (pair + 1) * MXU_N)
                yp = jnp.dot(d_ref[:, cols], wpool_ref[pair], preferred_element_type=F32)
                y_new[:, cols] = (yp * pscale_ref[:, cols]).astype(BF16)

        def gate_stage():
            glow = pj[:, COL_G:COL_G + GATE_PAD].astype(BF16)
            logits = jnp.dot(glow, wgu_ref[...], preferred_element_type=F32) + bg_ref[...]
            log_a = ((jnp.minimum(logits, 0.0) - jnp.log(1.0 + jnp.exp(-jnp.abs(logits))))
                     * (1.0 / GATE_TAU))
            la_hi = log_a.astype(BF16)
            vals["la"] = (la_hi, (log_a - la_hi.astype(F32)).astype(BF16))

        def prefix_stage():
            la_hi, la_lo = vals["la"]
            ri = lax.broadcasted_iota(jnp.int32, (half, half), 0)
            ci = lax.broadcasted_iota(jnp.int32, (half, half), 1)
            blk_start = ri - lax.rem(ri, L)
            tri_bd = jnp.where(ci <= ri, jnp.where(ci >= blk_start, 1.0, 0.0), 0.0).astype(BF16)
            bcs = []
            for hb in range(ts // half):
                hrows = slice(hb * half, (hb + 1) * half)
                bc2 = (jnp.dot(tri_bd, la_hi[hrows, :], preferred_element_type=F32)
                       + jnp.dot(tri_bd, la_lo[hrows, :], preferred_element_type=F32))
                bcs += [bc2[0:L, :], bc2[L:half, :]]
            vals["bcs"] = bcs
            vals["st"] = [st_ref[0], st_ref[1]]

        def scores_stage(nb):
            lane_first = lax.broadcasted_iota(jnp.int32, (L, 2 * GLA_DK), 1) < GLA_DK
            ri2 = lax.broadcasted_iota(jnp.int32, (L, 2 * L), 0)
            ci2 = lax.rem(lax.broadcasted_iota(jnp.int32, (L, 2 * L), 1), L)
            st_row_first = lax.broadcasted_iota(jnp.int32, (2 * GLA_DV, 2 * GLA_DK), 0) < GLA_DV
            st_col_first = lax.broadcasted_iota(jnp.int32, (2 * GLA_DV, 2 * GLA_DK), 1) < GLA_DK
            st_mask = st_row_first == st_col_first
            rows = slice(nb * L, (nb + 1) * L)
            bc = vals["bcs"][nb]
            b_last = bc[L - 1:L, :]
            q = pj[rows, COL_Q:COL_Q + D_GLA_K] * (GLA_DK ** -0.5)
            k = pj[rows, COL_K:COL_K + D_GLA_K]
            q_inter = (q * jnp.exp(bc)).astype(BF16)
            k_hat = (k * jnp.exp(b_last - bc)).astype(BF16)

            def ref_rows(group, offset):
                parts = [jnp.broadcast_to(bc[g0 + offset:g0 + offset + 1, :], (group, D_GLA_K))
                         for g0 in range(0, L, group)]
                return parts[0] if len(parts) == 1 else jnp.concatenate(parts, axis=0)

            levels = []
            size = GLA_DIAG
            levels.append((size, ref_rows(size, size // 2 - 1)))
            while size < L:
                levels.append((2 * size, ref_rows(2 * size, size - 1)))
                size *= 2
            a_pairs = [None] * (GLA_HEADS // 2)
            finer = None
            for size, b_ref in levels:
                e_fwd = jnp.exp(bc - b_ref)
                q_l = (q * e_fwd).astype(BF16)
                k_l = k * (1.0 / e_fwd)
                for pair in range(GLA_HEADS // 2):
                    kcols = slice(pair * 2 * GLA_DK, (pair + 1) * 2 * GLA_DK)
                    k_p = k_l[:, kcols]
                    k_pair = jnp.concatenate([jnp.where(lane_first, k_p, 0.0),
                                              jnp.where(lane_first, 0.0, k_p)], axis=0).astype(BF16)
                    s_pair = lax.dot_general(q_l[:, kcols], k_pair, _NT_DIMS,
                                             preferred_element_type=F32)
                    if finer is None:
                        a_pairs[pair] = s_pair
                    else:
                        same_finer = (ri2 // finer) == (ci2 // finer)
                        a_pairs[pair] = jnp.where(same_finer, a_pairs[pair], s_pair)
                finer = size
            causal2 = ri2 >= ci2
            a_pairs = [jnp.where(causal2, a, 0.0).astype(BF16) for a in a_pairs]
            v_blk = pj[rows, COL_V:COL_V + D_GLA_V].astype(BF16)
            kv_t = lax.dot_general(v_blk, k_hat, _TN_DIMS, preferred_element_type=F32)
            decay = jnp.exp(b_last)
            st_in = vals["st"]
            st_out = []
            for pair in range(GLA_HEADS // 2):
                pcols = slice(pair * MXU_N, (pair + 1) * MXU_N)
                kcols = slice(pair * 2 * GLA_DK, (pair + 1) * 2 * GLA_DK)
                st_out.append(st_in[pair] * decay[:, kcols]
                              + jnp.where(st_mask, kv_t[pcols, kcols], 0.0))
            vals["st"] = st_out
            vals[("blk", nb)] = (a_pairs, q_inter, [st_p.astype(BF16) for st_p in st_in])

        def output_stage(nb):
            lane_low = lax.broadcasted_iota(jnp.int32, (L, MXU_N), 1) < GLA_DV
            ggla = ggla_ref[...]
            a_pairs, q_inter, st_b = vals[("blk", nb)]
            rows = slice(nb * L, (nb + 1) * L)
            for pair in range(GLA_HEADS // 2):
                pcols = slice(pair * MXU_N, (pair + 1) * MXU_N)
                kcols = slice(pair * 2 * GLA_DK, (pair + 1) * 2 * GLA_DK)
                v_pair = pj[rows, COL_V + pair * MXU_N:COL_V + (pair + 1) * MXU_N]
                v_bd = jnp.concatenate([jnp.where(lane_low, v_pair, 0.0),
                                        jnp.where(lane_low, 0.0, v_pair)], axis=0).astype(BF16)
                o_inter = lax.dot_general(q_inter[:, kcols], st_b[pair], _NT_DIMS,
                                          preferred_element_type=F32)
                o_pair = jnp.dot(a_pairs[pair], v_bd, preferred_element_type=F32) + o_inter
                for hh in range(2):
                    hcols = slice(hh * GLA_DV, (hh + 1) * GLA_DV)
                    o_h = _rmsnorm(o_pair[:, hcols], ggla)
                    c0 = COL_R + pair * MXU_N + hh * GLA_DV
                    r_h = pj[rows, c0:c0 + GLA_DV]
                    o_h = o_h * (r_h * jax.nn.sigmoid(r_h))
                    y0 = D_POOL + pair * MXU_N + hh * GLA_DV
                    y_new[rows, y0:y0 + GLA_DV] = o_h.astype(BF16)

        if do_b:
            gate_stage()
        if do_cf:
            out_proj()
        if do_a:
            prenorm()
        if do_b:
            pool_sums()
        if do_a:
            proj_piece(0)
        if do_cf:
            out_finish()
        if do_b:
            prefix_stage()
        if do_a:
            proj_piece(1)
        if do_cf:
            ffn_chunk(0)
        if do_b:
            pool_maps()
            for nb in range(nblk):
                scores_stage(nb)
            st_ref[0] = vals["st"][0]
            st_ref[1] = vals["st"][1]
        if do_cf:
            ffn_chunk(1)
            ffn_chunk(2)
            ffn_chunk(3)
        if do_b:
            for nb in range(nblk):
                output_stage(nb)
        if do_cf:
            for c in range(4, D_FF // FFN_CHUNK):
                ffn_chunk(c)
        if do_a:
            proj_piece(2)
        if do_cf:
            ffn_down()
        if do_a:
            proj_r()
        if do_cf:
            ffn_finish()

    @pl.when(i == 0)
    def _():
        pl.run_scoped(load_weights,
                      pltpu.VMEM((STAGE_SLOTS, W_IN_ROWS, D_IN), F32),
                      pltpu.VMEM((STAGE_SLOTS, STAGE_ROWS, D_MODEL), F32),
                      pltpu.VMEM((STAGE_SLOTS, W_UP_ROWS, 2 * D_FF), F32),
                      pltpu.SemaphoreType.DMA((STAGE_SLOTS,)))
        step(0, True, False, False)
        step(1, True, True, False)

    @pl.when(jnp.logical_and(i >= 1, i < n_blocks))
    def _():
        step(0, True, True, True)
        step(1, True, True, True)

    @pl.when(i == n_blocks)
    def _():
        step(0, False, True, True)
        step(1, False, False, True)


def _const_spec(shape):
    zeros = (0,) * len(shape)
    return pl.BlockSpec(shape, lambda i: zeros, pipeline_mode=pl.Buffered(1))


def _decoder_layer(x, g_pre, w_in, w_pool_p, pool_scale, w_gu_p, b_gate, g_gla, w_out, g_post,
                   g_pre2, w_ffn_in, conv_w, conv_b, w_ffn_out, g_post2):
    B, S, D = x.shape
    ts = TILE
    bs = TILES_PER_STEP * ts
    n_blocks = (B * S) // bs
    x2 = x.reshape(B * S, D)
    kern = functools.partial(_layer_kernel, tiles_per_seq=S // ts, n_blocks=n_blocks)
    cs = _const_spec
    hbm = pl.BlockSpec(memory_space=pl.ANY)
    out = pl.pallas_call(
        kern,
        grid=(n_blocks + 1,),
        in_specs=[
            pl.BlockSpec((bs, D), lambda i: (jnp.minimum(i, n_blocks - 1), 0)),
            pl.BlockSpec((bs, D), lambda i: (jnp.maximum(i - 1, 0), 0)),
            cs((1, D)),
            hbm,
            cs((2, MXU_N, MXU_N)),
            cs((1, D_POOL)),
            cs((GATE_PAD, D_GLA_K)),
            cs((1, D_GLA_K)),
            cs((1, GLA_DV)),
            hbm,
            cs((1, D)),
            cs((1, D)),
            hbm,
            cs((3, D_FF)),
            cs((1, D_FF)),
            hbm,
            cs((1, D)),
        ],
        out_specs=pl.BlockSpec((bs, D), lambda i: (jnp.maximum(i - 1, 0), 0)),
        out_shape=jax.ShapeDtypeStruct((B * S, D), F32),
        scratch_shapes=[
            pltpu.VMEM((D, D_IN), BF16),
            pltpu.VMEM((D, D_GLA_V), BF16),
            pltpu.VMEM((D, D), BF16),
            pltpu.VMEM((D, 2 * D_FF), BF16),
            pltpu.VMEM((D_FF, D), BF16),
            pltpu.VMEM((ts, D), BF16),
            pltpu.VMEM((2, ts, D_PROJ), F32),
            pltpu.VMEM((MAX_WINDOW, D_POOL), F32),
            pltpu.VMEM((ts, D_POOL), BF16),
            pltpu.VMEM((GLA_HEADS // 2, 2 * GLA_DV, 2 * GLA_DK), F32),
            pltpu.VMEM((2, ts, D), BF16),
            pltpu.VMEM((ts, D), F32),
            pltpu.VMEM((ts, D), BF16),
            pltpu.VMEM((CONV_TAIL, D_FF), F32),
            pltpu.VMEM((ts, D_FF), BF16),
        ],
        compiler_params=pltpu.CompilerParams(
            dimension_semantics=("arbitrary",),
            vmem_limit_bytes=VMEM_LIMIT_BYTES),
        name="decoder_layer",
    )(x2, x2, g_pre, w_in, w_pool_p, pool_scale, w_gu_p, b_gate, g_gla, w_out, g_post,
      g_pre2, w_ffn_in, conv_w, conv_b, w_ffn_out, g_post2)
    return out.reshape(B, S, D)


def _pack_w_pool(w_pool):
    z = jnp.zeros((POOL_GROUP_DIM, POOL_GROUP_DIM), w_pool.dtype)
    pairs = [jnp.block([[w_pool[2 * p], z], [z, w_pool[2 * p + 1]]]) for p in range(2)]
    return jnp.stack(pairs).astype(BF16)


def kernel(x, g_pre_mix, w_in, w_pool, pool_scale, w_gate_up, b_gate, g_gla_norm, w_out, g_post_mix,
           g_pre_ffn, w_ffn_in, conv_w, conv_b, w_ffn_out, g_post_ffn):
    row = lambda a: a.reshape(1, -1).astype(F32)
    w_gu_p = jnp.pad(w_gate_up, ((0, GATE_PAD - GATE_RANK), (0, 0))).astype(BF16)
    return _decoder_layer(
        x, row(g_pre_mix), w_in.astype(F32), _pack_w_pool(w_pool),
        row(pool_scale), w_gu_p,
        row(b_gate), row(g_gla_norm), w_out.astype(F32), row(g_post_mix), row(g_pre_ffn),
        w_ffn_in.astype(F32), conv_w.astype(F32), row(conv_b), w_ffn_out.astype(F32),
        row(g_post_ffn))
```

```python
import functools
import math

import jax
import jax.numpy as jnp
from jax import lax
from jax.experimental import pallas as pl
from jax.experimental.pallas import tpu as pltpu

D_MODEL = 1024
D_POOL = 512
POOL_WINDOWS = (2, 4, 8, 16)
POOL_GROUP_DIM = 128
MAX_WINDOW = 16
D_GLA_V = 512
GLA_HEADS = 4
GLA_DV = 128
GLA_DK = 64
D_GLA_K = 256
GATE_RANK = 16
GATE_TAU = 16.0
D_FF = 2816
EPS = 1e-6

LANES = 128
MXU_N = 256
GATE_PAD = LANES
COL_U = 0
COL_Q = COL_U + D_POOL
COL_K = COL_Q + D_GLA_K
COL_V = COL_K + D_GLA_K
COL_G = COL_V + D_GLA_V
COL_R = COL_G + GATE_PAD
D_PROJ = COL_R + D_GLA_V
W_IN_R = COL_G + GATE_RANK
PROJ_PIECES = ((COL_U, D_POOL), (COL_Q, 2 * D_GLA_K), (COL_V, D_GLA_V + GATE_PAD))

TILE = 256
TILES_PER_STEP = 2
GLA_BLOCK = 128
GLA_DIAG = 32
FFN_CHUNK = 256
CONV_TAIL = 8

STAGE_SLOTS = 3
W_IN_ROWS = 128
W_IN_COLS = COL_G + GATE_PAD
W_UP_ROWS = 32
STAGE_ROWS = 128

VMEM_LIMIT_BYTES = 56 * 1024 * 1024

F32 = jnp.float32
BF16 = jnp.bfloat16
_NT_DIMS = (((1,), (1,)), ((), ()))
_TN_DIMS = (((0,), (0,)), ((), ()))


def _rmsnorm(x, g):
    return x * lax.rsqrt(jnp.mean(x * x, axis=-1, keepdims=True) + EPS) * g


def _layer_kernel(xn_ref, xp_ref, gpre_ref, win_hbm, wpool_ref, pscale_ref, wgu_ref, bg_ref,
                  ggla_ref, wout_hbm, gpost_ref, gpre2_ref, wfin_hbm, cw_ref, cb_ref, wfout_hbm,
                  gpost2_ref, o_ref, win_ref, wr_ref, wout_ref, wfin_ref, wfout_ref,
                  h_ref, proj_ref, utail_ref, d_ref, st_ref, y_ref,
                  x1_ref, h2_ref, tail_ref, act_ref, *, tiles_per_seq, n_blocks):
    ts = TILE
    L = GLA_BLOCK
    nblk = ts // L
    half = 2 * L
    i = pl.program_id(0)

    def load_weights(stage_in, stage_rows, stage_up, stage_sem):
        def cast_in_t(dst_ref):
            def cast(c0, chunk):
                dst_ref[:, c0:c0 + W_IN_ROWS] = chunk.T.astype(BF16)
            return cast

        def cast_to(dst_ref):
            def cast(r0, chunk):
                dst_ref[r0:r0 + chunk.shape[0], :] = chunk.astype(BF16)
            return cast

        jobs = []
        for c0 in range(0, W_IN_COLS, W_IN_ROWS):
            jobs.append((win_hbm, c0, c0, stage_in, cast_in_t(win_ref)))
        for c0 in range(0, D_GLA_V, W_IN_ROWS):
            jobs.append((win_hbm, W_IN_R + c0, c0, stage_in, cast_in_t(wr_ref)))
        for hbm_ref, stage, dst_ref in ((wout_hbm, stage_rows, wout_ref), (wfin_hbm, stage_up, wfin_ref),
                                        (wfout_hbm, stage_rows, wfout_ref)):
            chunk_rows = stage.shape[1]
            for r0 in range(0, hbm_ref.shape[0], chunk_rows):
                jobs.append((hbm_ref, r0, r0, stage, cast_to(dst_ref)))

        def copy(k):
            hbm_ref, r0, _, stage, _ = jobs[k]
            slot = k % STAGE_SLOTS
            return pltpu.make_async_copy(hbm_ref.at[pl.ds(r0, stage.shape[1]), :], stage.at[slot],
                                         stage_sem.at[slot])

        for k in range(STAGE_SLOTS):
            copy(k).start()
        for k in range(len(jobs)):
            _, _, d0, stage, cast = jobs[k]
            copy(k).wait()
            cast(d0, stage[k % STAGE_SLOTS])
            if k + STAGE_SLOTS < len(jobs):
                copy(k + STAGE_SLOTS).start()

    def step(sub, do_a, do_b, do_cf):
        t = TILES_PER_STEP * i + sub
        slot_a = sub % 2
        slot_b = 1 - slot_a
        rows_t = slice(sub * ts, (sub + 1) * ts)
        j = lax.rem(jnp.maximum(t - 1, 0), tiles_per_seq)
        jf = lax.rem(jnp.maximum(t - 2, 0), tiles_per_seq)
        pnew = proj_ref.at[slot_a]
        pj = proj_ref.at[slot_b]
        y_new = y_ref.at[slot_b]
        y_old = y_ref.at[slot_a]
        vals = {}

        if do_b:
            @pl.when(j == 0)
            def _():
                utail_ref[...] = jnp.zeros_like(utail_ref)
                st_ref[...] = jnp.zeros_like(st_ref)

        if do_cf:
            @pl.when(jf == 0)
            def _():
                tail_ref[...] = jnp.zeros_like(tail_ref)

        def prenorm():
            h_ref[...] = _rmsnorm(xn_ref[rows_t, :], gpre_ref[...]).astype(BF16)

        def proj_piece(p):
            n0, w = PROJ_PIECES[p]
            pnew[:, n0:n0 + w] = jnp.dot(h_ref[...], win_ref[:, n0:n0 + w],
                                         preferred_element_type=F32)

        def proj_r():
            pnew[:, COL_R:COL_R + D_GLA_V] = jnp.dot(h_ref[...], wr_ref[...],
                                                     preferred_element_type=F32)

        def out_proj():
            vals["mix"] = jnp.dot(y_old[...], wout_ref[...], preferred_element_type=F32)

        def out_finish():
            x1 = xp_ref[rows_t, :] + _rmsnorm(vals["mix"], gpost_ref[...])
            x1_ref[...] = x1
            h2_ref[...] = _rmsnorm(x1, gpre2_ref[...]).astype(BF16)

        def ffn_chunk(c):
            row = lax.broadcasted_iota(jnp.int32, (CONV_TAIL, FFN_CHUNK), 0)
            cols = slice(c * FFN_CHUNK, (c + 1) * FFN_CHUNK)
            vcols = slice(D_FF + c * FFN_CHUNK, D_FF + (c + 1) * FFN_CHUNK)
            gate = jnp.dot(h2_ref[...], wfin_ref[:, cols], preferred_element_type=F32)
            val = jnp.dot(h2_ref[...], wfin_ref[:, vcols], preferred_element_type=F32)
            prev = tail_ref[:, cols]
            tail_ref[:, cols] = gate[ts - CONV_TAIL:ts, :]
            p1 = prev[CONV_TAIL - 1:CONV_TAIL, :]
            p2 = prev[CONV_TAIL - 2:CONV_TAIL - 1, :]
            r1 = pltpu.roll(gate, 1, 0)
            r2 = pltpu.roll(gate, 2, 0)
            h1 = jnp.where(row == 0, p1, r1[:CONV_TAIL])
            h2 = jnp.where(row == 0, p2, jnp.where(row == 1, p1, r2[:CONV_TAIL]))
            g1 = jnp.concatenate([h1, r1[CONV_TAIL:]], axis=0)
            g2 = jnp.concatenate([h2, r2[CONV_TAIL:]], axis=0)
            cw = cw_ref[:, cols]
            gc = g2 * cw[0:1, :] + g1 * cw[1:2, :] + gate * cw[2:3, :] + cb_ref[:, cols]
            act = 0.5 * gc * (1.0 + lax.erf(gc * (1.0 / math.sqrt(2.0)))) * val
            act_ref[:, cols] = act.astype(BF16)

        def ffn_down():
            vals["ff"] = jnp.dot(act_ref[...], wfout_ref[...], preferred_element_type=F32)

        def ffn_finish():
            o_ref[rows_t, :] = x1_ref[...] + _rmsnorm(vals["ff"], gpost2_ref[...])

        def pool_sums():
            row = lax.broadcasted_iota(jnp.int32, (MAX_WINDOW, 1), 0)
            pos = (j * ts + row + 1).astype(F32)
            for g, w in enumerate(POOL_WINDOWS):
                cols = slice(g * POOL_GROUP_DIM, (g + 1) * POOL_GROUP_DIM)
                u_g = pj[:, COL_U + g * POOL_GROUP_DIM:COL_U + (g + 1) * POOL_GROUP_DIM]
                ext = jnp.concatenate([utail_ref[:, cols], u_g], axis=0)
                utail_ref[:, cols] = u_g[ts - MAX_WINDOW:, :]
                acc = ext
                span = 1
                while span < w:
                    acc = acc + pltpu.roll(acc, span, 0)
                    span *= 2
                head = acc[MAX_WINDOW:2 * MAX_WINDOW, :] / jnp.minimum(pos, float(w))
                body = acc[2 * MAX_WINDOW:, :] * (1.0 / w)
                d = jnp.concatenate([head, body], axis=0) - u_g
                d_ref[:, cols] = d.astype(BF16)

        def pool_maps():
            for pair in range(2):
                cols = slice(pair * MXU_N, (pair + 1) * MXU_N)
                yp = jnp.dot(d_ref[:, cols], wpool_ref[pair], preferred_element_type=F32)
                y_new[:, cols] = (yp * pscale_ref[:, cols]).astype(BF16)

        def gate_stage():
            glow = pj[:, COL_G:COL_G + GATE_PAD].astype(BF16)
            logits = jnp.dot(glow, wgu_ref[...], preferred_element_type=F32) + bg_ref[...]
            log_a = ((jnp.minimum(logits, 0.0) - jnp.log(1.0 + jnp.exp(-jnp.abs(logits))))
                     * (1.0 / GATE_TAU))
            la_hi = log_a.astype(BF16)
            vals["la"] = (la_hi, (log_a - la_hi.astype(F32)).astype(BF16))

        def prefix_stage():
            la_hi, la_lo = vals["la"]
            ri = lax.broadcasted_iota(jnp.int32, (half, half), 0)
            ci = lax.broadcasted_iota(jnp.int32, (half, half), 1)
            blk_start = ri - lax.rem(ri, L)
            tri_bd = jnp.where(ci <= ri, jnp.where(ci >= blk_start, 1.0, 0.0), 0.0).astype(BF16)
            bcs = []
            for hb in range(ts // half):
                hrows = slice(hb * half, (hb + 1) * half)
                bc2 = (jnp.dot(tri_bd, la_hi[hrows, :], preferred_element_type=F32)
                       + jnp.dot(tri_bd, la_lo[hrows, :], preferred_element_type=F32))
                bcs += [bc2[0:L, :], bc2[L:half, :]]
            vals["bcs"] = bcs
            vals["st"] = [st_ref[0], st_ref[1]]

        def scores_stage(nb):
            lane_first = lax.broadcasted_iota(jnp.int32, (L, 2 * GLA_DK), 1) < GLA_DK
            ri2 = lax.broadcasted_iota(jnp.int32, (L, 2 * L), 0)
            ci2 = lax.rem(lax.broadcasted_iota(jnp.int32, (L, 2 * L), 1), L)
            st_row_first = lax.broadcasted_iota(jnp.int32, (2 * GLA_DV, 2 * GLA_DK), 0) < GLA_DV
            st_col_first = lax.broadcasted_iota(jnp.int32, (2 * GLA_DV, 2 * GLA_DK), 1) < GLA_DK
            st_mask = st_row_first == st_col_first
            rows = slice(nb * L, (nb + 1) * L)
            bc = vals["bcs"][nb]
            b_last = bc[L - 1:L, :]
            q = pj[rows, COL_Q:COL_Q + D_GLA_K] * (GLA_DK ** -0.5)
            k = pj[rows, COL_K:COL_K + D_GLA_K]
            q_inter = (q * jnp.exp(bc)).astype(BF16)
            k_hat = (k * jnp.exp(b_last - bc)).astype(BF16)

            def ref_rows(group, offset):
                parts = [jnp.broadcast_to(bc[g0 + offset:g0 + offset + 1, :], (group, D_GLA_K))
                         for g0 in range(0, L, group)]
                return parts[0] if len(parts) == 1 else jnp.concatenate(parts, axis=0)

            levels = []
            size = GLA_DIAG
            levels.append((size, ref_rows(size, size // 2 - 1)))
            while size < L:
                levels.append((2 * size, ref_rows(2 * size, size - 1)))
                size *= 2
            a_pairs = [None] * (GLA_HEADS // 2)
            finer = None
            for size, b_ref in levels:
                e_fwd = jnp.exp(bc - b_ref)
                q_l = (q * e_fwd).astype(BF16)
                k_l = k * (1.0 / e_fwd)
                for pair in range(GLA_HEADS // 2):
                    kcols = slice(pair * 2 * GLA_DK, (pair + 1) * 2 * GLA_DK)
                    k_p = k_l[:, kcols]
                    k_pair = jnp.concatenate([jnp.where(lane_first, k_p, 0.0),
                                              jnp.where(lane_first, 0.0, k_p)], axis=0).astype(BF16)
                    s_pair = lax.dot_general(q_l[:, kcols], k_pair, _NT_DIMS,
                                             preferred_element_type=F32)
                    if finer is None:
                        a_pairs[pair] = s_pair
                    else:
                        same_finer = (ri2 // finer) == (ci2 // finer)
                        a_pairs[pair] = jnp.where(same_finer, a_pairs[pair], s_pair)
                finer = size
            causal2 = ri2 >= ci2
            a_pairs = [jnp.where(causal2, a, 0.0).astype(BF16) for a in a_pairs]
            v_blk = pj[rows, COL_V:COL_V + D_GLA_V].astype(BF16)
            kv_t = lax.dot_general(v_blk, k_hat, _TN_DIMS, preferred_element_type=F32)
            decay = jnp.exp(b_last)
            st_in = vals["st"]
            st_out = []
            for pair in range(GLA_HEADS // 2):
                pcols = slice(pair * MXU_N, (pair + 1) * MXU_N)
                kcols = slice(pair * 2 * GLA_DK, (pair + 1) * 2 * GLA_DK)
                st_out.append(st_in[pair] * decay[:, kcols]
                              + jnp.where(st_mask, kv_t[pcols, kcols], 0.0))
            vals["st"] = st_out
            vals[("blk", nb)] = (a_pairs, q_inter, [st_p.astype(BF16) for st_p in st_in])

        def output_stage(nb):
            lane_low = lax.broadcasted_iota(jnp.int32, (L, MXU_N), 1) < GLA_DV
            ggla = ggla_ref[...]
            a_pairs, q_inter, st_b = vals[("blk", nb)]
            rows = slice(nb * L, (nb + 1) * L)
            for pair in range(GLA_HEADS // 2):
                pcols = slice(pair * MXU_N, (pair + 1) * MXU_N)
                kcols = slice(pair * 2 * GLA_DK, (pair + 1) * 2 * GLA_DK)
                v_pair = pj[rows, COL_V + pair * MXU_N:COL_V + (pair + 1) * MXU_N]
                v_bd = jnp.concatenate([jnp.where(lane_low, v_pair, 0.0),
                                        jnp.where(lane_low, 0.0, v_pair)], axis=0).astype(BF16)
                o_inter = lax.dot_general(q_inter[:, kcols], st_b[pair], _NT_DIMS,
                                          preferred_element_type=F32)
                o_pair = jnp.dot(a_pairs[pair], v_bd, preferred_element_type=F32) + o_inter
                for hh in range(2):
                    hcols = slice(hh * GLA_DV, (hh + 1) * GLA_DV)
                    o_h = _rmsnorm(o_pair[:, hcols], ggla)
                    c0 = COL_R + pair * MXU_N + hh * GLA_DV
                    r_h = pj[rows, c0:c0 + GLA_DV]
                    o_h = o_h * (r_h * jax.nn.sigmoid(r_h))
                    y0 = D_POOL + pair * MXU_N + hh * GLA_DV
                    y_new[rows, y0:y0 + GLA_DV] = o_h.astype(BF16)

        if do_b:
            gate_stage()
        if do_cf:
            out_proj()
        if do_a:
            prenorm()
        if do_b:
            pool_sums()
        if do_a:
            proj_piece(0)
        if do_cf:
            out_finish()
        if do_b:
            prefix_stage()
        if do_a:
            proj_piece(1)
        if do_cf:
            ffn_chunk(0)
        if do_b:
            pool_maps()
            for nb in range(nblk):
                scores_stage(nb)
            st_ref[0] = vals["st"][0]
            st_ref[1] = vals["st"][1]
        if do_cf:
            ffn_chunk(1)
            ffn_chunk(2)
            ffn_chunk(3)
        if do_b:
            for nb in range(nblk):
                output_stage(nb)
        if do_cf:
            for c in range(4, D_FF // FFN_CHUNK):
                ffn_chunk(c)
        if do_a:
            proj_piece(2)
        if do_cf:
            ffn_down()
        if do_a:
            proj_r()
        if do_cf:
            ffn_finish()

    @pl.when(i == 0)
    def _():
        pl.run_scoped(load_weights,
                      pltpu.VMEM((STAGE_SLOTS, W_IN_ROWS, D_MODEL), F32),
                      pltpu.VMEM((STAGE_SLOTS, STAGE_ROWS, D_MODEL), F32),
                      pltpu.VMEM((STAGE_SLOTS, W_UP_ROWS, 2 * D_FF), F32),
                      pltpu.SemaphoreType.DMA((STAGE_SLOTS,)))
        step(0, True, False, False)
        step(1, True, True, False)

    @pl.when(jnp.logical_and(i >= 1, i < n_blocks))
    def _():
        step(0, True, True, True)
        step(1, True, True, True)

    @pl.when(i == n_blocks)
    def _():
        step(0, False, True, True)
        step(1, False, False, True)


def _const_spec(shape):
    zeros = (0,) * len(shape)
    return pl.BlockSpec(shape, lambda i: zeros, pipeline_mode=pl.Buffered(1))


def _decoder_layer(x, g_pre, w_in_t, w_pool_p, pool_scale, w_gu_p, b_gate, g_gla, w_out, g_post,
                   g_pre2, w_ffn_in, conv_w, conv_b, w_ffn_out, g_post2):
    B, S, D = x.shape
    ts = TILE
    bs = TILES_PER_STEP * ts
    n_blocks = (B * S) // bs
    x2 = x.reshape(B * S, D)
    kern = functools.partial(_layer_kernel, tiles_per_seq=S // ts, n_blocks=n_blocks)
    cs = _const_spec
    hbm = pl.BlockSpec(memory_space=pl.ANY)
    out = pl.pallas_call(
        kern,
        grid=(n_blocks + 1,),
        in_specs=[
            pl.BlockSpec((bs, D), lambda i: (jnp.minimum(i, n_blocks - 1), 0)),
            pl.BlockSpec((bs, D), lambda i: (jnp.maximum(i - 1, 0), 0)),
            cs((1, D)),
            hbm,
            cs((2, MXU_N, MXU_N)),
            cs((1, D_POOL)),
            cs((GATE_PAD, D_GLA_K)),
            cs((1, D_GLA_K)),
            cs((1, GLA_DV)),
            hbm,
            cs((1, D)),
            cs((1, D)),
            hbm,
            cs((3, D_FF)),
            cs((1, D_FF)),
            hbm,
            cs((1, D)),
        ],
        out_specs=pl.BlockSpec((bs, D), lambda i: (jnp.maximum(i - 1, 0), 0)),
        out_shape=jax.ShapeDtypeStruct((B * S, D), F32),
        scratch_shapes=[
            pltpu.VMEM((D, W_IN_COLS), BF16),
            pltpu.VMEM((D, D_GLA_V), BF16),
            pltpu.VMEM((D, D), BF16),
            pltpu.VMEM((D, 2 * D_FF), BF16),
            pltpu.VMEM((D_FF, D), BF16),
            pltpu.VMEM((ts, D), BF16),
            pltpu.VMEM((2, ts, D_PROJ), F32),
            pltpu.VMEM((MAX_WINDOW, D_POOL), F32),
            pltpu.VMEM((ts, D_POOL), BF16),
            pltpu.VMEM((GLA_HEADS // 2, 2 * GLA_DV, 2 * GLA_DK), F32),
            pltpu.VMEM((2, ts, D), BF16),
            pltpu.VMEM((ts, D), F32),
            pltpu.VMEM((ts, D), BF16),
            pltpu.VMEM((CONV_TAIL, D_FF), F32),
            pltpu.VMEM((ts, D_FF), BF16),
        ],
        compiler_params=pltpu.CompilerParams(
            dimension_semantics=("arbitrary",),
            vmem_limit_bytes=VMEM_LIMIT_BYTES),
        name="decoder_layer",
    )(x2, x2, g_pre, w_in_t, w_pool_p, pool_scale, w_gu_p, b_gate, g_gla, w_out, g_post,
      g_pre2, w_ffn_in, conv_w, conv_b, w_ffn_out, g_post2)
    return out.reshape(B, S, D)


def _pack_w_pool(w_pool):
    z = jnp.zeros((POOL_GROUP_DIM, POOL_GROUP_DIM), w_pool.dtype)
    pairs = [jnp.block([[w_pool[2 * p], z], [z, w_pool[2 * p + 1]]]) for p in range(2)]
    return jnp.stack(pairs).astype(BF16)


def kernel(x, g_pre_mix, w_in, w_pool, pool_scale, w_gate_up, b_gate, g_gla_norm, w_out, g_post_mix,
           g_pre_ffn, w_ffn_in, conv_w, conv_b, w_ffn_out, g_post_ffn):
    row = lambda a: a.reshape(1, -1).astype(F32)
    w_gu_p = jnp.pad(w_gate_up, ((0, GATE_PAD - GATE_RANK), (0, 0))).astype(BF16)
    return _decoder_layer(
        x, row(g_pre_mix), w_in.astype(F32).T, _pack_w_pool(w_pool),
        row(pool_scale), w_gu_p,
        row(b_gate), row(g_gla_norm), w_out.astype(F32), row(g_post_mix), row(g_pre_ffn),
        w_ffn_in.astype(F32), conv_w.astype(F32), row(conv_b), w_ffn_out.astype(F32),
        row(g_post_ffn))
```

```python
import functools
import math

import jax
import jax.numpy as jnp
from jax import lax
from jax.experimental import pallas as pl
from jax.experimental.pallas import tpu as pltpu

D_MODEL = 1024
D_POOL = 512
POOL_WINDOWS = (2, 4, 8, 16)
POOL_GROUP_DIM = 128
MAX_WINDOW = 16
D_GLA_V = 512
GLA_HEADS = 4
GLA_DV = 128
GLA_DK = 64
D_GLA_K = 256
GATE_RANK = 16
GATE_TAU = 16.0
D_FF = 2816
EPS = 1e-6

LANES = 128
MXU_N = 256
GATE_PAD = LANES
COL_U = 0
COL_Q = COL_U + D_POOL
COL_K = COL_Q + D_GLA_K
COL_V = COL_K + D_GLA_K
COL_G = COL_V + D_GLA_V
COL_R = COL_G + GATE_PAD
D_PROJ = COL_R + D_GLA_V
W_IN_R = COL_G + GATE_RANK
PROJ_PIECES = ((COL_U, D_POOL), (COL_Q, 2 * D_GLA_K), (COL_V, D_GLA_V + GATE_PAD))

TILE = 256
TILES_PER_STEP = 2
GLA_BLOCK = 128
GLA_DIAG = 32
FFN_CHUNK = 256
CONV_TAIL = 8

STAGE_SLOTS = 3
W_IN_ROWS = 128
W_IN_COLS = COL_G + GATE_PAD
W_UP_ROWS = 32
STAGE_ROWS = 128

VMEM_LIMIT_BYTES = 56 * 1024 * 1024

F32 = jnp.float32
BF16 = jnp.bfloat16
_NT_DIMS = (((1,), (1,)), ((), ()))
_TN_DIMS = (((0,), (0,)), ((), ()))


def _rmsnorm(x, g):
    return x * lax.rsqrt(jnp.mean(x * x, axis=-1, keepdims=True) + EPS) * g


def _layer_kernel(xn_ref, xp_ref, gpre_ref, win_hbm, wpool_ref, pscale_ref, wgu_ref, bg_ref,
                  ggla_ref, wout_hbm, gpost_ref, gpre2_ref, wfin_hbm, cw_ref, cb_ref, wfout_hbm,
                  gpost2_ref, o_ref, win_ref, wr_ref, wout_ref, wfin_ref, wfout_ref,
                  h_ref, proj_ref, utail_ref, d_ref, st_ref, y_ref,
                  x1_ref, h2_ref, tail_ref, act_ref, *, tiles_per_seq, n_blocks):
    ts = TILE
    L = GLA_BLOCK
    nblk = ts // L
    half = 2 * L
    i = pl.program_id(0)

    def load_weights(stage_in, stage_rows, stage_up, stage_sem):
        def cast_in_t(dst_ref):
            def cast(c0, chunk):
                dst_ref[:, c0:c0 + W_IN_ROWS] = chunk.T.astype(BF16)
            return cast

        def cast_to(dst_ref):
            def cast(r0, chunk):
                dst_ref[r0:r0 + chunk.shape[0], :] = chunk.astype(BF16)
            return cast

        jobs = []
        for c0 in range(0, W_IN_COLS, W_IN_ROWS):
            jobs.append((win_hbm, c0, c0, stage_in, cast_in_t(win_ref)))
        for c0 in range(0, D_GLA_V, W_IN_ROWS):
            jobs.append((win_hbm, W_IN_R + c0, c0, stage_in, cast_in_t(wr_ref)))
        for hbm_ref, stage, dst_ref in ((wout_hbm, stage_rows, wout_ref), (wfin_hbm, stage_up, wfin_ref),
                                        (wfout_hbm, stage_rows, wfout_ref)):
            chunk_rows = stage.shape[1]
            for r0 in range(0, hbm_ref.shape[0], chunk_rows):
                jobs.append((hbm_ref, r0, r0, stage, cast_to(dst_ref)))

        def copy(k):
            hbm_ref, r0, _, stage, _ = jobs[k]
            slot = k % STAGE_SLOTS
            return pltpu.make_async_copy(hbm_ref.at[pl.ds(r0, stage.shape[1]), :], stage.at[slot],
                                         stage_sem.at[slot])

        for k in range(STAGE_SLOTS):
            copy(k).start()
        for k in range(len(jobs)):
            _, _, d0, stage, cast = jobs[k]
            copy(k).wait()
            cast(d0, stage[k % STAGE_SLOTS])
            if k + STAGE_SLOTS < len(jobs):
                copy(k + STAGE_SLOTS).start()

    def step(sub, do_a, do_b, do_cf):
        t = TILES_PER_STEP * i + sub
        slot_a = sub % 2
        slot_b = 1 - slot_a
        rows_t = slice(sub * ts, (sub + 1) * ts)
        j = lax.rem(jnp.maximum(t - 1, 0), tiles_per_seq)
        jf = lax.rem(jnp.maximum(t - 2, 0), tiles_per_seq)
        pnew = proj_ref.at[slot_a]
        pj = proj_ref.at[slot_b]
        y_new = y_ref.at[slot_b]
        y_old = y_ref.at[slot_a]
        vals = {}

        if do_b:
            @pl.when(j == 0)
            def _():
                utail_ref[...] = jnp.zeros_like(utail_ref)
                st_ref[...] = jnp.zeros_like(st_ref)

        if do_cf:
            @pl.when(jf == 0)
            def _():
                tail_ref[...] = jnp.zeros_like(tail_ref)

        def prenorm():
            h_ref[...] = _rmsnorm(xn_ref[rows_t, :], gpre_ref[...]).astype(BF16)

        def proj_piece(p):
            n0, w = PROJ_PIECES[p]
            pnew[:, n0:n0 + w] = jnp.dot(h_ref[...], win_ref[:, n0:n0 + w],
                                         preferred_element_type=F32)

        def proj_r():
            pnew[:, COL_R:COL_R + D_GLA_V] = jnp.dot(h_ref[...], wr_ref[...],
                                                     preferred_element_type=F32)

        def out_proj():
            vals["mix"] = jnp.dot(y_old[...], wout_ref[...], preferred_element_type=F32)

        def out_finish():
            x1 = xp_ref[rows_t, :] + _rmsnorm(vals["mix"], gpost_ref[...])
            x1_ref[...] = x1
            h2_ref[...] = _rmsnorm(x1, gpre2_ref[...]).astype(BF16)

        def ffn_chunk(c):
            row = lax.broadcasted_iota(jnp.int32, (CONV_TAIL, FFN_CHUNK), 0)
            cols = slice(c * FFN_CHUNK, (c + 1) * FFN_CHUNK)
            vcols = slice(D_FF + c * FFN_CHUNK, D_FF + (c + 1) * FFN_CHUNK)
            gate = jnp.dot(h2_ref[...], wfin_ref[:, cols], preferred_element_type=F32)
            val = jnp.dot(h2_ref[...], wfin_ref[:, vcols], preferred_element_type=F32)
            prev = tail_ref[:, cols]
            tail_ref[:, cols] = gate[ts - CONV_TAIL:ts, :]
            p1 = prev[CONV_TAIL - 1:CONV_TAIL, :]
            p2 = prev[CONV_TAIL - 2:CONV_TAIL - 1, :]
            r1 = pltpu.roll(gate, 1, 0)
            r2 = pltpu.roll(gate, 2, 0)
            h1 = jnp.where(row == 0, p1, r1[:CONV_TAIL])
            h2 = jnp.where(row == 0, p2, jnp.where(row == 1, p1, r2[:CONV_TAIL]))
            g1 = jnp.concatenate([h1, r1[CONV_TAIL:]], axis=0)
            g2 = jnp.concatenate([h2, r2[CONV_TAIL:]], axis=0)
            cw = cw_ref[:, cols]
            gc = g2 * cw[0:1, :] + g1 * cw[1:2, :] + gate * cw[2:3, :] + cb_ref[:, cols]
            act = 0.5 * gc * (1.0 + lax.erf(gc * (1.0 / math.sqrt(2.0)))) * val
            act_ref[:, cols] = act.astype(BF16)

        def ffn_down():
            vals["ff"] = jnp.dot(act_ref[...], wfout_ref[...], preferred_element_type=F32)

        def ffn_finish():
            o_ref[rows_t, :] = x1_ref[...] + _rmsnorm(vals["ff"], gpost2_ref[...])

        def pool_sums():
            row = lax.broadcasted_iota(jnp.int32, (MAX_WINDOW, 1), 0)
            pos = (j * ts + row + 1).astype(F32)
            for g, w in enumerate(POOL_WINDOWS):
                cols = slice(g * POOL_GROUP_DIM, (g + 1) * POOL_GROUP_DIM)
                u_g = pj[:, COL_U + g * POOL_GROUP_DIM:COL_U + (g + 1) * POOL_GROUP_DIM]
                ext = jnp.concatenate([utail_ref[:, cols], u_g], axis=0)
                utail_ref[:, cols] = u_g[ts - MAX_WINDOW:, :]
                acc = ext
                span = 1
                while span < w:
                    acc = acc + pltpu.roll(acc, span, 0)
                    span *= 2
                head = acc[MAX_WINDOW:2 * MAX_WINDOW, :] / jnp.minimum(pos, float(w))
                body = acc[2 * MAX_WINDOW:, :] * (1.0 / w)
                d = jnp.concatenate([head, body], axis=0) - u_g
                d_ref[:, cols] = d.astype(BF16)

        def pool_maps():
            for pair in range(2):
                cols = slice(pair * MXU_N, (pair + 1) * MXU_N)
                yp = jnp.dot(d_ref[:, cols], wpool_ref[pair], preferred_element_type=F32)
                y_new[:, cols] = (yp * pscale_ref[:, cols]).astype(BF16)

        def gate_stage():
            glow = pj[:, COL_G:COL_G + GATE_PAD].astype(BF16)
            logits = jnp.dot(glow, wgu_ref[...], preferred_element_type=F32) + bg_ref[...]
            log_a = ((jnp.minimum(logits, 0.0) - jnp.log(1.0 + jnp.exp(-jnp.abs(logits))))
                     * (1.0 / GATE_TAU))
            la_hi = log_a.astype(BF16)
            vals["la"] = (la_hi, (log_a - la_hi.astype(F32)).astype(BF16))

        def prefix_stage():
            la_hi, la_lo = vals["la"]
            ri = lax.broadcasted_iota(jnp.int32, (half, half), 0)
            ci = lax.broadcasted_iota(jnp.int32, (half, half), 1)
            blk_start = ri - lax.rem(ri, L)
            tri_bd = jnp.where(ci <= ri, jnp.where(ci >= blk_start, 1.0, 0.0), 0.0).astype(BF16)
            bcs = []
            for hb in range(ts // half):
                hrows = slice(hb * half, (hb + 1) * half)
                bc2 = (jnp.dot(tri_bd, la_hi[hrows, :], preferred_element_type=F32)
                       + jnp.dot(tri_bd, la_lo[hrows, :], preferred_element_type=F32))
                bcs += [bc2[0:L, :], bc2[L:half, :]]
            vals["bcs"] = bcs
            vals["st"] = [st_ref[0], st_ref[1]]

        def scores_stage(nb):
            lane_first = lax.broadcasted_iota(jnp.int32, (L, 2 * GLA_DK), 1) < GLA_DK
            ri2 = lax.broadcasted_iota(jnp.int32, (L, 2 * L), 0)
            ci2 = lax.rem(lax.broadcasted_iota(jnp.int32, (L, 2 * L), 1), L)
            st_row_first = lax.broadcasted_iota(jnp.int32, (2 * GLA_DV, 2 * GLA_DK), 0) < GLA_DV
            st_col_first = lax.broadcasted_iota(jnp.int32, (2 * GLA_DV, 2 * GLA_DK), 1) < GLA_DK
            st_mask = st_row_first == st_col_first
            rows = slice(nb * L, (nb + 1) * L)
            bc = vals["bcs"][nb]
            b_last = bc[L - 1:L, :]
            q = pj[rows, COL_Q:COL_Q + D_GLA_K] * (GLA_DK ** -0.5)
            k = pj[rows, COL_K:COL_K + D_GLA_K]
            q_inter = (q * jnp.exp(bc)).astype(BF16)
            k_hat = (k * jnp.exp(b_last - bc)).astype(BF16)

            def ref_rows(group, offset):
                parts = [jnp.broadcast_to(bc[g0 + offset:g0 + offset + 1, :], (group, D_GLA_K))
                         for g0 in range(0, L, group)]
                return parts[0] if len(parts) == 1 else jnp.concatenate(parts, axis=0)

            levels = []
            size = GLA_DIAG
            levels.append((size, ref_rows(size, size // 2 - 1)))
            while size < L:
                levels.append((2 * size, ref_rows(2 * size, size - 1)))
                size *= 2
            a_pairs = [None] * (GLA_HEADS // 2)
            finer = None
            for size, b_ref in levels:
                e_fwd = jnp.exp(bc - b_ref)
                q_l = (q * e_fwd).astype(BF16)
                k_l = k * (1.0 / e_fwd)
                for pair in range(GLA_HEADS // 2):
                    kcols = slice(pair * 2 * GLA_DK, (pair + 1) * 2 * GLA_DK)
                    k_p = k_l[:, kcols]
                    k_pair = jnp.concatenate([jnp.where(lane_first, k_p, 0.0),
                                              jnp.where(lane_first, 0.0, k_p)], axis=0).astype(BF16)
                    s_pair = lax.dot_general(q_l[:, kcols], k_pair, _NT_DIMS,
                                             preferred_element_type=F32)
                    if finer is None:
                        a_pairs[pair] = s_pair
                    else:
                        same_finer = (ri2 // finer) == (ci2 // finer)
                        a_pairs[pair] = jnp.where(same_finer, a_pairs[pair], s_pair)
                finer = size
            causal2 = ri2 >= ci2
            a_pairs = [jnp.where(causal2, a, 0.0).astype(BF16) for a in a_pairs]
            v_blk = pj[rows, COL_V:COL_V + D_GLA_V].astype(BF16)
            kv_t = lax.dot_general(v_blk, k_hat, _TN_DIMS, preferred_element_type=F32)
            decay = jnp.exp(b_last)
            st_in = vals["st"]
            st_out = []
            for pair in range(GLA_HEADS // 2):
                pcols = slice(pair * MXU_N, (pair + 1) * MXU_N)
                kcols = slice(pair * 2 * GLA_DK, (pair + 1) * 2 * GLA_DK)
                st_out.append(st_in[pair] * decay[:, kcols]
                              + jnp.where(st_mask, kv_t[pcols, kcols], 0.0))
            vals["st"] = st_out
            vals[("blk", nb)] = (a_pairs, q_inter, [st_p.astype(BF16) for st_p in st_in])

        def output_stage(nb):
            lane_low = lax.broadcasted_iota(jnp.int32, (L, MXU_N), 1) < GLA_DV
            ggla = ggla_ref[...]
            a_pairs, q_inter, st_b = vals[("blk", nb)]
            rows = slice(nb * L, (nb + 1) * L)
            for pair in range(GLA_HEADS // 2):
                pcols = slice(pair * MXU_N, (pair + 1) * MXU_N)
                kcols = slice(pair * 2 * GLA_DK, (pair + 1) * 2 * GLA_DK)
                v_pair = pj[rows, COL_V + pair * MXU_N:COL_V + (pair + 1) * MXU_N]
                v_bd = jnp.concatenate([jnp.where(lane_low, v_pair, 0.0),
                                        jnp.where(lane_low, 0.0, v_pair)], axis=0).astype(BF16)
                o_inter = lax.dot_general(q_inter[:, kcols], st_b[pair], _NT_DIMS,
                                          preferred_element_type=F32)
                o_pair = jnp.dot(a_pairs[pair], v_bd, preferred_element_type=F32) + o_inter
                for hh in range(2):
                    hcols = slice(hh * GLA_DV, (hh + 1) * GLA_DV)
                    o_h = _rmsnorm(o_pair[:, hcols], ggla)
                    c0 = COL_R + pair * MXU_N + hh * GLA_DV
                    r_h = pj[rows, c0:c0 + GLA_DV]
                    o_h = o_h * (r_h * jax.nn.sigmoid(r_h))
                    y0 = D_POOL + pair * MXU_N + hh * GLA_DV
                    y_new[rows, y0:y0 + GLA_DV] = o_h.astype(BF16)

        if do_b:
            gate_stage()
        if do_cf:
            out_proj()
        if do_a:
            prenorm()
        if do_b:
            pool_sums()
        if do_a:
            proj_piece(0)
        if do_cf:
            out_finish()
        if do_b:
            prefix_stage()
        if do_a:
            proj_piece(1)
        if do_cf:
            ffn_chunk(0)
            ffn_chunk(1)
        if do_b:
            for nb in range(nblk):
                scores_stage(nb)
            st_ref[0] = vals["st"][0]
            st_ref[1] = vals["st"][1]
            pool_maps()
        if do_cf:
            ffn_chunk(2)
            ffn_chunk(3)
        if do_b:
            for nb in range(nblk):
                output_stage(nb)
        if do_cf:
            for c in range(4, D_FF // FFN_CHUNK):
                ffn_chunk(c)
        if do_a:
            proj_piece(2)
        if do_cf:
            ffn_down()
        if do_a:
            proj_r()
        if do_cf:
            ffn_finish()

    @pl.when(i == 0)
    def _():
        pl.run_scoped(load_weights,
                      pltpu.VMEM((STAGE_SLOTS, W_IN_ROWS, D_MODEL), F32),
                      pltpu.VMEM((STAGE_SLOTS, STAGE_ROWS, D_MODEL), F32),
                      pltpu.VMEM((STAGE_SLOTS, W_UP_ROWS, 2 * D_FF), F32),
                      pltpu.SemaphoreType.DMA((STAGE_SLOTS,)))
        step(0, True, False, False)
        step(1, True, True, False)

    @pl.when(jnp.logical_and(i >= 1, i < n_blocks))
    def _():
        step(0, True, True, True)
        step(1, True, True, True)

    @pl.when(i == n_blocks)
    def _():
        step(0, False, True, True)
        step(1, False, False, True)


def _const_spec(shape):
    zeros = (0,) * len(shape)
    return pl.BlockSpec(shape, lambda i: zeros, pipeline_mode=pl.Buffered(1))


def _decoder_layer(x, g_pre, w_in_t, w_pool_p, pool_scale, w_gu_p, b_gate, g_gla, w_out, g_post,
                   g_pre2, w_ffn_in, conv_w, conv_b, w_ffn_out, g_post2):
    B, S, D = x.shape
    ts = TILE
    bs = TILES_PER_STEP * ts
    n_blocks = (B * S) // bs
    x2 = x.reshape(B * S, D)
    kern = functools.partial(_layer_kernel, tiles_per_seq=S // ts, n_blocks=n_blocks)
    cs = _const_spec
    hbm = pl.BlockSpec(memory_space=pl.ANY)
    out = pl.pallas_call(
        kern,
        grid=(n_blocks + 1,),
        in_specs=[
            pl.BlockSpec((bs, D), lambda i: (jnp.minimum(i, n_blocks - 1), 0)),
            pl.BlockSpec((bs, D), lambda i: (jnp.maximum(i - 1, 0), 0)),
            cs((1, D)),
            hbm,
            cs((2, MXU_N, MXU_N)),
            cs((1, D_POOL)),
            cs((GATE_PAD, D_GLA_K)),
            cs((1, D_GLA_K)),
            cs((1, GLA_DV)),
            hbm,
            cs((1, D)),
            cs((1, D)),
            hbm,
            cs((3, D_FF)),
            cs((1, D_FF)),
            hbm,
            cs((1, D)),
        ],
        out_specs=pl.BlockSpec((bs, D), lambda i: (jnp.maximum(i - 1, 0), 0)),
        out_shape=jax.ShapeDtypeStruct((B * S, D), F32),
        scratch_shapes=[
            pltpu.VMEM((D, W_IN_COLS), BF16),
            pltpu.VMEM((D, D_GLA_V), BF16),
            pltpu.VMEM((D, D), BF16),
            pltpu.VMEM((D, 2 * D_FF), BF16),
            pltpu.VMEM((D_FF, D), BF16),
            pltpu.VMEM((ts, D), BF16),
            pltpu.VMEM((2, ts, D_PROJ), F32),
            pltpu.VMEM((MAX_WINDOW, D_POOL), F32),
            pltpu.VMEM((ts, D_POOL), BF16),
            pltpu.VMEM((GLA_HEADS // 2, 2 * GLA_DV, 2 * GLA_DK), F32),
            pltpu.VMEM((2, ts, D), BF16),
            pltpu.VMEM((ts, D), F32),
            pltpu.VMEM((ts, D), BF16),
            pltpu.VMEM((CONV_TAIL, D_FF), F32),
            pltpu.VMEM((ts, D_FF), BF16),
        ],
        compiler_params=pltpu.CompilerParams(
            dimension_semantics=("arbitrary",),
            vmem_limit_bytes=VMEM_LIMIT_BYTES),
        name="decoder_layer",
    )(x2, x2, g_pre, w_in_t, w_pool_p, pool_scale, w_gu_p, b_gate, g_gla, w_out, g_post,
      g_pre2, w_ffn_in, conv_w, conv_b, w_ffn_out, g_post2)
    return out.reshape(B, S, D)


def _pack_w_pool(w_pool):
    z = jnp.zeros((POOL_GROUP_DIM, POOL_GROUP_DIM), w_pool.dtype)
    pairs = [jnp.block([[w_pool[2 * p], z], [z, w_pool[2 * p + 1]]]) for p in range(2)]
    return jnp.stack(pairs).astype(BF16)


def kernel(x, g_pre_mix, w_in, w_pool, pool_scale, w_gate_up, b_gate, g_gla_norm, w_out, g_post_mix,
           g_pre_ffn, w_ffn_in, conv_w, conv_b, w_ffn_out, g_post_ffn):
    row = lambda a: a.reshape(1, -1).astype(F32)
    w_gu_p = jnp.pad(w_gate_up, ((0, GATE_PAD - GATE_RANK), (0, 0))).astype(BF16)
    return _decoder_layer(
        x, row(g_pre_mix), w_in.astype(F32).T, _pack_w_pool(w_pool),
        row(pool_scale), w_gu_p,
        row(b_gate), row(g_gla_norm), w_out.astype(F32), row(g_post_mix), row(g_pre_ffn),
        w_ffn_in.astype(F32), conv_w.astype(F32), row(conv_b), w_ffn_out.astype(F32),
        row(g_post_ffn))
```

```python
import functools
import math

import jax
import jax.numpy as jnp
from jax import lax
from jax.experimental import pallas as pl
from jax.experimental.pallas import tpu as pltpu

D_MODEL = 1024
D_POOL = 512
POOL_WINDOWS = (2, 4, 8, 16)
POOL_GROUP_DIM = 128
MAX_WINDOW = 16
D_GLA_V = 512
GLA_HEADS = 4
GLA_DV = 128
GLA_DK = 64
D_GLA_K = 256
GATE_RANK = 16
GATE_TAU = 16.0
D_FF = 2816
EPS = 1e-6

LANES = 128
MXU_N = 256
GATE_PAD = LANES
COL_U = 0
COL_Q = COL_U + D_POOL
COL_K = COL_Q + D_GLA_K
COL_V = COL_K + D_GLA_K
COL_G = COL_V + D_GLA_V
COL_R = COL_G + GATE_PAD
D_PROJ = COL_R + D_GLA_V
W_IN_R = COL_G + GATE_RANK
PROJ_PIECES = ((COL_U, D_POOL), (COL_Q, 2 * D_GLA_K), (COL_V, D_GLA_V + GATE_PAD))

TILE = 256
TILES_PER_STEP = 2
GLA_BLOCK = 128
GLA_DIAG = 32
FFN_CHUNK = 256
CONV_TAIL = 8

STAGE_SLOTS = 3
W_IN_ROWS = 128
W_IN_COLS = COL_G + GATE_PAD
W_UP_ROWS = 32
STAGE_ROWS = 128

VMEM_LIMIT_BYTES = 56 * 1024 * 1024

F32 = jnp.float32
BF16 = jnp.bfloat16
_NT_DIMS = (((1,), (1,)), ((), ()))
_TN_DIMS = (((0,), (0,)), ((), ()))


def _rmsnorm(x, g):
    return x * lax.rsqrt(jnp.mean(x * x, axis=-1, keepdims=True) + EPS) * g


def _layer_kernel(xn_ref, xp_ref, gpre_ref, win_hbm, wpool_ref, pscale_ref, wgu_ref, bg_ref,
                  ggla_ref, wout_hbm, gpost_ref, gpre2_ref, wfin_hbm, cw_ref, cb_ref, wfout_hbm,
                  gpost2_ref, o_ref, win_ref, wr_ref, wout_ref, wfin_ref, wfout_ref,
                  h_ref, proj_ref, utail_ref, d_ref, st_ref, y_ref,
                  x1_ref, h2_ref, tail_ref, act_ref, *, tiles_per_seq, n_blocks):
    ts = TILE
    L = GLA_BLOCK
    nblk = ts // L
    half = 2 * L
    i = pl.program_id(0)

    def load_weights(stage_in, stage_rows, stage_up, stage_sem):
        def cast_in_t(dst_ref):
            def cast(c0, chunk):
                dst_ref[:, c0:c0 + W_IN_ROWS] = chunk.T.astype(BF16)
            return cast

        def cast_to(dst_ref):
            def cast(r0, chunk):
                dst_ref[r0:r0 + chunk.shape[0], :] = chunk.astype(BF16)
            return cast

        jobs = []
        for c0 in range(0, W_IN_COLS, W_IN_ROWS):
            jobs.append((win_hbm, c0, c0, stage_in, cast_in_t(win_ref)))
        for c0 in range(0, D_GLA_V, W_IN_ROWS):
            jobs.append((win_hbm, W_IN_R + c0, c0, stage_in, cast_in_t(wr_ref)))
        for hbm_ref, stage, dst_ref in ((wout_hbm, stage_rows, wout_ref), (wfin_hbm, stage_up, wfin_ref),
                                        (wfout_hbm, stage_rows, wfout_ref)):
            chunk_rows = stage.shape[1]
            for r0 in range(0, hbm_ref.shape[0], chunk_rows):
                jobs.append((hbm_ref, r0, r0, stage, cast_to(dst_ref)))

        def copy(k):
            hbm_ref, r0, _, stage, _ = jobs[k]
            slot = k % STAGE_SLOTS
            return pltpu.make_async_copy(hbm_ref.at[pl.ds(r0, stage.shape[1]), :], stage.at[slot],
                                         stage_sem.at[slot])

        for k in range(STAGE_SLOTS):
            copy(k).start()
        for k in range(len(jobs)):
            _, _, d0, stage, cast = jobs[k]
            copy(k).wait()
            cast(d0, stage[k % STAGE_SLOTS])
            if k + STAGE_SLOTS < len(jobs):
                copy(k + STAGE_SLOTS).start()

    def step(sub, do_a, do_b, do_cf):
        t = TILES_PER_STEP * i + sub
        slot_a = sub % 2
        slot_b = 1 - slot_a
        rows_t = slice(sub * ts, (sub + 1) * ts)
        j = lax.rem(jnp.maximum(t - 1, 0), tiles_per_seq)
        jf = lax.rem(jnp.maximum(t - 2, 0), tiles_per_seq)
        pnew = proj_ref.at[slot_a]
        pj = proj_ref.at[slot_b]
        y_new = y_ref.at[slot_b]
        y_old = y_ref.at[slot_a]
        vals = {}

        if do_b:
            @pl.when(j == 0)
            def _():
                utail_ref[...] = jnp.zeros_like(utail_ref)
                st_ref[...] = jnp.zeros_like(st_ref)

        if do_cf:
            @pl.when(jf == 0)
            def _():
                tail_ref[...] = jnp.zeros_like(tail_ref)

        def prenorm():
            h_ref[...] = _rmsnorm(xn_ref[rows_t, :], gpre_ref[...]).astype(BF16)

        def proj_piece(p):
            n0, w = PROJ_PIECES[p]
            pnew[:, n0:n0 + w] = jnp.dot(h_ref[...], win_ref[:, n0:n0 + w],
                                         preferred_element_type=F32)

        def proj_r():
            pnew[:, COL_R:COL_R + D_GLA_V] = jnp.dot(h_ref[...], wr_ref[...],
                                                     preferred_element_type=F32)

        def out_proj():
            vals["mix"] = jnp.dot(y_old[...], wout_ref[...], preferred_element_type=F32)

        def out_finish():
            x1 = xp_ref[rows_t, :] + _rmsnorm(vals["mix"], gpost_ref[...])
            x1_ref[...] = x1
            h2_ref[...] = _rmsnorm(x1, gpre2_ref[...]).astype(BF16)

        def ffn_chunk(c):
            row = lax.broadcasted_iota(jnp.int32, (CONV_TAIL, FFN_CHUNK), 0)
            cols = slice(c * FFN_CHUNK, (c + 1) * FFN_CHUNK)
            vcols = slice(D_FF + c * FFN_CHUNK, D_FF + (c + 1) * FFN_CHUNK)
            gate = jnp.dot(h2_ref[...], wfin_ref[:, cols], preferred_element_type=F32)
            val = jnp.dot(h2_ref[...], wfin_ref[:, vcols], preferred_element_type=F32)
            prev = tail_ref[:, cols]
            tail_ref[:, cols] = gate[ts - CONV_TAIL:ts, :]
            p1 = prev[CONV_TAIL - 1:CONV_TAIL, :]
            p2 = prev[CONV_TAIL - 2:CONV_TAIL - 1, :]
            r1 = pltpu.roll(gate, 1, 0)
            r2 = pltpu.roll(gate, 2, 0)
            h1 = jnp.where(row == 0, p1, r1[:CONV_TAIL])
            h2 = jnp.where(row == 0, p2, jnp.where(row == 1, p1, r2[:CONV_TAIL]))
            g1 = jnp.concatenate([h1, r1[CONV_TAIL:]], axis=0)
            g2 = jnp.concatenate([h2, r2[CONV_TAIL:]], axis=0)
            cw = cw_ref[:, cols]
            gc = g2 * cw[0:1, :] + g1 * cw[1:2, :] + gate * cw[2:3, :] + cb_ref[:, cols]
            act = 0.5 * gc * (1.0 + lax.erf(gc * (1.0 / math.sqrt(2.0)))) * val
            act_ref[:, cols] = act.astype(BF16)

        def ffn_down():
            vals["ff"] = jnp.dot(act_ref[...], wfout_ref[...], preferred_element_type=F32)

        def ffn_finish():
            o_ref[rows_t, :] = x1_ref[...] + _rmsnorm(vals["ff"], gpost2_ref[...])

        def pool_sums():
            row = lax.broadcasted_iota(jnp.int32, (MAX_WINDOW, 1), 0)
            pos = (j * ts + row + 1).astype(F32)
            for g, w in enumerate(POOL_WINDOWS):
                cols = slice(g * POOL_GROUP_DIM, (g + 1) * POOL_GROUP_DIM)
                u_g = pj[:, COL_U + g * POOL_GROUP_DIM:COL_U + (g + 1) * POOL_GROUP_DIM]
                ext = jnp.concatenate([utail_ref[:, cols], u_g], axis=0)
                utail_ref[:, cols] = u_g[ts - MAX_WINDOW:, :]
                acc = ext
                span = 1
                while span < w:
                    acc = acc + pltpu.roll(acc, span, 0)
                    span *= 2
                head = acc[MAX_WINDOW:2 * MAX_WINDOW, :] / jnp.minimum(pos, float(w))
                body = acc[2 * MAX_WINDOW:, :] * (1.0 / w)
                d = jnp.concatenate([head, body], axis=0) - u_g
                d_ref[:, cols] = d.astype(BF16)

        def pool_maps():
            for pair in range(2):
                cols = slice(pair * MXU_N, (pair + 1) * MXU_N)
                yp = jnp.dot(d_ref[:, cols], wpool_ref[pair], preferred_element_type=F32)
                y_new[:, cols] = (yp * pscale_ref[:, cols]).astype(BF16)

        def gate_stage():
            glow = pj[:, COL_G:COL_G + GATE_PAD].astype(BF16)
            logits = jnp.dot(glow, wgu_ref[...], preferred_element_type=F32) + bg_ref[...]
            log_a = ((jnp.minimum(logits, 0.0) - jnp.log(1.0 + jnp.exp(-jnp.abs(logits))))
                     * (1.0 / GATE_TAU))
            la_hi = log_a.astype(BF16)
            vals["la"] = (la_hi, (log_a - la_hi.astype(F32)).astype(BF16))

        def prefix_stage():
            la_hi, la_lo = vals["la"]
            ri = lax.broadcasted_iota(jnp.int32, (half, half), 0)
            ci = lax.broadcasted_iota(jnp.int32, (half, half), 1)
            blk_start = ri - lax.rem(ri, L)
            tri_bd = jnp.where(ci <= ri, jnp.where(ci >= blk_start, 1.0, 0.0), 0.0).astype(BF16)
            bcs = []
            for hb in range(ts // half):
                hrows = slice(hb * half, (hb + 1) * half)
                bc2 = (jnp.dot(tri_bd, la_hi[hrows, :], preferred_element_type=F32)
                       + jnp.dot(tri_bd, la_lo[hrows, :], preferred_element_type=F32))
                bcs += [bc2[0:L, :], bc2[L:half, :]]
            vals["bcs"] = bcs
            vals["st"] = [st_ref[0], st_ref[1]]

        def scores_stage(nb):
            lane_first = lax.broadcasted_iota(jnp.int32, (L, 2 * GLA_DK), 1) < GLA_DK
            ri2 = lax.broadcasted_iota(jnp.int32, (L, 2 * L), 0)
            ci2 = lax.rem(lax.broadcasted_iota(jnp.int32, (L, 2 * L), 1), L)
            st_row_first = lax.broadcasted_iota(jnp.int32, (2 * GLA_DV, 2 * GLA_DK), 0) < GLA_DV
            st_col_first = lax.broadcasted_iota(jnp.int32, (2 * GLA_DV, 2 * GLA_DK), 1) < GLA_DK
            st_mask = st_row_first == st_col_first
            rows = slice(nb * L, (nb + 1) * L)
            bc = vals["bcs"][nb]
            b_last = bc[L - 1:L, :]
            q = pj[rows, COL_Q:COL_Q + D_GLA_K] * (GLA_DK ** -0.5)
            k = pj[rows, COL_K:COL_K + D_GLA_K]
            q_inter = (q * jnp.exp(bc)).astype(BF16)
            k_hat = (k * jnp.exp(b_last - bc)).astype(BF16)

            def ref_rows(group, offset):
                parts = [jnp.broadcast_to(bc[g0 + offset:g0 + offset + 1, :], (group, D_GLA_K))
                         for g0 in range(0, L, group)]
                return parts[0] if len(parts) == 1 else jnp.concatenate(parts, axis=0)

            levels = []
            size = GLA_DIAG
            levels.append((size, ref_rows(size, size // 2 - 1)))
            while size < L:
                levels.append((2 * size, ref_rows(2 * size, size - 1)))
                size *= 2
            a_pairs = [None] * (GLA_HEADS // 2)
            finer = None
            for size, b_ref in levels:
                e_fwd = jnp.exp(bc - b_ref)
                q_l = (q * e_fwd).astype(BF16)
                k_l = k * (1.0 / e_fwd)
                for pair in range(GLA_HEADS // 2):
                    kcols = slice(pair * 2 * GLA_DK, (pair + 1) * 2 * GLA_DK)
                    k_p = k_l[:, kcols]
                    k_pair = jnp.concatenate([jnp.where(lane_first, k_p, 0.0),
                                              jnp.where(lane_first, 0.0, k_p)], axis=0).astype(BF16)
                    s_pair = lax.dot_general(q_l[:, kcols], k_pair, _NT_DIMS,
                                             preferred_element_type=F32)
                    if finer is None:
                        a_pairs[pair] = s_pair
                    else:
                        same_finer = (ri2 // finer) == (ci2 // finer)
                        a_pairs[pair] = jnp.where(same_finer, a_pairs[pair], s_pair)
                finer = size
            causal2 = ri2 >= ci2
            a_pairs = [jnp.where(causal2, a, 0.0).astype(BF16) for a in a_pairs]
            v_blk = pj[rows, COL_V:COL_V + D_GLA_V].astype(BF16)
            kv_t = lax.dot_general(v_blk, k_hat, _TN_DIMS, preferred_element_type=F32)
            decay = jnp.exp(b_last)
            st_in = vals["st"]
            st_out = []
            for pair in range(GLA_HEADS // 2):
                pcols = slice(pair * MXU_N, (pair + 1) * MXU_N)
                kcols = slice(pair * 2 * GLA_DK, (pair + 1) * 2 * GLA_DK)
                st_out.append(st_in[pair] * decay[:, kcols]
                              + jnp.where(st_mask, kv_t[pcols, kcols], 0.0))
            vals["st"] = st_out
            vals[("blk", nb)] = (a_pairs, q_inter, [st_p.astype(BF16) for st_p in st_in])

        def output_stage(nb):
            lane_low = lax.broadcasted_iota(jnp.int32, (L, MXU_N), 1) < GLA_DV
            ggla = ggla_ref[...]
            a_pairs, q_inter, st_b = vals[("blk", nb)]
            rows = slice(nb * L, (nb + 1) * L)
            for pair in range(GLA_HEADS // 2):
                pcols = slice(pair * MXU_N, (pair + 1) * MXU_N)
                kcols = slice(pair * 2 * GLA_DK, (pair + 1) * 2 * GLA_DK)
                v_pair = pj[rows, COL_V + pair * MXU_N:COL_V + (pair + 1) * MXU_N]
                v_bd = jnp.concatenate([jnp.where(lane_low, v_pair, 0.0),
                                        jnp.where(lane_low, 0.0, v_pair)], axis=0).astype(BF16)
                o_inter = lax.dot_general(q_inter[:, kcols], st_b[pair], _NT_DIMS,
                                          preferred_element_type=F32)
                o_pair = jnp.dot(a_pairs[pair], v_bd, preferred_element_type=F32) + o_inter
                for hh in range(2):
                    hcols = slice(hh * GLA_DV, (hh + 1) * GLA_DV)
                    o_h = _rmsnorm(o_pair[:, hcols], ggla)
                    c0 = COL_R + pair * MXU_N + hh * GLA_DV
                    r_h = pj[rows, c0:c0 + GLA_DV]
                    o_h = o_h * (r_h * jax.nn.sigmoid(r_h))
                    y0 = D_POOL + pair * MXU_N + hh * GLA_DV
                    y_new[rows, y0:y0 + GLA_DV] = o_h.astype(BF16)

        if do_b:
            gate_stage()
        if do_cf:
            out_proj()
        if do_a:
            prenorm()
        if do_b:
            pool_sums()
        if do_a:
            proj_piece(0)
        if do_cf:
            out_finish()
        if do_b:
            prefix_stage()
        if do_a:
            proj_piece(1)
        if do_cf:
            ffn_chunk(0)
        if do_b:
            pool_maps()
            for nb in range(nblk):
                scores_stage(nb)
            st_ref[0] = vals["st"][0]
            st_ref[1] = vals["st"][1]
        if do_cf:
            ffn_chunk(1)
            ffn_chunk(2)
            ffn_chunk(3)
        if do_b:
            for nb in range(nblk):
                output_stage(nb)
        if do_cf:
            for c in range(4, D_FF // FFN_CHUNK):
                ffn_chunk(c)
        if do_a:
            proj_piece(2)
        if do_cf:
            ffn_down()
        if do_a:
            proj_r()
        if do_cf:
            ffn_finish()

    @pl.when(i == 0)
    def _():
        pl.run_scoped(load_weights,
                      pltpu.VMEM((STAGE_SLOTS, W_IN_ROWS, D_MODEL), F32),
                      pltpu.VMEM((STAGE_SLOTS, STAGE_ROWS, D_MODEL), F32),
                      pltpu.VMEM((STAGE_SLOTS, W_UP_ROWS, 2 * D_FF), F32),
                      pltpu.SemaphoreType.DMA((STAGE_SLOTS,)))
        step(0, True, False, False)
        step(1, True, True, False)

    @pl.when(jnp.logical_and(i >= 1, i < n_blocks))
    def _():
        step(0, True, True, True)
        step(1, True, True, True)

    @pl.when(i == n_blocks)
    def _():
        step(0, False, True, True)
        step(1, False, False, True)


def _const_spec(shape):
    zeros = (0,) * len(shape)
    return pl.BlockSpec(shape, lambda i: zeros, pipeline_mode=pl.Buffered(1))


def _decoder_layer(x, g_pre, w_in_t, w_pool_p, pool_scale, w_gu_p, b_gate, g_gla, w_out, g_post,
                   g_pre2, w_ffn_in, conv_w, conv_b, w_ffn_out, g_post2):
    B, S, D = x.shape
    ts = TILE
    bs = TILES_PER_STEP * ts
    n_blocks = (B * S) // bs
    x2 = x.reshape(B * S, D)
    kern = functools.partial(_layer_kernel, tiles_per_seq=S // ts, n_blocks=n_blocks)
    cs = _const_spec
    hbm = pl.BlockSpec(memory_space=pl.ANY)
    out = pl.pallas_call(
        kern,
        grid=(n_blocks + 1,),
        in_specs=[
            pl.BlockSpec((bs, D), lambda i: (jnp.minimum(i, n_blocks - 1), 0)),
            pl.BlockSpec((bs, D), lambda i: (jnp.maximum(i - 1, 0), 0)),
            cs((1, D)),
            hbm,
            cs((2, MXU_N, MXU_N)),
            cs((1, D_POOL)),
            cs((GATE_PAD, D_GLA_K)),
            cs((1, D_GLA_K)),
            cs((1, GLA_DV)),
            hbm,
            cs((1, D)),
            cs((1, D)),
            hbm,
            cs((3, D_FF)),
            cs((1, D_FF)),
            hbm,
            cs((1, D)),
        ],
        out_specs=pl.BlockSpec((bs, D), lambda i: (jnp.maximum(i - 1, 0), 0)),
        out_shape=jax.ShapeDtypeStruct((B * S, D), F32),
        scratch_shapes=[
            pltpu.VMEM((D, W_IN_COLS), BF16),
            pltpu.VMEM((D, D_GLA_V), BF16),
            pltpu.VMEM((D, D), BF16),
            pltpu.VMEM((D, 2 * D_FF), BF16),
            pltpu.VMEM((D_FF, D), BF16),
            pltpu.VMEM((ts, D), BF16),
            pltpu.VMEM((2, ts, D_PROJ), F32),
            pltpu.VMEM((MAX_WINDOW, D_POOL), F32),
            pltpu.VMEM((ts, D_POOL), BF16),
            pltpu.VMEM((GLA_HEADS // 2, 2 * GLA_DV, 2 * GLA_DK), F32),
            pltpu.VMEM((2, ts, D), BF16),
            pltpu.VMEM((ts, D), F32),
            pltpu.VMEM((ts, D), BF16),
            pltpu.VMEM((CONV_TAIL, D_FF), F32),
            pltpu.VMEM((ts, D_FF), BF16),
        ],
        compiler_params=pltpu.CompilerParams(
            dimension_semantics=("arbitrary",),
            vmem_limit_bytes=VMEM_LIMIT_BYTES),
        name="decoder_layer",
    )(x2, x2, g_pre, w_in_t, w_pool_p, pool_scale, w_gu_p, b_gate, g_gla, w_out, g_post,
      g_pre2, w_ffn_in, conv_w, conv_b, w_ffn_out, g_post2)
    return out.reshape(B, S, D)


def _pack_w_pool(w_pool):
    z = jnp.zeros((POOL_GROUP_DIM, POOL_GROUP_DIM), w_pool.dtype)
    pairs = [jnp.block([[w_pool[2 * p], z], [z, w_pool[2 * p + 1]]]) for p in range(2)]
    return jnp.stack(pairs).astype(BF16)


def kernel(x, g_pre_mix, w_in, w_pool, pool_scale, w_gate_up, b_gate, g_gla_norm, w_out, g_post_mix,
           g_pre_ffn, w_ffn_in, conv_w, conv_b, w_ffn_out, g_post_ffn):
    row = lambda a: a.reshape(1, -1).astype(F32)
    w_gu_p = jnp.pad(w_gate_up, ((0, GATE_PAD - GATE_RANK), (0, 0))).astype(BF16)
    return _decoder_layer(
        x, row(g_pre_mix), w_in.astype(F32).T, _pack_w_pool(w_pool),
        row(pool_scale), w_gu_p,
        row(b_gate), row(g_gla_norm), w_out.astype(F32), row(g_post_mix), row(g_pre_ffn),
        w_ffn_in.astype(F32), conv_w.astype(F32), row(conv_b), w_ffn_out.astype(F32),
        row(g_post_ffn))
```

```python
import functools
import math

import jax
import jax.numpy as jnp
from jax import lax
from jax.experimental import pallas as pl
from jax.experimental.pallas import tpu as pltpu

D_MODEL = 1024
D_POOL = 512
POOL_WINDOWS = (2, 4, 8, 16)
POOL_GROUP_DIM = 128
MAX_WINDOW = 16
D_GLA_V = 512
GLA_HEADS = 4
GLA_DV = 128
GLA_DK = 64
D_GLA_K = 256
GATE_RANK = 16
GATE_TAU = 16.0
D_FF = 2816
EPS = 1e-6

LANES = 128
MXU_N = 256
GATE_PAD = LANES
COL_U = 0
COL_Q = COL_U + D_POOL
COL_K = COL_Q + D_GLA_K
COL_V = COL_K + D_GLA_K
COL_G = COL_V + D_GLA_V
COL_R = COL_G + GATE_PAD
D_PROJ = COL_R + D_GLA_V
W_IN_R = COL_G + GATE_RANK
PROJ_PIECES = ((COL_U, D_POOL), (COL_Q, 2 * D_GLA_K), (COL_V, D_GLA_V + GATE_PAD))

TILE = 256
TILES_PER_STEP = 2
GLA_BLOCK = 128
GLA_DIAG = 32
FFN_CHUNK = 256
CONV_TAIL = 8

STAGE_SLOTS = 3
W_IN_ROWS = 128
W_IN_COLS = COL_G + GATE_PAD
W_UP_ROWS = 32
STAGE_ROWS = 128

VMEM_LIMIT_BYTES = 56 * 1024 * 1024

F32 = jnp.float32
BF16 = jnp.bfloat16
_NT_DIMS = (((1,), (1,)), ((), ()))
_TN_DIMS = (((0,), (0,)), ((), ()))


def _rmsnorm(x, g):
    return x * lax.rsqrt(jnp.mean(x * x, axis=-1, keepdims=True) + EPS) * g


def _layer_kernel(xn_ref, xp_ref, gpre_ref, win_hbm, wpool_f32, pscale_ref, wgu_f32, bg_ref,
                  ggla_ref, wout_hbm, gpost_ref, gpre2_ref, wfin_hbm, cw_ref, cb_ref, wfout_hbm,
                  gpost2_ref, o_ref, win_ref, wr_ref, wout_ref, wfin_ref, wfout_ref, wpool_ref, wgu_ref,
                  h_ref, proj_ref, utail_ref, d_ref, st_ref, y_ref,
                  x1_ref, h2_ref, tail_ref, act_ref, *, tiles_per_seq, n_blocks):
    ts = TILE
    L = GLA_BLOCK
    nblk = ts // L
    half = 2 * L
    i = pl.program_id(0)

    def load_weights(stage_in, stage_rows, stage_up, stage_sem):
        def cast_in_t(dst_ref):
            def cast(c0, chunk):
                dst_ref[:, c0:c0 + W_IN_ROWS] = chunk.T.astype(BF16)
            return cast

        def cast_to(dst_ref):
            def cast(r0, chunk):
                dst_ref[r0:r0 + chunk.shape[0], :] = chunk.astype(BF16)
            return cast

        jobs = []
        for c0 in range(0, W_IN_COLS, W_IN_ROWS):
            jobs.append((win_hbm, c0, c0, stage_in, cast_in_t(win_ref)))
        for c0 in range(0, D_GLA_V, W_IN_ROWS):
            jobs.append((win_hbm, W_IN_R + c0, c0, stage_in, cast_in_t(wr_ref)))
        for hbm_ref, stage, dst_ref in ((wout_hbm, stage_rows, wout_ref), (wfin_hbm, stage_up, wfin_ref),
                                        (wfout_hbm, stage_rows, wfout_ref)):
            chunk_rows = stage.shape[1]
            for r0 in range(0, hbm_ref.shape[0], chunk_rows):
                jobs.append((hbm_ref, r0, r0, stage, cast_to(dst_ref)))

        def copy(k):
            hbm_ref, r0, _, stage, _ = jobs[k]
            slot = k % STAGE_SLOTS
            return pltpu.make_async_copy(hbm_ref.at[pl.ds(r0, stage.shape[1]), :], stage.at[slot],
                                         stage_sem.at[slot])

        for k in range(STAGE_SLOTS):
            copy(k).start()
        for k in range(len(jobs)):
            _, _, d0, stage, cast = jobs[k]
            copy(k).wait()
            cast(d0, stage[k % STAGE_SLOTS])
            if k + STAGE_SLOTS < len(jobs):
                copy(k + STAGE_SLOTS).start()

    def pack_small_weights():
        g = POOL_GROUP_DIM
        wpool_ref[...] = jnp.zeros_like(wpool_ref)
        for grp in range(len(POOL_WINDOWS)):
            o = (grp % 2) * g
            wpool_ref[grp // 2, o:o + g, o:o + g] = wpool_f32[grp].astype(BF16)
        wgu_ref[...] = jnp.zeros_like(wgu_ref)
        wgu_ref[0:GATE_RANK, :] = wgu_f32[...].astype(BF16)

    def step(sub, do_a, do_b, do_cf):
        t = TILES_PER_STEP * i + sub
        slot_a = sub % 2
        slot_b = 1 - slot_a
        rows_t = slice(sub * ts, (sub + 1) * ts)
        j = lax.rem(jnp.maximum(t - 1, 0), tiles_per_seq)
        jf = lax.rem(jnp.maximum(t - 2, 0), tiles_per_seq)
        pnew = proj_ref.at[slot_a]
        pj = proj_ref.at[slot_b]
        y_new = y_ref.at[slot_b]
        y_old = y_ref.at[slot_a]
        vals = {}

        if do_b:
            @pl.when(j == 0)
            def _():
                utail_ref[...] = jnp.zeros_like(utail_ref)
                st_ref[...] = jnp.zeros_like(st_ref)

        if do_cf:
            @pl.when(jf == 0)
            def _():
                tail_ref[...] = jnp.zeros_like(tail_ref)

        def prenorm():
            h_ref[...] = _rmsnorm(xn_ref[rows_t, :], gpre_ref[...]).astype(BF16)

        def proj_piece(p):
            n0, w = PROJ_PIECES[p]
            pnew[:, n0:n0 + w] = jnp.dot(h_ref[...], win_ref[:, n0:n0 + w],
                                         preferred_element_type=F32)

        def proj_r():
            pnew[:, COL_R:COL_R + D_GLA_V] = jnp.dot(h_ref[...], wr_ref[...],
                                                     preferred_element_type=F32)

        def out_proj():
            vals["mix"] = jnp.dot(y_old[...], wout_ref[...], preferred_element_type=F32)

        def out_finish():
            x1 = xp_ref[rows_t, :] + _rmsnorm(vals["mix"], gpost_ref[...])
            x1_ref[...] = x1
            h2_ref[...] = _rmsnorm(x1, gpre2_ref[...]).astype(BF16)

        def ffn_chunk(c):
            row = lax.broadcasted_iota(jnp.int32, (CONV_TAIL, FFN_CHUNK), 0)
            cols = slice(c * FFN_CHUNK, (c + 1) * FFN_CHUNK)
            vcols = slice(D_FF + c * FFN_CHUNK, D_FF + (c + 1) * FFN_CHUNK)
            gate = jnp.dot(h2_ref[...], wfin_ref[:, cols], preferred_element_type=F32)
            val = jnp.dot(h2_ref[...], wfin_ref[:, vcols], preferred_element_type=F32)
            prev = tail_ref[:, cols]
            tail_ref[:, cols] = gate[ts - CONV_TAIL:ts, :]
            p1 = prev[CONV_TAIL - 1:CONV_TAIL, :]
            p2 = prev[CONV_TAIL - 2:CONV_TAIL - 1, :]
            r1 = pltpu.roll(gate, 1, 0)
            r2 = pltpu.roll(gate, 2, 0)
            h1 = jnp.where(row == 0, p1, r1[:CONV_TAIL])
            h2 = jnp.where(row == 0, p2, jnp.where(row == 1, p1, r2[:CONV_TAIL]))
            g1 = jnp.concatenate([h1, r1[CONV_TAIL:]], axis=0)
            g2 = jnp.concatenate([h2, r2[CONV_TAIL:]], axis=0)
            cw = cw_ref[:, cols]
            gc = g2 * cw[0:1, :] + g1 * cw[1:2, :] + gate * cw[2:3, :] + cb_ref[:, cols]
            act = 0.5 * gc * (1.0 + lax.erf(gc * (1.0 / math.sqrt(2.0)))) * val
            act_ref[:, cols] = act.astype(BF16)

        def ffn_down():
            vals["ff"] = jnp.dot(act_ref[...], wfout_ref[...], preferred_element_type=F32)

        def ffn_finish():
            o_ref[rows_t, :] = x1_ref[...] + _rmsnorm(vals["ff"], gpost2_ref[...])

        def pool_sums():
            row = lax.broadcasted_iota(jnp.int32, (MAX_WINDOW, 1), 0)
            pos = (j * ts + row + 1).astype(F32)
            for g, w in enumerate(POOL_WINDOWS):
                cols = slice(g * POOL_GROUP_DIM, (g + 1) * POOL_GROUP_DIM)
                u_g = pj[:, COL_U + g * POOL_GROUP_DIM:COL_U + (g + 1) * POOL_GROUP_DIM]
                ext = jnp.concatenate([utail_ref[:, cols], u_g], axis=0)
                utail_ref[:, cols] = u_g[ts - MAX_WINDOW:, :]
                acc = ext
                span = 1
                while span < w:
                    acc = acc + pltpu.roll(acc, span, 0)
                    span *= 2
                head = acc[MAX_WINDOW:2 * MAX_WINDOW, :] / jnp.minimum(pos, float(w))
                body = acc[2 * MAX_WINDOW:, :] * (1.0 / w)
                d = jnp.concatenate([head, body], axis=0) - u_g
                d_ref[:, cols] = d.astype(BF16)

        def pool_maps():
            for pair in range(2):
                cols = slice(pair * MXU_N, (pair + 1) * MXU_N)
                yp = jnp.dot(d_ref[:, cols], wpool_ref[pair], preferred_element_type=F32)
                y_new[:, cols] = (yp * pscale_ref[:, cols]).astype(BF16)

        def gate_stage():
            glow = pj[:, COL_G:COL_G + GATE_PAD].astype(BF16)
            logits = jnp.dot(glow, wgu_ref[...], preferred_element_type=F32) + bg_ref[...]
            log_a = ((jnp.minimum(logits, 0.0) - jnp.log(1.0 + jnp.exp(-jnp.abs(logits))))
                     * (1.0 / GATE_TAU))
            la_hi = log_a.astype(BF16)
            vals["la"] = (la_hi, (log_a - la_hi.astype(F32)).astype(BF16))

        def prefix_stage():
            la_hi, la_lo = vals["la"]
            ri = lax.broadcasted_iota(jnp.int32, (half, half), 0)
            ci = lax.broadcasted_iota(jnp.int32, (half, half), 1)
            blk_start = ri - lax.rem(ri, L)
            tri_bd = jnp.where(ci <= ri, jnp.where(ci >= blk_start, 1.0, 0.0), 0.0).astype(BF16)
            bcs = []
            for hb in range(ts // half):
                hrows = slice(hb * half, (hb + 1) * half)
                bc2 = (jnp.dot(tri_bd, la_hi[hrows, :], preferred_element_type=F32)
                       + jnp.dot(tri_bd, la_lo[hrows, :], preferred_element_type=F32))
                bcs += [bc2[0:L, :], bc2[L:half, :]]
            vals["bcs"] = bcs
            vals["st"] = [st_ref[0], st_ref[1]]

        def scores_stage(nb):
            lane_first = lax.broadcasted_iota(jnp.int32, (L, 2 * GLA_DK), 1) < GLA_DK
            ri2 = lax.broadcasted_iota(jnp.int32, (L, 2 * L), 0)
            ci2 = lax.rem(lax.broadcasted_iota(jnp.int32, (L, 2 * L), 1), L)
            st_row_first = lax.broadcasted_iota(jnp.int32, (2 * GLA_DV, 2 * GLA_DK), 0) < GLA_DV
            st_col_first = lax.broadcasted_iota(jnp.int32, (2 * GLA_DV, 2 * GLA_DK), 1) < GLA_DK
            st_mask = st_row_first == st_col_first
            rows = slice(nb * L, (nb + 1) * L)
            bc = vals["bcs"][nb]
            b_last = bc[L - 1:L, :]
            q = pj[rows, COL_Q:COL_Q + D_GLA_K] * (GLA_DK ** -0.5)
            k = pj[rows, COL_K:COL_K + D_GLA_K]
            q_inter = (q * jnp.exp(bc)).astype(BF16)
            k_hat = (k * jnp.exp(b_last - bc)).astype(BF16)

            def ref_rows(group, offset):
                parts = [jnp.broadcast_to(bc[g0 + offset:g0 + offset + 1, :], (group, D_GLA_K))
                         for g0 in range(0, L, group)]
                return parts[0] if len(parts) == 1 else jnp.concatenate(parts, axis=0)

            levels = []
            size = GLA_DIAG
            levels.append((size, ref_rows(size, size // 2 - 1)))
            while size < L:
                levels.append((2 * size, ref_rows(2 * size, size - 1)))
                size *= 2
            a_pairs = [None] * (GLA_HEADS // 2)
            finer = None
            for size, b_ref in levels:
                e_fwd = jnp.exp(bc - b_ref)
                q_l = (q * e_fwd).astype(BF16)
                k_l = k * (1.0 / e_fwd)
                for pair in range(GLA_HEADS // 2):
                    kcols = slice(pair * 2 * GLA_DK, (pair + 1) * 2 * GLA_DK)
                    k_p = k_l[:, kcols]
                    k_pair = jnp.concatenate([jnp.where(lane_first, k_p, 0.0),
                                              jnp.where(lane_first, 0.0, k_p)], axis=0).astype(BF16)
                    s_pair = lax.dot_general(q_l[:, kcols], k_pair, _NT_DIMS,
                                             preferred_element_type=F32)
                    if finer is None:
                        a_pairs[pair] = s_pair
                    else:
                        same_finer = (ri2 // finer) == (ci2 // finer)
                        a_pairs[pair] = jnp.where(same_finer, a_pairs[pair], s_pair)
                finer = size
            causal2 = ri2 >= ci2
            a_pairs = [jnp.where(causal2, a, 0.0).astype(BF16) for a in a_pairs]
            v_blk = pj[rows, COL_V:COL_V + D_GLA_V].astype(BF16)
            kv_t = lax.dot_general(v_blk, k_hat, _TN_DIMS, preferred_element_type=F32)
            decay = jnp.exp(b_last)
            st_in = vals["st"]
            st_out = []
            for pair in range(GLA_HEADS // 2):
                pcols = slice(pair * MXU_N, (pair + 1) * MXU_N)
                kcols = slice(pair * 2 * GLA_DK, (pair + 1) * 2 * GLA_DK)
                st_out.append(st_in[pair] * decay[:, kcols]
                              + jnp.where(st_mask, kv_t[pcols, kcols], 0.0))
            vals["st"] = st_out
            vals[("blk", nb)] = (a_pairs, q_inter, [st_p.astype(BF16) for st_p in st_in])

        def output_stage(nb):
            lane_low = lax.broadcasted_iota(jnp.int32, (L, MXU_N), 1) < GLA_DV
            ggla = ggla_ref[...]
            a_pairs, q_inter, st_b = vals[("blk", nb)]
            rows = slice(nb * L, (nb + 1) * L)
            for pair in range(GLA_HEADS // 2):
                pcols = slice(pair * MXU_N, (pair + 1) * MXU_N)
                kcols = slice(pair * 2 * GLA_DK, (pair + 1) * 2 * GLA_DK)
                v_pair = pj[rows, COL_V + pair * MXU_N:COL_V + (pair + 1) * MXU_N]
                v_bd = jnp.concatenate([jnp.where(lane_low, v_pair, 0.0),
                                        jnp.where(lane_low, 0.0, v_pair)], axis=0).astype(BF16)
                o_inter = lax.dot_general(q_inter[:, kcols], st_b[pair], _NT_DIMS,
                                          preferred_element_type=F32)
                o_pair = jnp.dot(a_pairs[pair], v_bd, preferred_element_type=F32) + o_inter
                for hh in range(2):
                    hcols = slice(hh * GLA_DV, (hh + 1) * GLA_DV)
                    o_h = _rmsnorm(o_pair[:, hcols], ggla)
                    c0 = COL_R + pair * MXU_N + hh * GLA_DV
                    r_h = pj[rows, c0:c0 + GLA_DV]
                    o_h = o_h * (r_h * jax.nn.sigmoid(r_h))
                    y0 = D_POOL + pair * MXU_N + hh * GLA_DV
                    y_new[rows, y0:y0 + GLA_DV] = o_h.astype(BF16)

        if do_b:
            gate_stage()
        if do_cf:
            out_proj()
        if do_a:
            prenorm()
        if do_b:
            pool_sums()
        if do_a:
            proj_piece(0)
        if do_cf:
            out_finish()
        if do_b:
            prefix_stage()
        if do_a:
            proj_piece(1)
        if do_cf:
            ffn_chunk(0)
        if do_b:
            pool_maps()
            for nb in range(nblk):
                scores_stage(nb)
            st_ref[0] = vals["st"][0]
            st_ref[1] = vals["st"][1]
        if do_cf:
            ffn_chunk(1)
            ffn_chunk(2)
            ffn_chunk(3)
        if do_b:
            for nb in range(nblk):
                output_stage(nb)
        if do_cf:
            for c in range(4, D_FF // FFN_CHUNK):
                ffn_chunk(c)
        if do_a:
            proj_piece(2)
        if do_cf:
            ffn_down()
        if do_a:
            proj_r()
        if do_cf:
            ffn_finish()

    @pl.when(i == 0)
    def _():
        pl.run_scoped(load_weights,
                      pltpu.VMEM((STAGE_SLOTS, W_IN_ROWS, D_MODEL), F32),
                      pltpu.VMEM((STAGE_SLOTS, STAGE_ROWS, D_MODEL), F32),
                      pltpu.VMEM((STAGE_SLOTS, W_UP_ROWS, 2 * D_FF), F32),
                      pltpu.SemaphoreType.DMA((STAGE_SLOTS,)))
        pack_small_weights()
        step(0, True, False, False)
        step(1, True, True, False)

    @pl.when(jnp.logical_and(i >= 1, i < n_blocks))
    def _():
        step(0, True, True, True)
        step(1, True, True, True)

    @pl.when(i == n_blocks)
    def _():
        step(0, False, True, True)
        step(1, False, False, True)


def _const_spec(shape):
    zeros = (0,) * len(shape)
    return pl.BlockSpec(shape, lambda i: zeros, pipeline_mode=pl.Buffered(1))


def _decoder_layer(x, g_pre, w_in_t, w_pool, pool_scale, w_gu, b_gate, g_gla, w_out, g_post,
                   g_pre2, w_ffn_in, conv_w, conv_b, w_ffn_out, g_post2):
    B, S, D = x.shape
    ts = TILE
    bs = TILES_PER_STEP * ts
    n_blocks = (B * S) // bs
    x2 = x.reshape(B * S, D)
    kern = functools.partial(_layer_kernel, tiles_per_seq=S // ts, n_blocks=n_blocks)
    cs = _const_spec
    hbm = pl.BlockSpec(memory_space=pl.ANY)
    out = pl.pallas_call(
        kern,
        grid=(n_blocks + 1,),
        in_specs=[
            pl.BlockSpec((bs, D), lambda i: (jnp.minimum(i, n_blocks - 1), 0)),
            pl.BlockSpec((bs, D), lambda i: (jnp.maximum(i - 1, 0), 0)),
            cs((1, D)),
            hbm,
            cs((len(POOL_WINDOWS), POOL_GROUP_DIM, POOL_GROUP_DIM)),
            cs((1, D_POOL)),
            cs((GATE_RANK, D_GLA_K)),
            cs((1, D_GLA_K)),
            cs((1, GLA_DV)),
            hbm,
            cs((1, D)),
            cs((1, D)),
            hbm,
            cs((3, D_FF)),
            cs((1, D_FF)),
            hbm,
            cs((1, D)),
        ],
        out_specs=pl.BlockSpec((bs, D), lambda i: (jnp.maximum(i - 1, 0), 0)),
        out_shape=jax.ShapeDtypeStruct((B * S, D), F32),
        scratch_shapes=[
            pltpu.VMEM((D, W_IN_COLS), BF16),
            pltpu.VMEM((D, D_GLA_V), BF16),
            pltpu.VMEM((D, D), BF16),
            pltpu.VMEM((D, 2 * D_FF), BF16),
            pltpu.VMEM((D_FF, D), BF16),
            pltpu.VMEM((2, MXU_N, MXU_N), BF16),
            pltpu.VMEM((GATE_PAD, D_GLA_K), BF16),
            pltpu.VMEM((ts, D), BF16),
            pltpu.VMEM((2, ts, D_PROJ), F32),
            pltpu.VMEM((MAX_WINDOW, D_POOL), F32),
            pltpu.VMEM((ts, D_POOL), BF16),
            pltpu.VMEM((GLA_HEADS // 2, 2 * GLA_DV, 2 * GLA_DK), F32),
            pltpu.VMEM((2, ts, D), BF16),
            pltpu.VMEM((ts, D), F32),
            pltpu.VMEM((ts, D), BF16),
            pltpu.VMEM((CONV_TAIL, D_FF), F32),
            pltpu.VMEM((ts, D_FF), BF16),
        ],
        compiler_params=pltpu.CompilerParams(
            dimension_semantics=("arbitrary",),
            vmem_limit_bytes=VMEM_LIMIT_BYTES),
        name="decoder_layer",
    )(x2, x2, g_pre, w_in_t, w_pool, pool_scale, w_gu, b_gate, g_gla, w_out, g_post,
      g_pre2, w_ffn_in, conv_w, conv_b, w_ffn_out, g_post2)
    return out.reshape(B, S, D)


def kernel(x, g_pre_mix, w_in, w_pool, pool_scale, w_gate_up, b_gate, g_gla_norm, w_out, g_post_mix,
           g_pre_ffn, w_ffn_in, conv_w, conv_b, w_ffn_out, g_post_ffn):
    row = lambda a: a.reshape(1, -1).astype(F32)
    return _decoder_layer(
        x, row(g_pre_mix), w_in.astype(F32).T, w_pool.astype(F32),
        row(pool_scale), w_gate_up.astype(F32),
        row(b_gate), row(g_gla_norm), w_out.astype(F32), row(g_post_mix), row(g_pre_ffn),
        w_ffn_in.astype(F32), conv_w.astype(F32), row(conv_b), w_ffn_out.astype(F32),
        row(g_post_ffn))
```

```python
import functools
import math

import jax
import jax.numpy as jnp
from jax import lax
from jax.experimental import pallas as pl
from jax.experimental.pallas import tpu as pltpu

D_MODEL = 1024
D_POOL = 512
POOL_WINDOWS = (2, 4, 8, 16)
POOL_GROUP_DIM = 128
MAX_WINDOW = 16
D_GLA_V = 512
GLA_HEADS = 4
GLA_DV = 128
GLA_DK = 64
D_GLA_K = 256
GATE_RANK = 16
GATE_TAU = 16.0
D_FF = 2816
EPS = 1e-6

LANES = 128
MXU_N = 256
GATE_PAD = LANES
COL_U = 0
COL_Q = COL_U + D_POOL
COL_K = COL_Q + D_GLA_K
COL_V = COL_K + D_GLA_K
COL_G = COL_V + D_GLA_V
COL_R = COL_G + GATE_PAD
D_PROJ = COL_R + D_GLA_V
W_IN_R = COL_G + GATE_RANK
PROJ_PIECES = ((COL_U, D_POOL), (COL_Q, 2 * D_GLA_K), (COL_V, D_GLA_V + GATE_PAD))

TILE = 256
TILES_PER_STEP = 2
GLA_BLOCK = 128
GLA_DIAG = 32
FFN_CHUNK = 256
CONV_TAIL = 8

STAGE_SLOTS = 3
W_IN_ROWS = 128
W_IN_COLS = COL_G + GATE_PAD
W_UP_ROWS = 32
STAGE_ROWS = 128

VMEM_LIMIT_BYTES = 56 * 1024 * 1024

F32 = jnp.float32
BF16 = jnp.bfloat16
_NT_DIMS = (((1,), (1,)), ((), ()))
_TN_DIMS = (((0,), (0,)), ((), ()))


def _rmsnorm(x, g):
    return x * lax.rsqrt(jnp.mean(x * x, axis=-1, keepdims=True) + EPS) * g


def _layer_kernel(xn_ref, xp_ref, gpre_ref, win_hbm, wpool_f32, pscale_ref, wgu_f32, bg_ref,
                  ggla_ref, wout_hbm, gpost_ref, gpre2_ref, wfin_hbm, cw_ref, cb_ref, wfout_hbm,
                  gpost2_ref, o_ref, win_ref, wr_ref, wout_ref, wfin_ref, wfout_ref, wpool_ref, wgu_ref,
                  h_ref, proj_ref, utail_ref, d_ref, st_ref, y_ref,
                  x1_ref, h2_ref, tail_ref, act_ref, *, tiles_per_seq, n_blocks):
    ts = TILE
    L = GLA_BLOCK
    nblk = ts // L
    half = 2 * L
    i = pl.program_id(0)

    def load_weights(stage_in, stage_rows, stage_up, stage_sem):
        def cast_in_t(dst_ref):
            def cast(c0, chunk):
                dst_ref[:, c0:c0 + W_IN_ROWS] = chunk.T.astype(BF16)
            return cast

        def cast_to(dst_ref):
            def cast(r0, chunk):
                dst_ref[r0:r0 + chunk.shape[0], :] = chunk.astype(BF16)
            return cast

        jobs = []
        for c0 in range(0, W_IN_COLS, W_IN_ROWS):
            jobs.append((win_hbm, c0, c0, stage_in, cast_in_t(win_ref)))
        for c0 in range(0, D_GLA_V, W_IN_ROWS):
            jobs.append((win_hbm, W_IN_R + c0, c0, stage_in, cast_in_t(wr_ref)))
        for hbm_ref, stage, dst_ref in ((wout_hbm, stage_rows, wout_ref), (wfin_hbm, stage_up, wfin_ref),
                                        (wfout_hbm, stage_rows, wfout_ref)):
            chunk_rows = stage.shape[1]
            for r0 in range(0, hbm_ref.shape[0], chunk_rows):
                jobs.append((hbm_ref, r0, r0, stage, cast_to(dst_ref)))

        def copy(k):
            hbm_ref, r0, _, stage, _ = jobs[k]
            slot = k % STAGE_SLOTS
            return pltpu.make_async_copy(hbm_ref.at[pl.ds(r0, stage.shape[1]), :], stage.at[slot],
                                         stage_sem.at[slot])

        for k in range(STAGE_SLOTS):
            copy(k).start()
        for k in range(len(jobs)):
            _, _, d0, stage, cast = jobs[k]
            copy(k).wait()
            cast(d0, stage[k % STAGE_SLOTS])
            if k + STAGE_SLOTS < len(jobs):
                copy(k + STAGE_SLOTS).start()

    def pack_small_weights():
        g = POOL_GROUP_DIM
        wpool_ref[...] = jnp.zeros_like(wpool_ref)
        for grp in range(len(POOL_WINDOWS)):
            o = (grp % 2) * g
            wpool_ref[grp // 2, o:o + g, o:o + g] = wpool_f32[grp].astype(BF16)
        wgu_ref[...] = jnp.zeros_like(wgu_ref)
        wgu_ref[0:GATE_RANK, :] = wgu_f32[...].astype(BF16)

    def step(sub, do_a, do_b, do_cf):
        t = TILES_PER_STEP * i + sub
        slot_a = sub % 2
        slot_b = 1 - slot_a
        rows_t = slice(sub * ts, (sub + 1) * ts)
        j = lax.rem(jnp.maximum(t - 1, 0), tiles_per_seq)
        jf = lax.rem(jnp.maximum(t - 2, 0), tiles_per_seq)
        pnew = proj_ref.at[slot_a]
        pj = proj_ref.at[slot_b]
        y_new = y_ref.at[slot_b]
        y_old = y_ref.at[slot_a]
        vals = {}

        if do_b:
            @pl.when(j == 0)
            def _():
                utail_ref[...] = jnp.zeros_like(utail_ref)
                st_ref[...] = jnp.zeros_like(st_ref)

        if do_cf:
            @pl.when(jf == 0)
            def _():
                tail_ref[...] = jnp.zeros_like(tail_ref)

        def prenorm():
            h_ref[...] = _rmsnorm(xn_ref[rows_t, :], gpre_ref[...]).astype(BF16)

        def proj_piece(p):
            n0, w = PROJ_PIECES[p]
            pnew[:, n0:n0 + w] = jnp.dot(h_ref[...], win_ref[:, n0:n0 + w],
                                         preferred_element_type=F32)

        def proj_r():
            pnew[:, COL_R:COL_R + D_GLA_V] = jnp.dot(h_ref[...], wr_ref[...],
                                                     preferred_element_type=F32)

        def out_proj():
            vals["mix"] = jnp.dot(y_old[...], wout_ref[...], preferred_element_type=F32)

        def out_finish():
            x1 = xp_ref[rows_t, :] + _rmsnorm(vals["mix"], gpost_ref[...])
            x1_ref[...] = x1
            h2_ref[...] = _rmsnorm(x1, gpre2_ref[...]).astype(BF16)

        def ffn_chunk(c):
            row = lax.broadcasted_iota(jnp.int32, (CONV_TAIL, FFN_CHUNK), 0)
            cols = slice(c * FFN_CHUNK, (c + 1) * FFN_CHUNK)
            vcols = slice(D_FF + c * FFN_CHUNK, D_FF + (c + 1) * FFN_CHUNK)
            gate = jnp.dot(h2_ref[...], wfin_ref[:, cols], preferred_element_type=F32)
            val = jnp.dot(h2_ref[...], wfin_ref[:, vcols], preferred_element_type=F32)
            prev = tail_ref[:, cols]
            tail_ref[:, cols] = gate[ts - CONV_TAIL:ts, :]
            p1 = prev[CONV_TAIL - 1:CONV_TAIL, :]
            p2 = prev[CONV_TAIL - 2:CONV_TAIL - 1, :]
            r1 = pltpu.roll(gate, 1, 0)
            r2 = pltpu.roll(gate, 2, 0)
            h1 = jnp.where(row == 0, p1, r1[:CONV_TAIL])
            h2 = jnp.where(row == 0, p2, jnp.where(row == 1, p1, r2[:CONV_TAIL]))
            g1 = jnp.concatenate([h1, r1[CONV_TAIL:]], axis=0)
            g2 = jnp.concatenate([h2, r2[CONV_TAIL:]], axis=0)
            cw = cw_ref[:, cols] * 0.5
            hc = g2 * cw[0:1, :] + g1 * cw[1:2, :] + gate * cw[2:3, :] + cb_ref[:, cols] * 0.5
            act = hc * (1.0 + lax.erf(hc * math.sqrt(2.0))) * val
            act_ref[:, cols] = act.astype(BF16)

        def ffn_down():
            vals["ff"] = jnp.dot(act_ref[...], wfout_ref[...], preferred_element_type=F32)

        def ffn_finish():
            o_ref[rows_t, :] = x1_ref[...] + _rmsnorm(vals["ff"], gpost2_ref[...])

        def pool_sums():
            row = lax.broadcasted_iota(jnp.int32, (MAX_WINDOW, 1), 0)
            pos = (j * ts + row + 1).astype(F32)
            for g, w in enumerate(POOL_WINDOWS):
                cols = slice(g * POOL_GROUP_DIM, (g + 1) * POOL_GROUP_DIM)
                u_g = pj[:, COL_U + g * POOL_GROUP_DIM:COL_U + (g + 1) * POOL_GROUP_DIM]
                ext = jnp.concatenate([utail_ref[:, cols], u_g], axis=0)
                utail_ref[:, cols] = u_g[ts - MAX_WINDOW:, :]
                acc = ext
                span = 1
                while span < w:
                    acc = acc + pltpu.roll(acc, span, 0)
                    span *= 2
                head = acc[MAX_WINDOW:2 * MAX_WINDOW, :] / jnp.minimum(pos, float(w))
                body = acc[2 * MAX_WINDOW:, :] * (1.0 / w)
                d = jnp.concatenate([head, body], axis=0) - u_g
                d_ref[:, cols] = d.astype(BF16)

        def pool_maps():
            for pair in range(2):
                cols = slice(pair * MXU_N, (pair + 1) * MXU_N)
                yp = jnp.dot(d_ref[:, cols], wpool_ref[pair], preferred_element_type=F32)
                y_new[:, cols] = (yp * pscale_ref[:, cols]).astype(BF16)

        def gate_stage():
            glow = pj[:, COL_G:COL_G + GATE_PAD].astype(BF16)
            logits = jnp.dot(glow, wgu_ref[...], preferred_element_type=F32) + bg_ref[...]
            log_a = ((jnp.minimum(logits, 0.0) - jnp.log(1.0 + jnp.exp(-jnp.abs(logits))))
                     * (1.0 / GATE_TAU))
            la_hi = log_a.astype(BF16)
            vals["la"] = (la_hi, (log_a - la_hi.astype(F32)).astype(BF16))

        def prefix_stage():
            la_hi, la_lo = vals["la"]
            ri = lax.broadcasted_iota(jnp.int32, (half, half), 0)
            ci = lax.broadcasted_iota(jnp.int32, (half, half), 1)
            blk_start = ri - lax.rem(ri, L)
            tri_bd = jnp.where(ci <= ri, jnp.where(ci >= blk_start, 1.0, 0.0), 0.0).astype(BF16)
            bcs = []
            for hb in range(ts // half):
                hrows = slice(hb * half, (hb + 1) * half)
                bc2 = (jnp.dot(tri_bd, la_hi[hrows, :], preferred_element_type=F32)
                       + jnp.dot(tri_bd, la_lo[hrows, :], preferred_element_type=F32))
                bcs += [bc2[0:L, :], bc2[L:half, :]]
            vals["bcs"] = bcs
            vals["st"] = [st_ref[0], st_ref[1]]

        def scores_stage(nb):
            lane_first = lax.broadcasted_iota(jnp.int32, (L, 2 * GLA_DK), 1) < GLA_DK
            ri2 = lax.broadcasted_iota(jnp.int32, (L, 2 * L), 0)
            ci2 = lax.rem(lax.broadcasted_iota(jnp.int32, (L, 2 * L), 1), L)
            st_row_first = lax.broadcasted_iota(jnp.int32, (2 * GLA_DV, 2 * GLA_DK), 0) < GLA_DV
            st_col_first = lax.broadcasted_iota(jnp.int32, (2 * GLA_DV, 2 * GLA_DK), 1) < GLA_DK
            st_mask = st_row_first == st_col_first
            rows = slice(nb * L, (nb + 1) * L)
            bc = vals["bcs"][nb]
            b_last = bc[L - 1:L, :]
            q = pj[rows, COL_Q:COL_Q + D_GLA_K] * (GLA_DK ** -0.5)
            k = pj[rows, COL_K:COL_K + D_GLA_K]
            q_inter = (q * jnp.exp(bc)).astype(BF16)
            k_hat = (k * jnp.exp(b_last - bc)).astype(BF16)

            def ref_rows(group, offset):
                parts = [jnp.broadcast_to(bc[g0 + offset:g0 + offset + 1, :], (group, D_GLA_K))
                         for g0 in range(0, L, group)]
                return parts[0] if len(parts) == 1 else jnp.concatenate(parts, axis=0)

            levels = []
            size = GLA_DIAG
            levels.append((size, ref_rows(size, size // 2 - 1)))
            while size < L:
                levels.append((2 * size, ref_rows(2 * size, size - 1)))
                size *= 2
            a_pairs = [None] * (GLA_HEADS // 2)
            finer = None
            for size, b_ref in levels:
                e_fwd = jnp.exp(bc - b_ref)
                q_l = (q * e_fwd).astype(BF16)
                k_l = k * (1.0 / e_fwd)
                for pair in range(GLA_HEADS // 2):
                    kcols = slice(pair * 2 * GLA_DK, (pair + 1) * 2 * GLA_DK)
                    k_p = k_l[:, kcols]
                    k_pair = jnp.concatenate([jnp.where(lane_first, k_p, 0.0),
                                              jnp.where(lane_first, 0.0, k_p)], axis=0).astype(BF16)
                    s_pair = lax.dot_general(q_l[:, kcols], k_pair, _NT_DIMS,
                                             preferred_element_type=F32)
                    if finer is None:
                        a_pairs[pair] = s_pair
                    else:
                        same_finer = (ri2 // finer) == (ci2 // finer)
                        a_pairs[pair] = jnp.where(same_finer, a_pairs[pair], s_pair)
                finer = size
            causal2 = ri2 >= ci2
            a_pairs = [jnp.where(causal2, a, 0.0).astype(BF16) for a in a_pairs]
            v_blk = pj[rows, COL_V:COL_V + D_GLA_V].astype(BF16)
            kv_t = lax.dot_general(v_blk, k_hat, _TN_DIMS, preferred_element_type=F32)
            decay = jnp.exp(b_last)
            st_in = vals["st"]
            st_out = []
            for pair in range(GLA_HEADS // 2):
                pcols = slice(pair * MXU_N, (pair + 1) * MXU_N)
                kcols = slice(pair * 2 * GLA_DK, (pair + 1) * 2 * GLA_DK)
                st_out.append(st_in[pair] * decay[:, kcols]
                              + jnp.where(st_mask, kv_t[pcols, kcols], 0.0))
            vals["st"] = st_out
            vals[("blk", nb)] = (a_pairs, q_inter, [st_p.astype(BF16) for st_p in st_in])

        def output_stage(nb):
            lane_low = lax.broadcasted_iota(jnp.int32, (L, MXU_N), 1) < GLA_DV
            ggla = ggla_ref[...]
            a_pairs, q_inter, st_b = vals[("blk", nb)]
            rows = slice(nb * L, (nb + 1) * L)
            for pair in range(GLA_HEADS // 2):
                pcols = slice(pair * MXU_N, (pair + 1) * MXU_N)
                kcols = slice(pair * 2 * GLA_DK, (pair + 1) * 2 * GLA_DK)
                v_pair = pj[rows, COL_V + pair * MXU_N:COL_V + (pair + 1) * MXU_N]
                v_bd = jnp.concatenate([jnp.where(lane_low, v_pair, 0.0),
                                        jnp.where(lane_low, 0.0, v_pair)], axis=0).astype(BF16)
                o_inter = lax.dot_general(q_inter[:, kcols], st_b[pair], _NT_DIMS,
                                          preferred_element_type=F32)
                o_pair = jnp.dot(a_pairs[pair], v_bd, preferred_element_type=F32) + o_inter
                for hh in range(2):
                    hcols = slice(hh * GLA_DV, (hh + 1) * GLA_DV)
                    o_h = _rmsnorm(o_pair[:, hcols], ggla)
                    c0 = COL_R + pair * MXU_N + hh * GLA_DV
                    r_h = pj[rows, c0:c0 + GLA_DV]
                    o_h = o_h * (r_h * jax.nn.sigmoid(r_h))
                    y0 = D_POOL + pair * MXU_N + hh * GLA_DV
                    y_new[rows, y0:y0 + GLA_DV] = o_h.astype(BF16)

        if do_b:
            gate_stage()
        if do_cf:
            out_proj()
        if do_a:
            prenorm()
        if do_b:
            pool_sums()
        if do_a:
            proj_piece(0)
        if do_cf:
            out_finish()
        if do_b:
            prefix_stage()
        if do_a:
            proj_piece(1)
        if do_cf:
            ffn_chunk(0)
        if do_b:
            pool_maps()
            for nb in range(nblk):
                scores_stage(nb)
            st_ref[0] = vals["st"][0]
            st_ref[1] = vals["st"][1]
        if do_cf:
            ffn_chunk(1)
            ffn_chunk(2)
            ffn_chunk(3)
        if do_b:
            for nb in range(nblk):
                output_stage(nb)
        if do_cf:
            for c in range(4, D_FF // FFN_CHUNK):
                ffn_chunk(c)
        if do_a:
            proj_piece(2)
        if do_cf:
            ffn_down()
        if do_a:
            proj_r()
        if do_cf:
            ffn_finish()

    @pl.when(i == 0)
    def _():
        pl.run_scoped(load_weights,
                      pltpu.VMEM((STAGE_SLOTS, W_IN_ROWS, D_MODEL), F32),
                      pltpu.VMEM((STAGE_SLOTS, STAGE_ROWS, D_MODEL), F32),
                      pltpu.VMEM((STAGE_SLOTS, W_UP_ROWS, 2 * D_FF), F32),
                      pltpu.SemaphoreType.DMA((STAGE_SLOTS,)))
        pack_small_weights()
        step(0, True, False, False)
        step(1, True, True, False)

    @pl.when(jnp.logical_and(i >= 1, i < n_blocks))
    def _():
        step(0, True, True, True)
        step(1, True, True, True)

    @pl.when(i == n_blocks)
    def _():
        step(0, False, True, True)
        step(1, False, False, True)


def _const_spec(shape):
    zeros = (0,) * len(shape)
    return pl.BlockSpec(shape, lambda i: zeros, pipeline_mode=pl.Buffered(1))


def _decoder_layer(x, g_pre, w_in_t, w_pool, pool_scale, w_gu, b_gate, g_gla, w_out, g_post,
                   g_pre2, w_ffn_in, conv_w, conv_b, w_ffn_out, g_post2):
    B, S, D = x.shape
    ts = TILE
    bs = TILES_PER_STEP * ts
    n_blocks = (B * S) // bs
    x2 = x.reshape(B * S, D)
    kern = functools.partial(_layer_kernel, tiles_per_seq=S // ts, n_blocks=n_blocks)
    cs = _const_spec
    hbm = pl.BlockSpec(memory_space=pl.ANY)
    out = pl.pallas_call(
        kern,
        grid=(n_blocks + 1,),
        in_specs=[
            pl.BlockSpec((bs, D), lambda i: (jnp.minimum(i, n_blocks - 1), 0)),
            pl.BlockSpec((bs, D), lambda i: (jnp.maximum(i - 1, 0), 0)),
            cs((1, D)),
            hbm,
            cs((len(POOL_WINDOWS), POOL_GROUP_DIM, POOL_GROUP_DIM)),
            cs((1, D_POOL)),
            cs((GATE_RANK, D_GLA_K)),
            cs((1, D_GLA_K)),
            cs((1, GLA_DV)),
            hbm,
            cs((1, D)),
            cs((1, D)),
            hbm,
            cs((3, D_FF)),
            cs((1, D_FF)),
            hbm,
            cs((1, D)),
        ],
        out_specs=pl.BlockSpec((bs, D), lambda i: (jnp.maximum(i - 1, 0), 0)),
        out_shape=jax.ShapeDtypeStruct((B * S, D), F32),
        scratch_shapes=[
            pltpu.VMEM((D, W_IN_COLS), BF16),
            pltpu.VMEM((D, D_GLA_V), BF16),
            pltpu.VMEM((D, D), BF16),
            pltpu.VMEM((D, 2 * D_FF), BF16),
            pltpu.VMEM((D_FF, D), BF16),
            pltpu.VMEM((2, MXU_N, MXU_N), BF16),
            pltpu.VMEM((GATE_PAD, D_GLA_K), BF16),
            pltpu.VMEM((ts, D), BF16),
            pltpu.VMEM((2, ts, D_PROJ), F32),
            pltpu.VMEM((MAX_WINDOW, D_POOL), F32),
            pltpu.VMEM((ts, D_POOL), BF16),
            pltpu.VMEM((GLA_HEADS // 2, 2 * GLA_DV, 2 * GLA_DK), F32),
            pltpu.VMEM((2, ts, D), BF16),
            pltpu.VMEM((ts, D), F32),
            pltpu.VMEM((ts, D), BF16),
            pltpu.VMEM((CONV_TAIL, D_FF), F32),
            pltpu.VMEM((ts, D_FF), BF16),
        ],
        compiler_params=pltpu.CompilerParams(
            dimension_semantics=("arbitrary",),
            vmem_limit_bytes=VMEM_LIMIT_BYTES),
        name="decoder_layer",
    )(x2, x2, g_pre, w_in_t, w_pool, pool_scale, w_gu, b_gate, g_gla, w_out, g_post,
      g_pre2, w_ffn_in, conv_w, conv_b, w_ffn_out, g_post2)
    return out.reshape(B, S, D)


def kernel(x, g_pre_mix, w_in, w_pool, pool_scale, w_gate_up, b_gate, g_gla_norm, w_out, g_post_mix,
           g_pre_ffn, w_ffn_in, conv_w, conv_b, w_ffn_out, g_post_ffn):
    row = lambda a: a.reshape(1, -1).astype(F32)
    return _decoder_layer(
        x, row(g_pre_mix), w_in.astype(F32).T, w_pool.astype(F32),
        row(pool_scale), w_gate_up.astype(F32),
        row(b_gate), row(g_gla_norm), w_out.astype(F32), row(g_post_mix), row(g_pre_ffn),
        w_ffn_in.astype(F32), conv_w.astype(F32), row(conv_b), w_ffn_out.astype(F32),
        row(g_post_ffn))
```

```python
import functools
import math

import jax
import jax.numpy as jnp
from jax import lax
from jax.experimental import pallas as pl
from jax.experimental.pallas import tpu as pltpu

D_MODEL = 1024
D_POOL = 512
POOL_WINDOWS = (2, 4, 8, 16)
POOL_GROUP_DIM = 128
MAX_WINDOW = 16
D_GLA_V = 512
GLA_HEADS = 4
GLA_DV = 128
GLA_DK = 64
D_GLA_K = 256
GATE_RANK = 16
GATE_TAU = 16.0
D_FF = 2816
EPS = 1e-6

LANES = 128
MXU_N = 256
GATE_PAD = LANES
COL_U = 0
COL_Q = COL_U + D_POOL
COL_K = COL_Q + D_GLA_K
COL_V = COL_K + D_GLA_K
COL_G = COL_V + D_GLA_V
COL_R = COL_G + GATE_PAD
D_PROJ = COL_R + D_GLA_V
W_IN_R = COL_G + GATE_RANK
PROJ_PIECES = ((COL_U, D_POOL), (COL_Q, 2 * D_GLA_K), (COL_V, D_GLA_V + GATE_PAD))

TILE = 256
TILES_PER_STEP = 2
GLA_BLOCK = 128
GLA_DIAG = 32
FFN_CHUNK = 256
CONV_TAIL = 8

STAGE_SLOTS = 3
W_IN_ROWS = 128
W_IN_COLS = COL_G + GATE_PAD
W_UP_ROWS = 32
STAGE_ROWS = 128

VMEM_LIMIT_BYTES = 56 * 1024 * 1024

F32 = jnp.float32
BF16 = jnp.bfloat16
_NT_DIMS = (((1,), (1,)), ((), ()))
_TN_DIMS = (((0,), (0,)), ((), ()))


def _rms_scale(x):
    return x * lax.rsqrt(jnp.mean(x * x, axis=-1, keepdims=True) + EPS)


def _rmsnorm(x, g):
    return _rms_scale(x) * g


def _layer_kernel(xn_ref, xp_ref, gpre_ref, win_hbm, wpool_f32, pscale_ref, wgu_f32, bg_ref,
                  ggla_ref, wout_hbm, gpost_ref, gpre2_ref, wfin_hbm, cw_ref, cb_ref, wfout_hbm,
                  gpost2_ref, o_ref, win_ref, wr_ref, wout_ref, wfin_ref, wfout_ref, wpool_ref, wgu_ref,
                  h_ref, proj_ref, utail_ref, d_ref, st_ref, y_ref,
                  x1_ref, h2_ref, tail_ref, act_ref, *, tiles_per_seq, n_blocks):
    ts = TILE
    L = GLA_BLOCK
    nblk = ts // L
    half = 2 * L
    i = pl.program_id(0)

    def load_weights(stage_in, stage_rows, stage_up, stage_sem):
        def cast_in_t(dst_ref, scale=1.0):
            def cast(c0, chunk):
                dst_ref[:, c0:c0 + W_IN_ROWS] = (chunk * (gpre_ref[...] * scale)).T.astype(BF16)
            return cast

        def cast_to(dst_ref):
            def cast(r0, chunk):
                dst_ref[r0:r0 + chunk.shape[0], :] = chunk.astype(BF16)
            return cast

        jobs = []
        for c0 in range(0, W_IN_COLS, W_IN_ROWS):
            q_scale = GLA_DK ** -0.5 if COL_Q <= c0 < COL_K else 1.0
            jobs.append((win_hbm, c0, c0, stage_in, cast_in_t(win_ref, q_scale)))
        for c0 in range(0, D_GLA_V, W_IN_ROWS):
            jobs.append((win_hbm, W_IN_R + c0, c0, stage_in, cast_in_t(wr_ref)))
        for hbm_ref, stage, dst_ref in ((wout_hbm, stage_rows, wout_ref), (wfin_hbm, stage_up, wfin_ref),
                                        (wfout_hbm, stage_rows, wfout_ref)):
            chunk_rows = stage.shape[1]
            for r0 in range(0, hbm_ref.shape[0], chunk_rows):
                jobs.append((hbm_ref, r0, r0, stage, cast_to(dst_ref)))

        def copy(k):
            hbm_ref, r0, _, stage, _ = jobs[k]
            slot = k % STAGE_SLOTS
            return pltpu.make_async_copy(hbm_ref.at[pl.ds(r0, stage.shape[1]), :], stage.at[slot],
                                         stage_sem.at[slot])

        for k in range(STAGE_SLOTS):
            copy(k).start()
        for k in range(len(jobs)):
            _, _, d0, stage, cast = jobs[k]
            copy(k).wait()
            cast(d0, stage[k % STAGE_SLOTS])
            if k + STAGE_SLOTS < len(jobs):
                copy(k + STAGE_SLOTS).start()

    def pack_small_weights():
        g = POOL_GROUP_DIM
        wpool_ref[...] = jnp.zeros_like(wpool_ref)
        for grp in range(len(POOL_WINDOWS)):
            o = (grp % 2) * g
            wpool_ref[grp // 2, o:o + g, o:o + g] = wpool_f32[grp].astype(BF16)
        wgu_ref[...] = jnp.zeros_like(wgu_ref)
        wgu_ref[0:GATE_RANK, :] = wgu_f32[...].astype(BF16)

    def step(sub, do_a, do_b, do_cf):
        t = TILES_PER_STEP * i + sub
        slot_a = sub % 2
        slot_b = 1 - slot_a
        rows_t = slice(sub * ts, (sub + 1) * ts)
        j = lax.rem(jnp.maximum(t - 1, 0), tiles_per_seq)
        jf = lax.rem(jnp.maximum(t - 2, 0), tiles_per_seq)
        pnew = proj_ref.at[slot_a]
        pj = proj_ref.at[slot_b]
        y_new = y_ref.at[slot_b]
        y_old = y_ref.at[slot_a]
        vals = {}

        if do_b:
            @pl.when(j == 0)
            def _():
                utail_ref[...] = jnp.zeros_like(utail_ref)
                st_ref[...] = jnp.zeros_like(st_ref)

        if do_cf:
            @pl.when(jf == 0)
            def _():
                tail_ref[...] = jnp.zeros_like(tail_ref)

        def prenorm():
            h_ref[...] = _rms_scale(xn_ref[rows_t, :]).astype(BF16)

        def proj_piece(p):
            n0, w = PROJ_PIECES[p]
            pnew[:, n0:n0 + w] = jnp.dot(h_ref[...], win_ref[:, n0:n0 + w],
                                         preferred_element_type=F32)

        def proj_r():
            pnew[:, COL_R:COL_R + D_GLA_V] = jnp.dot(h_ref[...], wr_ref[...],
                                                     preferred_element_type=F32)

        def out_proj():
            vals["mix"] = jnp.dot(y_old[...], wout_ref[...], preferred_element_type=F32)

        def out_finish():
            x1 = xp_ref[rows_t, :] + _rmsnorm(vals["mix"], gpost_ref[...])
            x1_ref[...] = x1
            h2_ref[...] = _rmsnorm(x1, gpre2_ref[...]).astype(BF16)

        def ffn_chunk(c):
            row = lax.broadcasted_iota(jnp.int32, (CONV_TAIL, FFN_CHUNK), 0)
            cols = slice(c * FFN_CHUNK, (c + 1) * FFN_CHUNK)
            vcols = slice(D_FF + c * FFN_CHUNK, D_FF + (c + 1) * FFN_CHUNK)
            gate = jnp.dot(h2_ref[...], wfin_ref[:, cols], preferred_element_type=F32)
            val = jnp.dot(h2_ref[...], wfin_ref[:, vcols], preferred_element_type=F32)
            prev = tail_ref[:, cols]
            tail_ref[:, cols] = gate[ts - CONV_TAIL:ts, :]
            p1 = prev[CONV_TAIL - 1:CONV_TAIL, :]
            p2 = prev[CONV_TAIL - 2:CONV_TAIL - 1, :]
            r1 = pltpu.roll(gate, 1, 0)
            r2 = pltpu.roll(gate, 2, 0)
            h1 = jnp.where(row == 0, p1, r1[:CONV_TAIL])
            h2 = jnp.where(row == 0, p2, jnp.where(row == 1, p1, r2[:CONV_TAIL]))
            g1 = jnp.concatenate([h1, r1[CONV_TAIL:]], axis=0)
            g2 = jnp.concatenate([h2, r2[CONV_TAIL:]], axis=0)
            cw = cw_ref[:, cols] * 0.5
            hc = g2 * cw[0:1, :] + g1 * cw[1:2, :] + gate * cw[2:3, :] + cb_ref[:, cols] * 0.5
            act = hc * (1.0 + lax.erf(hc * math.sqrt(2.0))) * val
            act_ref[:, cols] = act.astype(BF16)

        def ffn_down():
            vals["ff"] = jnp.dot(act_ref[...], wfout_ref[...], preferred_element_type=F32)

        def ffn_finish():
            o_ref[rows_t, :] = x1_ref[...] + _rmsnorm(vals["ff"], gpost2_ref[...])

        def pool_sums():
            row = lax.broadcasted_iota(jnp.int32, (MAX_WINDOW, 1), 0)
            pos = (j * ts + row + 1).astype(F32)
            for g, w in enumerate(POOL_WINDOWS):
                cols = slice(g * POOL_GROUP_DIM, (g + 1) * POOL_GROUP_DIM)
                u_g = pj[:, COL_U + g * POOL_GROUP_DIM:COL_U + (g + 1) * POOL_GROUP_DIM]
                ext = jnp.concatenate([utail_ref[:, cols], u_g], axis=0)
                utail_ref[:, cols] = u_g[ts - MAX_WINDOW:, :]
                acc = ext
                span = 1
                while span < w:
                    acc = acc + pltpu.roll(acc, span, 0)
                    span *= 2
                head = acc[MAX_WINDOW:2 * MAX_WINDOW, :] / jnp.minimum(pos, float(w))
                body = acc[2 * MAX_WINDOW:, :] * (1.0 / w)
                d = jnp.concatenate([head, body], axis=0) - u_g
                d_ref[:, cols] = d.astype(BF16)

        def pool_maps():
            for pair in range(2):
                cols = slice(pair * MXU_N, (pair + 1) * MXU_N)
                yp = jnp.dot(d_ref[:, cols], wpool_ref[pair], preferred_element_type=F32)
                y_new[:, cols] = (yp * pscale_ref[:, cols]).astype(BF16)

        def gate_stage():
            glow = pj[:, COL_G:COL_G + GATE_PAD].astype(BF16)
            logits = jnp.dot(glow, wgu_ref[...], preferred_element_type=F32) + bg_ref[...]
            log_a = ((jnp.minimum(logits, 0.0) - jnp.log(1.0 + jnp.exp(-jnp.abs(logits))))
                     * (1.0 / GATE_TAU))
            la_hi = log_a.astype(BF16)
            vals["la"] = (la_hi, (log_a - la_hi.astype(F32)).astype(BF16))

        def prefix_stage():
            la_hi, la_lo = vals["la"]
            ri = lax.broadcasted_iota(jnp.int32, (half, half), 0)
            ci = lax.broadcasted_iota(jnp.int32, (half, half), 1)
            blk_start = ri - lax.rem(ri, L)
            tri_bd = jnp.where(ci <= ri, jnp.where(ci >= blk_start, 1.0, 0.0), 0.0).astype(BF16)
            bcs = []
            for hb in range(ts // half):
                hrows = slice(hb * half, (hb + 1) * half)
                bc2 = (jnp.dot(tri_bd, la_hi[hrows, :], preferred_element_type=F32)
                       + jnp.dot(tri_bd, la_lo[hrows, :], preferred_element_type=F32))
                bcs += [bc2[0:L, :], bc2[L:half, :]]
            vals["bcs"] = bcs
            vals["st"] = [st_ref[0], st_ref[1]]

        def scores_stage(nb):
            lane_first = lax.broadcasted_iota(jnp.int32, (L, 2 * GLA_DK), 1) < GLA_DK
            ri2 = lax.broadcasted_iota(jnp.int32, (L, 2 * L), 0)
            ci2 = lax.rem(lax.broadcasted_iota(jnp.int32, (L, 2 * L), 1), L)
            st_row_first = lax.broadcasted_iota(jnp.int32, (2 * GLA_DV, 2 * GLA_DK), 0) < GLA_DV
            st_col_first = lax.broadcasted_iota(jnp.int32, (2 * GLA_DV, 2 * GLA_DK), 1) < GLA_DK
            st_mask = st_row_first == st_col_first
            rows = slice(nb * L, (nb + 1) * L)
            bc = vals["bcs"][nb]
            b_last = bc[L - 1:L, :]
            q = pj[rows, COL_Q:COL_Q + D_GLA_K]
            k = pj[rows, COL_K:COL_K + D_GLA_K]
            q_inter = (q * jnp.exp(bc)).astype(BF16)
            k_hat = (k * jnp.exp(b_last - bc)).astype(BF16)

            def ref_rows(group, offset):
                parts = [jnp.broadcast_to(bc[g0 + offset:g0 + offset + 1, :], (group, D_GLA_K))
                         for g0 in range(0, L, group)]
                return parts[0] if len(parts) == 1 else jnp.concatenate(parts, axis=0)

            levels = []
            size = GLA_DIAG
            levels.append((size, ref_rows(size, size // 2 - 1)))
            while size < L:
                levels.append((2 * size, ref_rows(2 * size, size - 1)))
                size *= 2
            a_pairs = [None] * (GLA_HEADS // 2)
            finer = None
            for size, b_ref in levels:
                e_fwd = jnp.exp(bc - b_ref)
                q_l = (q * e_fwd).astype(BF16)
                k_l = k * (1.0 / e_fwd)
                for pair in range(GLA_HEADS // 2):
                    kcols = slice(pair * 2 * GLA_DK, (pair + 1) * 2 * GLA_DK)
                    k_p = k_l[:, kcols]
                    k_pair = jnp.concatenate([jnp.where(lane_first, k_p, 0.0),
                                              jnp.where(lane_first, 0.0, k_p)], axis=0).astype(BF16)
                    s_pair = lax.dot_general(q_l[:, kcols], k_pair, _NT_DIMS,
                                             preferred_element_type=F32)
                    if finer is None:
                        a_pairs[pair] = s_pair
                    else:
                        same_finer = (ri2 // finer) == (ci2 // finer)
                        a_pairs[pair] = jnp.where(same_finer, a_pairs[pair], s_pair)
                finer = size
            causal2 = ri2 >= ci2
            a_pairs = [jnp.where(causal2, a, 0.0).astype(BF16) for a in a_pairs]
            v_blk = pj[rows, COL_V:COL_V + D_GLA_V].astype(BF16)
            kv_t = lax.dot_general(v_blk, k_hat, _TN_DIMS, preferred_element_type=F32)
            decay = jnp.exp(b_last)
            st_in = vals["st"]
            st_out = []
            for pair in range(GLA_HEADS // 2):
                pcols = slice(pair * MXU_N, (pair + 1) * MXU_N)
                kcols = slice(pair * 2 * GLA_DK, (pair + 1) * 2 * GLA_DK)
                st_out.append(st_in[pair] * decay[:, kcols]
                              + jnp.where(st_mask, kv_t[pcols, kcols], 0.0))
            vals["st"] = st_out
            vals[("blk", nb)] = (a_pairs, q_inter, [st_p.astype(BF16) for st_p in st_in])

        def output_stage(nb):
            lane_low = lax.broadcasted_iota(jnp.int32, (L, MXU_N), 1) < GLA_DV
            ggla = ggla_ref[...]
            a_pairs, q_inter, st_b = vals[("blk", nb)]
            rows = slice(nb * L, (nb + 1) * L)
            for pair in range(GLA_HEADS // 2):
                pcols = slice(pair * MXU_N, (pair + 1) * MXU_N)
                kcols = slice(pair * 2 * GLA_DK, (pair + 1) * 2 * GLA_DK)
                v_pair = pj[rows, COL_V + pair * MXU_N:COL_V + (pair + 1) * MXU_N]
                v_bd = jnp.concatenate([jnp.where(lane_low, v_pair, 0.0),
                                        jnp.where(lane_low, 0.0, v_pair)], axis=0).astype(BF16)
                o_inter = lax.dot_general(q_inter[:, kcols], st_b[pair], _NT_DIMS,
                                          preferred_element_type=F32)
                o_pair = jnp.dot(a_pairs[pair], v_bd, preferred_element_type=F32) + o_inter
                for hh in range(2):
                    hcols = slice(hh * GLA_DV, (hh + 1) * GLA_DV)
                    o_h = _rmsnorm(o_pair[:, hcols], ggla)
                    c0 = COL_R + pair * MXU_N + hh * GLA_DV
                    r_h = pj[rows, c0:c0 + GLA_DV]
                    o_h = o_h * (r_h * jax.nn.sigmoid(r_h))
                    y0 = D_POOL + pair * MXU_N + hh * GLA_DV
                    y_new[rows, y0:y0 + GLA_DV] = o_h.astype(BF16)

        if do_b:
            gate_stage()
        if do_cf:
            out_proj()
        if do_a:
            prenorm()
        if do_b:
            pool_sums()
        if do_a:
            proj_piece(0)
        if do_cf:
            out_finish()
        if do_b:
            prefix_stage()
        if do_a:
            proj_piece(1)
        if do_cf:
            ffn_chunk(0)
        if do_b:
            pool_maps()
            for nb in range(nblk):
                scores_stage(nb)
            st_ref[0] = vals["st"][0]
            st_ref[1] = vals["st"][1]
        if do_cf:
            ffn_chunk(1)
            ffn_chunk(2)
            ffn_chunk(3)
        if do_b:
            for nb in range(nblk):
                output_stage(nb)
        if do_cf:
            for c in range(4, D_FF // FFN_CHUNK):
                ffn_chunk(c)
        if do_a:
            proj_piece(2)
        if do_cf:
            ffn_down()
        if do_a:
            proj_r()
        if do_cf:
            ffn_finish()

    @pl.when(i == 0)
    def _():
        pl.run_scoped(load_weights,
                      pltpu.VMEM((STAGE_SLOTS, W_IN_ROWS, D_MODEL), F32),
                      pltpu.VMEM((STAGE_SLOTS, STAGE_ROWS, D_MODEL), F32),
                      pltpu.VMEM((STAGE_SLOTS, W_UP_ROWS, 2 * D_FF), F32),
                      pltpu.SemaphoreType.DMA((STAGE_SLOTS,)))
        pack_small_weights()
        step(0, True, False, False)
        step(1, True, True, False)

    @pl.when(jnp.logical_and(i >= 1, i < n_blocks))
    def _():
        step(0, True, True, True)
        step(1, True, True, True)

    @pl.when(i == n_blocks)
    def _():
        step(0, False, True, True)
        step(1, False, False, True)


def _const_spec(shape):
    zeros = (0,) * len(shape)
    return pl.BlockSpec(shape, lambda i: zeros, pipeline_mode=pl.Buffered(1))


def _decoder_layer(x, g_pre, w_in_t, w_pool, pool_scale, w_gu, b_gate, g_gla, w_out, g_post,
                   g_pre2, w_ffn_in, conv_w, conv_b, w_ffn_out, g_post2):
    B, S, D = x.shape
    ts = TILE
    bs = TILES_PER_STEP * ts
    n_blocks = (B * S) // bs
    x2 = x.reshape(B * S, D)
    kern = functools.partial(_layer_kernel, tiles_per_seq=S // ts, n_blocks=n_blocks)
    cs = _const_spec
    hbm = pl.BlockSpec(memory_space=pl.ANY)
    out = pl.pallas_call(
        kern,
        grid=(n_blocks + 1,),
        in_specs=[
            pl.BlockSpec((bs, D), lambda i: (jnp.minimum(i, n_blocks - 1), 0)),
            pl.BlockSpec((bs, D), lambda i: (jnp.maximum(i - 1, 0), 0)),
            cs((1, D)),
            hbm,
            cs((len(POOL_WINDOWS), POOL_GROUP_DIM, POOL_GROUP_DIM)),
            cs((1, D_POOL)),
            cs((GATE_RANK, D_GLA_K)),
            cs((1, D_GLA_K)),
            cs((1, GLA_DV)),
            hbm,
            cs((1, D)),
            cs((1, D)),
            hbm,
            cs((3, D_FF)),
            cs((1, D_FF)),
            hbm,
            cs((1, D)),
        ],
        out_specs=pl.BlockSpec((bs, D), lambda i: (jnp.maximum(i - 1, 0), 0)),
        out_shape=jax.ShapeDtypeStruct((B * S, D), F32),
        scratch_shapes=[
            pltpu.VMEM((D, W_IN_COLS), BF16),
            pltpu.VMEM((D, D_GLA_V), BF16),
            pltpu.VMEM((D, D), BF16),
            pltpu.VMEM((D, 2 * D_FF), BF16),
            pltpu.VMEM((D_FF, D), BF16),
            pltpu.VMEM((2, MXU_N, MXU_N), BF16),
            pltpu.VMEM((GATE_PAD, D_GLA_K), BF16),
            pltpu.VMEM((ts, D), BF16),
            pltpu.VMEM((2, ts, D_PROJ), F32),
            pltpu.VMEM((MAX_WINDOW, D_POOL), F32),
            pltpu.VMEM((ts, D_POOL), BF16),
            pltpu.VMEM((GLA_HEADS // 2, 2 * GLA_DV, 2 * GLA_DK), F32),
            pltpu.VMEM((2, ts, D), BF16),
            pltpu.VMEM((ts, D), F32),
            pltpu.VMEM((ts, D), BF16),
            pltpu.VMEM((CONV_TAIL, D_FF), F32),
            pltpu.VMEM((ts, D_FF), BF16),
        ],
        compiler_params=pltpu.CompilerParams(
            dimension_semantics=("arbitrary",),
            vmem_limit_bytes=VMEM_LIMIT_BYTES),
        name="decoder_layer",
    )(x2, x2, g_pre, w_in_t, w_pool, pool_scale, w_gu, b_gate, g_gla, w_out, g_post,
      g_pre2, w_ffn_in, conv_w, conv_b, w_ffn_out, g_post2)
    return out.reshape(B, S, D)


def kernel(x, g_pre_mix, w_in, w_pool, pool_scale, w_gate_up, b_gate, g_gla_norm, w_out, g_post_mix,
           g_pre_ffn, w_ffn_in, conv_w, conv_b, w_ffn_out, g_post_ffn):
    row = lambda a: a.reshape(1, -1).astype(F32)
    return _decoder_layer(
        x, row(g_pre_mix), w_in.astype(F32).T, w_pool.astype(F32),
        row(pool_scale), w_gate_up.astype(F32),
        row(b_gate), row(g_gla_norm), w_out.astype(F32), row(g_post_mix), row(g_pre_ffn),
        w_ffn_in.astype(F32), conv_w.astype(F32), row(conv_b), w_ffn_out.astype(F32),
        row(g_post_ffn))
```

```python
import functools
import math

import jax
import jax.numpy as jnp
from jax import lax
from jax.experimental import pallas as pl
from jax.experimental.pallas import tpu as pltpu

D_MODEL = 1024
D_POOL = 512
POOL_WINDOWS = (2, 4, 8, 16)
POOL_GROUP_DIM = 128
MAX_WINDOW = 16
D_GLA_V = 512
GLA_HEADS = 4
GLA_DV = 128
GLA_DK = 64
D_GLA_K = 256
GATE_RANK = 16
GATE_TAU = 16.0
D_FF = 2816
EPS = 1e-6

LANES = 128
MXU_N = 256
GATE_PAD = LANES
COL_U = 0
COL_Q = COL_U + D_POOL
COL_K = COL_Q + D_GLA_K
COL_V = COL_K + D_GLA_K
COL_G = COL_V + D_GLA_V
COL_R = COL_G + GATE_PAD
D_PROJ = COL_R + D_GLA_V
W_IN_R = COL_G + GATE_RANK
PROJ_PIECES = ((COL_U, D_POOL), (COL_Q, 2 * D_GLA_K), (COL_V, D_GLA_V + GATE_PAD))

TILE = 256
TILES_PER_STEP = 2
GLA_BLOCK = 128
GLA_DIAG = 32
FFN_CHUNK = 256
CONV_TAIL = 8

STAGE_SLOTS = 3
W_IN_ROWS = 128
W_IN_COLS = COL_G + GATE_PAD
W_UP_ROWS = 32
STAGE_ROWS = 128

VMEM_LIMIT_BYTES = 56 * 1024 * 1024

F32 = jnp.float32
BF16 = jnp.bfloat16
_NT_DIMS = (((1,), (1,)), ((), ()))
_TN_DIMS = (((0,), (0,)), ((), ()))


def _rmsnorm(x, g):
    return x * lax.rsqrt(jnp.mean(x * x, axis=-1, keepdims=True) + EPS) * g


def _layer_kernel(xn_ref, xp_ref, gpre_ref, win_hbm, wpool_f32, pscale_ref, wgu_f32, bg_ref,
                  ggla_ref, wout_hbm, gpost_ref, gpre2_ref, wfin_hbm, cw_ref, cb_ref, wfout_hbm,
                  gpost2_ref, o_ref, win_ref, wr_ref, wout_ref, wfin_ref, wfout_ref, wpool_ref, wgu_ref,
                  h_ref, proj_ref, utail_ref, d_ref, st_ref, y_ref,
                  x1_ref, h2_ref, tail_ref, act_ref, *, tiles_per_seq, n_blocks):
    ts = TILE
    L = GLA_BLOCK
    nblk = ts // L
    half = 2 * L
    i = pl.program_id(0)

    def load_weights(stage_in, stage_rows, stage_up, stage_sem):
        def cast_in_t(dst_ref):
            def cast(c0, chunk):
                dst_ref[:, c0:c0 + W_IN_ROWS] = chunk.T.astype(BF16)
            return cast

        def cast_to(dst_ref):
            def cast(r0, chunk):
                dst_ref[r0:r0 + chunk.shape[0], :] = chunk.astype(BF16)
            return cast

        jobs = []
        for c0 in range(0, W_IN_COLS, W_IN_ROWS):
            jobs.append((win_hbm, c0, c0, stage_in, cast_in_t(win_ref)))
        for c0 in range(0, D_GLA_V, W_IN_ROWS):
            jobs.append((win_hbm, W_IN_R + c0, c0, stage_in, cast_in_t(wr_ref)))
        for hbm_ref, stage, dst_ref in ((wout_hbm, stage_rows, wout_ref), (wfin_hbm, stage_up, wfin_ref),
                                        (wfout_hbm, stage_rows, wfout_ref)):
            chunk_rows = stage.shape[1]
            for r0 in range(0, hbm_ref.shape[0], chunk_rows):
                jobs.append((hbm_ref, r0, r0, stage, cast_to(dst_ref)))

        def copy(k):
            hbm_ref, r0, _, stage, _ = jobs[k]
            slot = k % STAGE_SLOTS
            return pltpu.make_async_copy(hbm_ref.at[pl.ds(r0, stage.shape[1]), :], stage.at[slot],
                                         stage_sem.at[slot])

        for k in range(STAGE_SLOTS):
            copy(k).start()
        for k in range(len(jobs)):
            _, _, d0, stage, cast = jobs[k]
            copy(k).wait()
            cast(d0, stage[k % STAGE_SLOTS])
            if k + STAGE_SLOTS < len(jobs):
                copy(k + STAGE_SLOTS).start()

    def pack_small_weights():
        g = POOL_GROUP_DIM
        wpool_ref[...] = jnp.zeros_like(wpool_ref)
        for grp in range(len(POOL_WINDOWS)):
            o = (grp % 2) * g
            wpool_ref[grp // 2, o:o + g, o:o + g] = wpool_f32[grp].astype(BF16)
        wgu_ref[...] = jnp.zeros_like(wgu_ref)
        wgu_ref[0:GATE_RANK, :] = wgu_f32[...].astype(BF16)

    def step(sub, do_a, do_b, do_cf):
        t = TILES_PER_STEP * i + sub
        slot_a = sub % 2
        slot_b = 1 - slot_a
        rows_t = slice(sub * ts, (sub + 1) * ts)
        j = lax.rem(jnp.maximum(t - 1, 0), tiles_per_seq)
        jf = lax.rem(jnp.maximum(t - 2, 0), tiles_per_seq)
        pnew = proj_ref.at[slot_a]
        pj = proj_ref.at[slot_b]
        y_new = y_ref.at[slot_b]
        y_old = y_ref.at[slot_a]
        vals = {}

        if do_b:
            @pl.when(j == 0)
            def _():
                utail_ref[...] = jnp.zeros_like(utail_ref)
                st_ref[...] = jnp.zeros_like(st_ref)

        if do_cf:
            @pl.when(jf == 0)
            def _():
                tail_ref[...] = jnp.zeros_like(tail_ref)

        def prenorm():
            h_ref[...] = _rmsnorm(xn_ref[rows_t, :], gpre_ref[...]).astype(BF16)

        def proj_piece(p):
            n0, w = PROJ_PIECES[p]
            pnew[:, n0:n0 + w] = jnp.dot(h_ref[...], win_ref[:, n0:n0 + w],
                                         preferred_element_type=F32)

        def proj_r():
            pnew[:, COL_R:COL_R + D_GLA_V] = jnp.dot(h_ref[...], wr_ref[...],
                                                     preferred_element_type=F32)

        def out_proj():
            vals["mix"] = jnp.dot(y_old[...], wout_ref[...], preferred_element_type=F32)

        def out_finish():
            x1 = xp_ref[rows_t, :] + _rmsnorm(vals["mix"], gpost_ref[...])
            x1_ref[...] = x1
            h2_ref[...] = _rmsnorm(x1, gpre2_ref[...]).astype(BF16)

        def ffn_chunk(c):
            row = lax.broadcasted_iota(jnp.int32, (CONV_TAIL, FFN_CHUNK), 0)
            cols = slice(c * FFN_CHUNK, (c + 1) * FFN_CHUNK)
            vcols = slice(D_FF + c * FFN_CHUNK, D_FF + (c + 1) * FFN_CHUNK)
            gate = jnp.dot(h2_ref[...], wfin_ref[:, cols], preferred_element_type=F32)
            val = jnp.dot(h2_ref[...], wfin_ref[:, vcols], preferred_element_type=F32)
            prev = tail_ref[:, cols]
            tail_ref[:, cols] = gate[ts - CONV_TAIL:ts, :]
            p1 = prev[CONV_TAIL - 1:CONV_TAIL, :]
            p2 = prev[CONV_TAIL - 2:CONV_TAIL - 1, :]
            r1 = pltpu.roll(gate, 1, 0)
            r2 = pltpu.roll(gate, 2, 0)
            h1 = jnp.where(row == 0, p1, r1[:CONV_TAIL])
            h2 = jnp.where(row == 0, p2, jnp.where(row == 1, p1, r2[:CONV_TAIL]))
            g1 = jnp.concatenate([h1, r1[CONV_TAIL:]], axis=0)
            g2 = jnp.concatenate([h2, r2[CONV_TAIL:]], axis=0)
            cw = cw_ref[:, cols] * 0.5
            hc = g2 * cw[0:1, :] + g1 * cw[1:2, :] + gate * cw[2:3, :] + cb_ref[:, cols] * 0.5
            act = (hc * val) * (1.0 + lax.erf(hc * math.sqrt(2.0)))
            act_ref[:, cols] = act.astype(BF16)

        def ffn_down():
            vals["ff"] = jnp.dot(act_ref[...], wfout_ref[...], preferred_element_type=F32)

        def ffn_finish():
            o_ref[rows_t, :] = x1_ref[...] + _rmsnorm(vals["ff"], gpost2_ref[...])

        def pool_sums():
            row = lax.broadcasted_iota(jnp.int32, (MAX_WINDOW, 1), 0)
            pos = (j * ts + row + 1).astype(F32)
            for g, w in enumerate(POOL_WINDOWS):
                cols = slice(g * POOL_GROUP_DIM, (g + 1) * POOL_GROUP_DIM)
                u_g = pj[:, COL_U + g * POOL_GROUP_DIM:COL_U + (g + 1) * POOL_GROUP_DIM]
                ext = jnp.concatenate([utail_ref[:, cols], u_g], axis=0)
                utail_ref[:, cols] = u_g[ts - MAX_WINDOW:, :]
                acc = ext
                span = 1
                while span < w:
                    acc = acc + pltpu.roll(acc, span, 0)
                    span *= 2
                head = acc[MAX_WINDOW:2 * MAX_WINDOW, :] / jnp.minimum(pos, float(w))
                body = acc[2 * MAX_WINDOW:, :] * (1.0 / w)
                d = jnp.concatenate([head, body], axis=0) - u_g
                d_ref[:, cols] = d.astype(BF16)

        def pool_maps():
            for pair in range(2):
                cols = slice(pair * MXU_N, (pair + 1) * MXU_N)
                yp = jnp.dot(d_ref[:, cols], wpool_ref[pair], preferred_element_type=F32)
                y_new[:, cols] = (yp * pscale_ref[:, cols]).astype(BF16)

        def gate_stage():
            glow = pj[:, COL_G:COL_G + GATE_PAD].astype(BF16)
            logits = jnp.dot(glow, wgu_ref[...], preferred_element_type=F32) + bg_ref[...]
            log_a = ((jnp.minimum(logits, 0.0) - jnp.log(1.0 + jnp.exp(-jnp.abs(logits))))
                     * (1.0 / GATE_TAU))
            la_hi = log_a.astype(BF16)
            vals["la"] = (la_hi, (log_a - la_hi.astype(F32)).astype(BF16))

        def prefix_stage():
            la_hi, la_lo = vals["la"]
            ri = lax.broadcasted_iota(jnp.int32, (half, half), 0)
            ci = lax.broadcasted_iota(jnp.int32, (half, half), 1)
            blk_start = ri - lax.rem(ri, L)
            tri_bd = jnp.where(ci <= ri, jnp.where(ci >= blk_start, 1.0, 0.0), 0.0).astype(BF16)
            bcs = []
            for hb in range(ts // half):
                hrows = slice(hb * half, (hb + 1) * half)
                bc2 = (jnp.dot(tri_bd, la_hi[hrows, :], preferred_element_type=F32)
                       + jnp.dot(tri_bd, la_lo[hrows, :], preferred_element_type=F32))
                bcs += [bc2[0:L, :], bc2[L:half, :]]
            vals["bcs"] = bcs
            vals["st"] = [st_ref[0], st_ref[1]]

        def scores_stage(nb):
            lane_first = lax.broadcasted_iota(jnp.int32, (L, 2 * GLA_DK), 1) < GLA_DK
            ri2 = lax.broadcasted_iota(jnp.int32, (L, 2 * L), 0)
            ci2 = lax.rem(lax.broadcasted_iota(jnp.int32, (L, 2 * L), 1), L)
            st_row_first = lax.broadcasted_iota(jnp.int32, (2 * GLA_DV, 2 * GLA_DK), 0) < GLA_DV
            st_col_first = lax.broadcasted_iota(jnp.int32, (2 * GLA_DV, 2 * GLA_DK), 1) < GLA_DK
            st_mask = st_row_first == st_col_first
            rows = slice(nb * L, (nb + 1) * L)
            bc = vals["bcs"][nb]
            b_last = bc[L - 1:L, :]
            q = pj[rows, COL_Q:COL_Q + D_GLA_K] * (GLA_DK ** -0.5)
            k = pj[rows, COL_K:COL_K + D_GLA_K]
            q_inter = (q * jnp.exp(bc)).astype(BF16)
            k_hat = (k * jnp.exp(b_last - bc)).astype(BF16)

            def ref_rows(group, offset):
                parts = [jnp.broadcast_to(bc[g0 + offset:g0 + offset + 1, :], (group, D_GLA_K))
                         for g0 in range(0, L, group)]
                return parts[0] if len(parts) == 1 else jnp.concatenate(parts, axis=0)

            levels = []
            size = GLA_DIAG
            levels.append((size, ref_rows(size, size // 2 - 1)))
            while size < L:
                levels.append((2 * size, ref_rows(2 * size, size - 1)))
                size *= 2
            a_pairs = [None] * (GLA_HEADS // 2)
            finer = None
            for size, b_ref in levels:
                e_fwd = jnp.exp(bc - b_ref)
                q_l = (q * e_fwd).astype(BF16)
                k_l = k * (1.0 / e_fwd)
                for pair in range(GLA_HEADS // 2):
                    kcols = slice(pair * 2 * GLA_DK, (pair + 1) * 2 * GLA_DK)
                    k_p = k_l[:, kcols]
                    k_pair = jnp.concatenate([jnp.where(lane_first, k_p, 0.0),
                                              jnp.where(lane_first, 0.0, k_p)], axis=0).astype(BF16)
                    s_pair = lax.dot_general(q_l[:, kcols], k_pair, _NT_DIMS,
                                             preferred_element_type=F32)
                    if finer is None:
                        a_pairs[pair] = s_pair
                    else:
                        same_finer = (ri2 // finer) == (ci2 // finer)
                        a_pairs[pair] = jnp.where(same_finer, a_pairs[pair], s_pair)
                finer = size
            causal2 = ri2 >= ci2
            a_pairs = [jnp.where(causal2, a, 0.0).astype(BF16) for a in a_pairs]
            v_blk = pj[rows, COL_V:COL_V + D_GLA_V].astype(BF16)
            kv_t = lax.dot_general(v_blk, k_hat, _TN_DIMS, preferred_element_type=F32)
            decay = jnp.exp(b_last)
            st_in = vals["st"]
            st_out = []
            for pair in range(GLA_HEADS // 2):
                pcols = slice(pair * MXU_N, (pair + 1) * MXU_N)
                kcols = slice(pair * 2 * GLA_DK, (pair + 1) * 2 * GLA_DK)
                st_out.append(st_in[pair] * decay[:, kcols]
                              + jnp.where(st_mask, kv_t[pcols, kcols], 0.0))
            vals["st"] = st_out
            vals[("blk", nb)] = (a_pairs, q_inter, [st_p.astype(BF16) for st_p in st_in])

        def output_stage(nb):
            lane_low = lax.broadcasted_iota(jnp.int32, (L, MXU_N), 1) < GLA_DV
            ggla = ggla_ref[...]
            a_pairs, q_inter, st_b = vals[("blk", nb)]
            rows = slice(nb * L, (nb + 1) * L)
            for pair in range(GLA_HEADS // 2):
                pcols = slice(pair * MXU_N, (pair + 1) * MXU_N)
                kcols = slice(pair * 2 * GLA_DK, (pair + 1) * 2 * GLA_DK)
                v_pair = pj[rows, COL_V + pair * MXU_N:COL_V + (pair + 1) * MXU_N]
                v_bd = jnp.concatenate([jnp.where(lane_low, v_pair, 0.0),
                                        jnp.where(lane_low, 0.0, v_pair)], axis=0).astype(BF16)
                o_inter = lax.dot_general(q_inter[:, kcols], st_b[pair], _NT_DIMS,
                                          preferred_element_type=F32)
                o_pair = jnp.dot(a_pairs[pair], v_bd, preferred_element_type=F32) + o_inter
                for hh in range(2):
                    hcols = slice(hh * GLA_DV, (hh + 1) * GLA_DV)
                    o_h = _rmsnorm(o_pair[:, hcols], ggla)
                    c0 = COL_R + pair * MXU_N + hh * GLA_DV
                    r_h = pj[rows, c0:c0 + GLA_DV]
                    o_h = o_h * (r_h * jax.nn.sigmoid(r_h))
                    y0 = D_POOL + pair * MXU_N + hh * GLA_DV
                    y_new[rows, y0:y0 + GLA_DV] = o_h.astype(BF16)

        if do_b:
            gate_stage()
        if do_cf:
            out_proj()
        if do_a:
            prenorm()
        if do_b:
            pool_sums()
        if do_a:
            proj_piece(0)
        if do_cf:
            out_finish()
        if do_b:
            prefix_stage()
        if do_a:
            proj_piece(1)
        if do_cf:
            ffn_chunk(0)
        if do_b:
            pool_maps()
            for nb in range(nblk):
                scores_stage(nb)
            st_ref[0] = vals["st"][0]
            st_ref[1] = vals["st"][1]
        if do_cf:
            ffn_chunk(1)
            ffn_chunk(2)
            ffn_chunk(3)
        if do_b:
            for nb in range(nblk):
                output_stage(nb)
        if do_cf:
            for c in range(4, D_FF // FFN_CHUNK):
                ffn_chunk(c)
        if do_a:
            proj_piece(2)
        if do_cf:
            ffn_down()
        if do_a:
            proj_r()
        if do_cf:
            ffn_finish()

    @pl.when(i == 0)
    def _():
        pl.run_scoped(load_weights,
                      pltpu.VMEM((STAGE_SLOTS, W_IN_ROWS, D_MODEL), F32),
                      pltpu.VMEM((STAGE_SLOTS, STAGE_ROWS, D_MODEL), F32),
                      pltpu.VMEM((STAGE_SLOTS, W_UP_ROWS, 2 * D_FF), F32),
                      pltpu.SemaphoreType.DMA((STAGE_SLOTS,)))
        pack_small_weights()
        step(0, True, False, False)
        step(1, True, True, False)

    @pl.when(jnp.logical_and(i >= 1, i < n_blocks))
    def _():
        step(0, True, True, True)
        step(1, True, True, True)

    @pl.when(i == n_blocks)
    def _():
        step(0, False, True, True)
        step(1, False, False, True)


def _const_spec(shape):
    zeros = (0,) * len(shape)
    return pl.BlockSpec(shape, lambda i: zeros, pipeline_mode=pl.Buffered(1))


def _decoder_layer(x, g_pre, w_in_t, w_pool, pool_scale, w_gu, b_gate, g_gla, w_out, g_post,
                   g_pre2, w_ffn_in, conv_w, conv_b, w_ffn_out, g_post2):
    B, S, D = x.shape
    ts = TILE
    bs = TILES_PER_STEP * ts
    n_blocks = (B * S) // bs
    x2 = x.reshape(B * S, D)
    kern = functools.partial(_layer_kernel, tiles_per_seq=S // ts, n_blocks=n_blocks)
    cs = _const_spec
    hbm = pl.BlockSpec(memory_space=pl.ANY)
    out = pl.pallas_call(
        kern,
        grid=(n_blocks + 1,),
        in_specs=[
            pl.BlockSpec((bs, D), lambda i: (jnp.minimum(i, n_blocks - 1), 0)),
            pl.BlockSpec((bs, D), lambda i: (jnp.maximum(i - 1, 0), 0)),
            cs((1, D)),
            hbm,
            cs((len(POOL_WINDOWS), POOL_GROUP_DIM, POOL_GROUP_DIM)),
            cs((1, D_POOL)),
            cs((GATE_RANK, D_GLA_K)),
            cs((1, D_GLA_K)),
            cs((1, GLA_DV)),
            hbm,
            cs((1, D)),
            cs((1, D)),
            hbm,
            cs((3, D_FF)),
            cs((1, D_FF)),
            hbm,
            cs((1, D)),
        ],
        out_specs=pl.BlockSpec((bs, D), lambda i: (jnp.maximum(i - 1, 0), 0)),
        out_shape=jax.ShapeDtypeStruct((B * S, D), F32),
        scratch_shapes=[
            pltpu.VMEM((D, W_IN_COLS), BF16),
            pltpu.VMEM((D, D_GLA_V), BF16),
            pltpu.VMEM((D, D), BF16),
            pltpu.VMEM((D, 2 * D_FF), BF16),
            pltpu.VMEM((D_FF, D), BF16),
            pltpu.VMEM((2, MXU_N, MXU_N), BF16),
            pltpu.VMEM((GATE_PAD, D_GLA_K), BF16),
            pltpu.VMEM((ts, D), BF16),
            pltpu.VMEM((2, ts, D_PROJ), F32),
            pltpu.VMEM((MAX_WINDOW, D_POOL), F32),
            pltpu.VMEM((ts, D_POOL), BF16),
            pltpu.VMEM((GLA_HEADS // 2, 2 * GLA_DV, 2 * GLA_DK), F32),
            pltpu.VMEM((2, ts, D), BF16),
            pltpu.VMEM((ts, D), F32),
            pltpu.VMEM((ts, D), BF16),
            pltpu.VMEM((CONV_TAIL, D_FF), F32),
            pltpu.VMEM((ts, D_FF), BF16),
        ],
        compiler_params=pltpu.CompilerParams(
            dimension_semantics=("arbitrary",),
            vmem_limit_bytes=VMEM_LIMIT_BYTES),
        name="decoder_layer",
    )(x2, x2, g_pre, w_in_t, w_pool, pool_scale, w_gu, b_gate, g_gla, w_out, g_post,
      g_pre2, w_ffn_in, conv_w, conv_b, w_ffn_out, g_post2)
    return out.reshape(B, S, D)


def kernel(x, g_pre_mix, w_in, w_pool, pool_scale, w_gate_up, b_gate, g_gla_norm, w_out, g_post_mix,
           g_pre_ffn, w_ffn_in, conv_w, conv_b, w_ffn_out, g_post_ffn):
    row = lambda a: a.reshape(1, -1).astype(F32)
    return _decoder_layer(
        x, row(g_pre_mix), w_in.astype(F32).T, w_pool.astype(F32),
        row(pool_scale), w_gate_up.astype(F32),
        row(b_gate), row(g_gla_norm), w_out.astype(F32), row(g_post_mix), row(g_pre_ffn),
        w_ffn_in.astype(F32), conv_w.astype(F32), row(conv_b), w_ffn_out.astype(F32),
        row(g_post_ffn))
```

```python
import functools
import math

import jax
import jax.numpy as jnp
from jax import lax
from jax.experimental import pallas as pl
from jax.experimental.pallas import tpu as pltpu

D_MODEL = 1024
D_POOL = 512
POOL_WINDOWS = (2, 4, 8, 16)
POOL_GROUP_DIM = 128
MAX_WINDOW = 16
D_GLA_V = 512
GLA_HEADS = 4
GLA_DV = 128
GLA_DK = 64
D_GLA_K = 256
GATE_RANK = 16
GATE_TAU = 16.0
D_FF = 2816
EPS = 1e-6

LANES = 128
MXU_N = 256
GATE_PAD = LANES
COL_U = 0
COL_Q = COL_U + D_POOL
COL_K = COL_Q + D_GLA_K
COL_V = COL_K + D_GLA_K
COL_G = COL_V + D_GLA_V
COL_R = COL_G + GATE_PAD
D_PROJ = COL_R + D_GLA_V
W_IN_R = COL_G + GATE_RANK
PROJ_PIECES = ((COL_U, D_POOL), (COL_Q, 2 * D_GLA_K), (COL_V, D_GLA_V + GATE_PAD))

TILE = 256
TILES_PER_STEP = 2
GLA_BLOCK = 128
GLA_DIAG = 32
FFN_CHUNK = 256
CONV_TAIL = 8

STAGE_SLOTS = 3
W_IN_ROWS = 128
W_IN_COLS = COL_G + GATE_PAD
W_UP_ROWS = 32
STAGE_ROWS = 128

VMEM_LIMIT_BYTES = 56 * 1024 * 1024

F32 = jnp.float32
BF16 = jnp.bfloat16
_NT_DIMS = (((1,), (1,)), ((), ()))
_TN_DIMS = (((0,), (0,)), ((), ()))


def _rmsnorm(x, g):
    return x * lax.rsqrt(jnp.mean(x * x, axis=-1, keepdims=True) + EPS) * g


def _layer_kernel(xn_ref, xp_ref, gpre_ref, win_hbm, wpool_f32, pscale_ref, wgu_f32, bg_ref,
                  ggla_ref, wout_hbm, gpost_ref, gpre2_ref, wfin_hbm, cw_ref, cb_ref, wfout_hbm,
                  gpost2_ref, o_ref, win_ref, wr_ref, wout_ref, wfin_ref, wfout_ref, wpool_ref, wgu_ref,
                  h_ref, proj_ref, utail_ref, d_ref, st_ref, y_ref,
                  x1_ref, h2_ref, tail_ref, act_ref, *, tiles_per_seq, n_blocks):
    ts = TILE
    L = GLA_BLOCK
    nblk = ts // L
    half = 2 * L
    i = pl.program_id(0)

    def load_weights(stage_in, stage_rows, stage_up, stage_sem):
        def cast_in_t(dst_ref):
            def cast(c0, chunk):
                dst_ref[:, c0:c0 + W_IN_ROWS] = chunk.T.astype(BF16)
            return cast

        def cast_to(dst_ref):
            def cast(r0, chunk):
                dst_ref[r0:r0 + chunk.shape[0], :] = chunk.astype(BF16)
            return cast

        jobs = []
        for c0 in range(0, W_IN_COLS, W_IN_ROWS):
            jobs.append((win_hbm, c0, c0, stage_in, cast_in_t(win_ref)))
        for c0 in range(0, D_GLA_V, W_IN_ROWS):
            jobs.append((win_hbm, W_IN_R + c0, c0, stage_in, cast_in_t(wr_ref)))
        for hbm_ref, stage, dst_ref in ((wout_hbm, stage_rows, wout_ref), (wfin_hbm, stage_up, wfin_ref),
                                        (wfout_hbm, stage_rows, wfout_ref)):
            chunk_rows = stage.shape[1]
            for r0 in range(0, hbm_ref.shape[0], chunk_rows):
                jobs.append((hbm_ref, r0, r0, stage, cast_to(dst_ref)))

        def copy(k):
            hbm_ref, r0, _, stage, _ = jobs[k]
            slot = k % STAGE_SLOTS
            return pltpu.make_async_copy(hbm_ref.at[pl.ds(r0, stage.shape[1]), :], stage.at[slot],
                                         stage_sem.at[slot])

        for k in range(STAGE_SLOTS):
            copy(k).start(priority=k % 2)
        for k in range(len(jobs)):
            _, _, d0, stage, cast = jobs[k]
            copy(k).wait()
            cast(d0, stage[k % STAGE_SLOTS])
            if k + STAGE_SLOTS < len(jobs):
                copy(k + STAGE_SLOTS).start(priority=(k + STAGE_SLOTS) % 2)

    def pack_small_weights():
        g = POOL_GROUP_DIM
        wpool_ref[...] = jnp.zeros_like(wpool_ref)
        for grp in range(len(POOL_WINDOWS)):
            o = (grp % 2) * g
            wpool_ref[grp // 2, o:o + g, o:o + g] = wpool_f32[grp].astype(BF16)
        wgu_ref[...] = jnp.zeros_like(wgu_ref)
        wgu_ref[0:GATE_RANK, :] = wgu_f32[...].astype(BF16)

    def step(sub, do_a, do_b, do_cf):
        t = TILES_PER_STEP * i + sub
        slot_a = sub % 2
        slot_b = 1 - slot_a
        rows_t = slice(sub * ts, (sub + 1) * ts)
        j = lax.rem(jnp.maximum(t - 1, 0), tiles_per_seq)
        jf = lax.rem(jnp.maximum(t - 2, 0), tiles_per_seq)
        pnew = proj_ref.at[slot_a]
        pj = proj_ref.at[slot_b]
        y_new = y_ref.at[slot_b]
        y_old = y_ref.at[slot_a]
        vals = {}

        if do_b:
            @pl.when(j == 0)
            def _():
                utail_ref[...] = jnp.zeros_like(utail_ref)
                st_ref[...] = jnp.zeros_like(st_ref)

        if do_cf:
            @pl.when(jf == 0)
            def _():
                tail_ref[...] = jnp.zeros_like(tail_ref)

        def prenorm():
            h_ref[...] = _rmsnorm(xn_ref[rows_t, :], gpre_ref[...]).astype(BF16)

        def proj_piece(p):
            n0, w = PROJ_PIECES[p]
            pnew[:, n0:n0 + w] = jnp.dot(h_ref[...], win_ref[:, n0:n0 + w],
                                         preferred_element_type=F32)

        def proj_r():
            pnew[:, COL_R:COL_R + D_GLA_V] = jnp.dot(h_ref[...], wr_ref[...],
                                                     preferred_element_type=F32)

        def out_proj():
            vals["mix"] = jnp.dot(y_old[...], wout_ref[...], preferred_element_type=F32)

        def out_finish():
            x1 = xp_ref[rows_t, :] + _rmsnorm(vals["mix"], gpost_ref[...])
            x1_ref[...] = x1
            h2_ref[...] = _rmsnorm(x1, gpre2_ref[...]).astype(BF16)

        def ffn_chunk(c):
            row = lax.broadcasted_iota(jnp.int32, (CONV_TAIL, FFN_CHUNK), 0)
            cols = slice(c * FFN_CHUNK, (c + 1) * FFN_CHUNK)
            vcols = slice(D_FF + c * FFN_CHUNK, D_FF + (c + 1) * FFN_CHUNK)
            gate = jnp.dot(h2_ref[...], wfin_ref[:, cols], preferred_element_type=F32)
            val = jnp.dot(h2_ref[...], wfin_ref[:, vcols], preferred_element_type=F32)
            prev = tail_ref[:, cols]
            tail_ref[:, cols] = gate[ts - CONV_TAIL:ts, :]
            p1 = prev[CONV_TAIL - 1:CONV_TAIL, :]
            p2 = prev[CONV_TAIL - 2:CONV_TAIL - 1, :]
            r1 = pltpu.roll(gate, 1, 0)
            r2 = pltpu.roll(gate, 2, 0)
            h1 = jnp.where(row == 0, p1, r1[:CONV_TAIL])
            h2 = jnp.where(row == 0, p2, jnp.where(row == 1, p1, r2[:CONV_TAIL]))
            g1 = jnp.concatenate([h1, r1[CONV_TAIL:]], axis=0)
            g2 = jnp.concatenate([h2, r2[CONV_TAIL:]], axis=0)
            cw = cw_ref[:, cols] * 0.5
            hc = g2 * cw[0:1, :] + g1 * cw[1:2, :] + gate * cw[2:3, :] + cb_ref[:, cols] * 0.5
            act = hc * (1.0 + lax.erf(hc * math.sqrt(2.0))) * val
            act_ref[:, cols] = act.astype(BF16)

        def ffn_down():
            vals["ff"] = jnp.dot(act_ref[...], wfout_ref[...], preferred_element_type=F32)

        def ffn_finish():
            o_ref[rows_t, :] = x1_ref[...] + _rmsnorm(vals["ff"], gpost2_ref[...])

        def pool_sums():
            row = lax.broadcasted_iota(jnp.int32, (MAX_WINDOW, 1), 0)
            pos = (j * ts + row + 1).astype(F32)
            for g, w in enumerate(POOL_WINDOWS):
                cols = slice(g * POOL_GROUP_DIM, (g + 1) * POOL_GROUP_DIM)
                u_g = pj[:, COL_U + g * POOL_GROUP_DIM:COL_U + (g + 1) * POOL_GROUP_DIM]
                ext = jnp.concatenate([utail_ref[:, cols], u_g], axis=0)
                utail_ref[:, cols] = u_g[ts - MAX_WINDOW:, :]
                acc = ext
                span = 1
                while span < w:
                    acc = acc + pltpu.roll(acc, span, 0)
                    span *= 2
                head = acc[MAX_WINDOW:2 * MAX_WINDOW, :] / jnp.minimum(pos, float(w))
                body = acc[2 * MAX_WINDOW:, :] * (1.0 / w)
                d = jnp.concatenate([head, body], axis=0) - u_g
                d_ref[:, cols] = d.astype(BF16)

        def pool_maps():
            for pair in range(2):
                cols = slice(pair * MXU_N, (pair + 1) * MXU_N)
                yp = jnp.dot(d_ref[:, cols], wpool_ref[pair], preferred_element_type=F32)
                y_new[:, cols] = (yp * pscale_ref[:, cols]).astype(BF16)

        def gate_stage():
            glow = pj[:, COL_G:COL_G + GATE_PAD].astype(BF16)
            logits = jnp.dot(glow, wgu_ref[...], preferred_element_type=F32) + bg_ref[...]
            log_a = ((jnp.minimum(logits, 0.0) - jnp.log(1.0 + jnp.exp(-jnp.abs(logits))))
                     * (1.0 / GATE_TAU))
            la_hi = log_a.astype(BF16)
            vals["la"] = (la_hi, (log_a - la_hi.astype(F32)).astype(BF16))

        def prefix_stage():
            la_hi, la_lo = vals["la"]
            ri = lax.broadcasted_iota(jnp.int32, (half, half), 0)
            ci = lax.broadcasted_iota(jnp.int32, (half, half), 1)
            blk_start = ri - lax.rem(ri, L)
            tri_bd = jnp.where(ci <= ri, jnp.where(ci >= blk_start, 1.0, 0.0), 0.0).astype(BF16)
            bcs = []
            for hb in range(ts // half):
                hrows = slice(hb * half, (hb + 1) * half)
                bc2 = (jnp.dot(tri_bd, la_hi[hrows, :], preferred_element_type=F32)
                       + jnp.dot(tri_bd, la_lo[hrows, :], preferred_element_type=F32))
                bcs += [bc2[0:L, :], bc2[L:half, :]]
            vals["bcs"] = bcs
            vals["st"] = [st_ref[0], st_ref[1]]

        def scores_stage(nb):
            lane_first = lax.broadcasted_iota(jnp.int32, (L, 2 * GLA_DK), 1) < GLA_DK
            ri2 = lax.broadcasted_iota(jnp.int32, (L, 2 * L), 0)
            ci2 = lax.rem(lax.broadcasted_iota(jnp.int32, (L, 2 * L), 1), L)
            st_row_first = lax.broadcasted_iota(jnp.int32, (2 * GLA_DV, 2 * GLA_DK), 0) < GLA_DV
            st_col_first = lax.broadcasted_iota(jnp.int32, (2 * GLA_DV, 2 * GLA_DK), 1) < GLA_DK
            st_mask = st_row_first == st_col_first
            rows = slice(nb * L, (nb + 1) * L)
            bc = vals["bcs"][nb]
            b_last = bc[L - 1:L, :]
            q = pj[rows, COL_Q:COL_Q + D_GLA_K] * (GLA_DK ** -0.5)
            k = pj[rows, COL_K:COL_K + D_GLA_K]
            q_inter = (q * jnp.exp(bc)).astype(BF16)
            k_hat = (k * jnp.exp(b_last - bc)).astype(BF16)

            def ref_rows(group, offset):
                parts = [jnp.broadcast_to(bc[g0 + offset:g0 + offset + 1, :], (group, D_GLA_K))
                         for g0 in range(0, L, group)]
                return parts[0] if len(parts) == 1 else jnp.concatenate(parts, axis=0)

            levels = []
            size = GLA_DIAG
            levels.append((size, ref_rows(size, size // 2 - 1)))
            while size < L:
                levels.append((2 * size, ref_rows(2 * size, size - 1)))
                size *= 2
            a_pairs = [None] * (GLA_HEADS // 2)
            finer = None
            for size, b_ref in levels:
                e_fwd = jnp.exp(bc - b_ref)
                q_l = (q * e_fwd).astype(BF16)
                k_l = k * (1.0 / e_fwd)
                for pair in range(GLA_HEADS // 2):
                    kcols = slice(pair * 2 * GLA_DK, (pair + 1) * 2 * GLA_DK)
                    k_p = k_l[:, kcols]
                    k_pair = jnp.concatenate([jnp.where(lane_first, k_p, 0.0),
                                              jnp.where(lane_first, 0.0, k_p)], axis=0).astype(BF16)
                    s_pair = lax.dot_general(q_l[:, kcols], k_pair, _NT_DIMS,
                                             preferred_element_type=F32)
                    if finer is None:
                        a_pairs[pair] = s_pair
                    else:
                        same_finer = (ri2 // finer) == (ci2 // finer)
                        a_pairs[pair] = jnp.where(same_finer, a_pairs[pair], s_pair)
                finer = size
            causal2 = ri2 >= ci2
            a_pairs = [jnp.where(causal2, a, 0.0).astype(BF16) for a in a_pairs]
            v_blk = pj[rows, COL_V:COL_V + D_GLA_V].astype(BF16)
            kv_t = lax.dot_general(v_blk, k_hat, _TN_DIMS, preferred_element_type=F32)
            decay = jnp.exp(b_last)
            st_in = vals["st"]
            st_out = []
            for pair in range(GLA_HEADS // 2):
                pcols = slice(pair * MXU_N, (pair + 1) * MXU_N)
                kcols = slice(pair * 2 * GLA_DK, (pair + 1) * 2 * GLA_DK)
                st_out.append(st_in[pair] * decay[:, kcols]
                              + jnp.where(st_mask, kv_t[pcols, kcols], 0.0))
            vals["st"] = st_out
            vals[("blk", nb)] = (a_pairs, q_inter, [st_p.astype(BF16) for st_p in st_in])

        def output_stage(nb):
            lane_low = lax.broadcasted_iota(jnp.int32, (L, MXU_N), 1) < GLA_DV
            ggla = ggla_ref[...]
            a_pairs, q_inter, st_b = vals[("blk", nb)]
            rows = slice(nb * L, (nb + 1) * L)
            for pair in range(GLA_HEADS // 2):
                pcols = slice(pair * MXU_N, (pair + 1) * MXU_N)
                kcols = slice(pair * 2 * GLA_DK, (pair + 1) * 2 * GLA_DK)
                v_pair = pj[rows, COL_V + pair * MXU_N:COL_V + (pair + 1) * MXU_N]
                v_bd = jnp.concatenate([jnp.where(lane_low, v_pair, 0.0),
                                        jnp.where(lane_low, 0.0, v_pair)], axis=0).astype(BF16)
                o_inter = lax.dot_general(q_inter[:, kcols], st_b[pair], _NT_DIMS,
                                          preferred_element_type=F32)
                o_pair = jnp.dot(a_pairs[pair], v_bd, preferred_element_type=F32) + o_inter
                for hh in range(2):
                    hcols = slice(hh * GLA_DV, (hh + 1) * GLA_DV)
                    o_h = _rmsnorm(o_pair[:, hcols], ggla)
                    c0 = COL_R + pair * MXU_N + hh * GLA_DV
                    r_h = pj[rows, c0:c0 + GLA_DV]
                    o_h = o_h * (r_h * jax.nn.sigmoid(r_h))
                    y0 = D_POOL + pair * MXU_N + hh * GLA_DV
                    y_new[rows, y0:y0 + GLA_DV] = o_h.astype(BF16)

        if do_b:
            gate_stage()
        if do_cf:
            out_proj()
        if do_a:
            prenorm()
        if do_b:
            pool_sums()
        if do_a:
            proj_piece(0)
        if do_cf:
            out_finish()
        if do_b:
            prefix_stage()
        if do_a:
            proj_piece(1)
        if do_cf:
            ffn_chunk(0)
        if do_b:
            pool_maps()
            for nb in range(nblk):
                scores_stage(nb)
            st_ref[0] = vals["st"][0]
            st_ref[1] = vals["st"][1]
        if do_cf:
            ffn_chunk(1)
            ffn_chunk(2)
            ffn_chunk(3)
        if do_b:
            for nb in range(nblk):
                output_stage(nb)
        if do_cf:
            for c in range(4, D_FF // FFN_CHUNK):
                ffn_chunk(c)
        if do_a:
            proj_piece(2)
        if do_cf:
            ffn_down()
        if do_a:
            proj_r()
        if do_cf:
            ffn_finish()

    @pl.when(i == 0)
    def _():
        pl.run_scoped(load_weights,
                      pltpu.VMEM((STAGE_SLOTS, W_IN_ROWS, D_MODEL), F32),
                      pltpu.VMEM((STAGE_SLOTS, STAGE_ROWS, D_MODEL), F32),
                      pltpu.VMEM((STAGE_SLOTS, W_UP_ROWS, 2 * D_FF), F32),
                      pltpu.SemaphoreType.DMA((STAGE_SLOTS,)))
        pack_small_weights()
        step(0, True, False, False)
        step(1, True, True, False)

    @pl.when(jnp.logical_and(i >= 1, i < n_blocks))
    def _():
        step(0, True, True, True)
        step(1, True, True, True)

    @pl.when(i == n_blocks)
    def _():
        step(0, False, True, True)
        step(1, False, False, True)


def _const_spec(shape):
    zeros = (0,) * len(shape)
    return pl.BlockSpec(shape, lambda i: zeros, pipeline_mode=pl.Buffered(1))


def _decoder_layer(x, g_pre, w_in_t, w_pool, pool_scale, w_gu, b_gate, g_gla, w_out, g_post,
                   g_pre2, w_ffn_in, conv_w, conv_b, w_ffn_out, g_post2):
    B, S, D = x.shape
    ts = TILE
    bs = TILES_PER_STEP * ts
    n_blocks = (B * S) // bs
    x2 = x.reshape(B * S, D)
    kern = functools.partial(_layer_kernel, tiles_per_seq=S // ts, n_blocks=n_blocks)
    cs = _const_spec
    hbm = pl.BlockSpec(memory_space=pl.ANY)
    out = pl.pallas_call(
        kern,
        grid=(n_blocks + 1,),
        in_specs=[
            pl.BlockSpec((bs, D), lambda i: (jnp.minimum(i, n_blocks - 1), 0)),
            pl.BlockSpec((bs, D), lambda i: (jnp.maximum(i - 1, 0), 0)),
            cs((1, D)),
            hbm,
            cs((len(POOL_WINDOWS), POOL_GROUP_DIM, POOL_GROUP_DIM)),
            cs((1, D_POOL)),
            cs((GATE_RANK, D_GLA_K)),
            cs((1, D_GLA_K)),
            cs((1, GLA_DV)),
            hbm,
            cs((1, D)),
            cs((1, D)),
            hbm,
            cs((3, D_FF)),
            cs((1, D_FF)),
            hbm,
            cs((1, D)),
        ],
        out_specs=pl.BlockSpec((bs, D), lambda i: (jnp.maximum(i - 1, 0), 0)),
        out_shape=jax.ShapeDtypeStruct((B * S, D), F32),
        scratch_shapes=[
            pltpu.VMEM((D, W_IN_COLS), BF16),
            pltpu.VMEM((D, D_GLA_V), BF16),
            pltpu.VMEM((D, D), BF16),
            pltpu.VMEM((D, 2 * D_FF), BF16),
            pltpu.VMEM((D_FF, D), BF16),
            pltpu.VMEM((2, MXU_N, MXU_N), BF16),
            pltpu.VMEM((GATE_PAD, D_GLA_K), BF16),
            pltpu.VMEM((ts, D), BF16),
            pltpu.VMEM((2, ts, D_PROJ), F32),
            pltpu.VMEM((MAX_WINDOW, D_POOL), F32),
            pltpu.VMEM((ts, D_POOL), BF16),
            pltpu.VMEM((GLA_HEADS // 2, 2 * GLA_DV, 2 * GLA_DK), F32),
            pltpu.VMEM((2, ts, D), BF16),
            pltpu.VMEM((ts, D), F32),
            pltpu.VMEM((ts, D), BF16),
            pltpu.VMEM((CONV_TAIL, D_FF), F32),
            pltpu.VMEM((ts, D_FF), BF16),
        ],
        compiler_params=pltpu.CompilerParams(
            dimension_semantics=("arbitrary",),
            vmem_limit_bytes=VMEM_LIMIT_BYTES),
        name="decoder_layer",
    )(x2, x2, g_pre, w_in_t, w_pool, pool_scale, w_gu, b_gate, g_gla, w_out, g_post,
      g_pre2, w_ffn_in, conv_w, conv_b, w_ffn_out, g_post2)
    return out.reshape(B, S, D)


def kernel(x, g_pre_mix, w_in, w_pool, pool_scale, w_gate_up, b_gate, g_gla_norm, w_out, g_post_mix,
           g_pre_ffn, w_ffn_in, conv_w, conv_b, w_ffn_out, g_post_ffn):
    row = lambda a: a.reshape(1, -1).astype(F32)
    return _decoder_layer(
        x, row(g_pre_mix), w_in.astype(F32).T, w_pool.astype(F32),
        row(pool_scale), w_gate_up.astype(F32),
        row(b_gate), row(g_gla_norm), w_out.astype(F32), row(g_post_mix), row(g_pre_ffn),
        w_ffn_in.astype(F32), conv_w.astype(F32), row(conv_b), w_ffn_out.astype(F32),
        row(g_post_ffn))
```

```python
import functools
import math

import jax
import jax.numpy as jnp
from jax import lax
from jax.experimental import pallas as pl
from jax.experimental.pallas import tpu as pltpu

D_MODEL = 1024
D_POOL = 512
POOL_WINDOWS = (2, 4, 8, 16)
POOL_GROUP_DIM = 128
MAX_WINDOW = 16
D_GLA_V = 512
GLA_HEADS = 4
GLA_DV = 128
GLA_DK = 64
D_GLA_K = 256
GATE_RANK = 16
GATE_TAU = 16.0
D_FF = 2816
EPS = 1e-6

LANES = 128
MXU_N = 256
GATE_PAD = LANES
COL_U = 0
COL_Q = COL_U + D_POOL
COL_K = COL_Q + D_GLA_K
COL_V = COL_K + D_GLA_K
COL_G = COL_V + D_GLA_V
COL_R = COL_G + GATE_PAD
D_PROJ = COL_R + D_GLA_V
W_IN_R = COL_G + GATE_RANK
PROJ_PIECES = ((COL_U, D_POOL), (COL_Q, 2 * D_GLA_K), (COL_V, D_GLA_V + GATE_PAD))

TILE = 256
TILES_PER_STEP = 2
GLA_BLOCK = 128
GLA_DIAG = 32
FFN_CHUNK = 256
CONV_TAIL = 8

STAGE_SLOTS = 3
W_IN_ROWS = 128
W_IN_COLS = COL_G + GATE_PAD
W_UP_ROWS = 32
STAGE_ROWS = 128

VMEM_LIMIT_BYTES = 56 * 1024 * 1024

F32 = jnp.float32
BF16 = jnp.bfloat16
_NT_DIMS = (((1,), (1,)), ((), ()))
_TN_DIMS = (((0,), (0,)), ((), ()))


def _rmsnorm(x, g):
    return x * lax.rsqrt(jnp.mean(x * x, axis=-1, keepdims=True) + EPS) * g


def _layer_kernel(xn_ref, xp_ref, gpre_ref, win_hbm, wpool_f32, pscale_ref, wgu_f32, bg_ref,
                  ggla_ref, wout_hbm, gpost_ref, gpre2_ref, wfin_hbm, cw_ref, cb_ref, wfout_hbm,
                  gpost2_ref, o_ref, win_ref, wr_ref, wout_ref, wfin_ref, wfout_ref, wpool_ref, wgu_ref,
                  h_ref, proj_ref, utail_ref, d_ref, st_ref, y_ref,
                  x1_ref, h2_ref, tail_ref, act_ref, *, tiles_per_seq, n_blocks):
    ts = TILE
    L = GLA_BLOCK
    nblk = ts // L
    half = 2 * L
    i = pl.program_id(0)

    def load_weights(stage_in, stage_rows, stage_up, stage_sem):
        def cast_in_t(dst_ref):
            def cast(c0, chunk):
                dst_ref[:, c0:c0 + W_IN_ROWS] = chunk.T.astype(BF16)
            return cast

        def cast_to(dst_ref):
            def cast(r0, chunk):
                dst_ref[r0:r0 + chunk.shape[0], :] = chunk.astype(BF16)
            return cast

        jobs = []
        for c0 in range(0, W_IN_COLS, W_IN_ROWS):
            jobs.append((win_hbm, c0, c0, stage_in, cast_in_t(win_ref)))
        for c0 in range(0, D_GLA_V, W_IN_ROWS):
            jobs.append((win_hbm, W_IN_R + c0, c0, stage_in, cast_in_t(wr_ref)))
        for hbm_ref, stage, dst_ref in ((wout_hbm, stage_rows, wout_ref), (wfin_hbm, stage_up, wfin_ref),
                                        (wfout_hbm, stage_rows, wfout_ref)):
            chunk_rows = stage.shape[1]
            for r0 in range(0, hbm_ref.shape[0], chunk_rows):
                jobs.append((hbm_ref, r0, r0, stage, cast_to(dst_ref)))

        def copy(k):
            hbm_ref, r0, _, stage, _ = jobs[k]
            slot = k % STAGE_SLOTS
            return pltpu.make_async_copy(hbm_ref.at[pl.ds(r0, stage.shape[1]), :], stage.at[slot],
                                         stage_sem.at[slot])

        for k in range(STAGE_SLOTS):
            copy(k).start()
        for k in range(len(jobs)):
            _, _, d0, stage, cast = jobs[k]
            copy(k).wait()
            cast(d0, stage[k % STAGE_SLOTS])
            if k + STAGE_SLOTS < len(jobs):
                copy(k + STAGE_SLOTS).start()

    def pack_small_weights():
        g = POOL_GROUP_DIM
        wpool_ref[...] = jnp.zeros_like(wpool_ref)
        for grp in range(len(POOL_WINDOWS)):
            o = (grp % 2) * g
            wpool_ref[grp // 2, o:o + g, o:o + g] = wpool_f32[grp].astype(BF16)
        wgu_ref[...] = jnp.zeros_like(wgu_ref)
        wgu_ref[0:GATE_RANK, :] = wgu_f32[...].astype(BF16)

    def step(sub, do_a, do_b, do_cf):
        t = TILES_PER_STEP * i + sub
        slot_a = sub % 2
        slot_b = 1 - slot_a
        rows_t = slice(sub * ts, (sub + 1) * ts)
        j = lax.rem(jnp.maximum(t - 1, 0), tiles_per_seq)
        jf = lax.rem(jnp.maximum(t - 2, 0), tiles_per_seq)
        pnew = proj_ref.at[slot_a]
        pj = proj_ref.at[slot_b]
        y_new = y_ref.at[slot_b]
        y_old = y_ref.at[slot_a]
        vals = {}

        if do_b:
            @pl.when(j == 0)
            def _():
                utail_ref[...] = jnp.zeros_like(utail_ref)
                st_ref[...] = jnp.zeros_like(st_ref)

        if do_cf:
            @pl.when(jf == 0)
            def _():
                tail_ref[...] = jnp.zeros_like(tail_ref)

        def prenorm():
            h_ref[...] = _rmsnorm(xn_ref[rows_t, :], gpre_ref[...]).astype(BF16)

        def proj_piece(p):
            n0, w = PROJ_PIECES[p]
            pnew[:, n0:n0 + w] = jnp.dot(h_ref[...], win_ref[:, n0:n0 + w],
                                         preferred_element_type=F32)

        def proj_r():
            pnew[:, COL_R:COL_R + D_GLA_V] = jnp.dot(h_ref[...], wr_ref[...],
                                                     preferred_element_type=F32)

        def out_proj():
            vals["mix"] = jnp.dot(y_old[...], wout_ref[...], preferred_element_type=F32)

        def out_finish():
            x1 = xp_ref[rows_t, :] + _rmsnorm(vals["mix"], gpost_ref[...])
            x1_ref[...] = x1
            h2_ref[...] = _rmsnorm(x1, gpre2_ref[...]).astype(BF16)

        def ffn_chunk(c):
            row = lax.broadcasted_iota(jnp.int32, (CONV_TAIL, FFN_CHUNK), 0)
            cols = slice(c * FFN_CHUNK, (c + 1) * FFN_CHUNK)
            vcols = slice(D_FF + c * FFN_CHUNK, D_FF + (c + 1) * FFN_CHUNK)
            gate = jnp.dot(h2_ref[...], wfin_ref[:, cols], preferred_element_type=F32)
            val = jnp.dot(h2_ref[...], wfin_ref[:, vcols], preferred_element_type=F32)
            prev = tail_ref[:, cols]
            tail_ref[:, cols] = gate[ts - CONV_TAIL:ts, :]
            p1 = prev[CONV_TAIL - 1:CONV_TAIL, :]
            p2 = prev[CONV_TAIL - 2:CONV_TAIL - 1, :]
            r1 = pltpu.roll(gate, 1, 0)
            r2 = pltpu.roll(gate, 2, 0)
            h1 = jnp.where(row == 0, p1, r1[:CONV_TAIL])
            h2 = jnp.where(row == 0, p2, jnp.where(row == 1, p1, r2[:CONV_TAIL]))
            g1 = jnp.concatenate([h1, r1[CONV_TAIL:]], axis=0)
            g2 = jnp.concatenate([h2, r2[CONV_TAIL:]], axis=0)
            cw = cw_ref[:, cols] * 0.5
            hc = g2 * cw[0:1, :] + g1 * cw[1:2, :] + gate * cw[2:3, :] + cb_ref[:, cols] * 0.5
            hb = hc.astype(BF16)
            act_ref[:, cols] = hb * (1.0 + lax.erf(hb * math.sqrt(2.0))) * val.astype(BF16)

        def ffn_down():
            vals["ff"] = jnp.dot(act_ref[...], wfout_ref[...], preferred_element_type=F32)

        def ffn_finish():
            o_ref[rows_t, :] = x1_ref[...] + _rmsnorm(vals["ff"], gpost2_ref[...])

        def pool_sums():
            row = lax.broadcasted_iota(jnp.int32, (MAX_WINDOW, 1), 0)
            pos = (j * ts + row + 1).astype(F32)
            for g, w in enumerate(POOL_WINDOWS):
                cols = slice(g * POOL_GROUP_DIM, (g + 1) * POOL_GROUP_DIM)
                u_g = pj[:, COL_U + g * POOL_GROUP_DIM:COL_U + (g + 1) * POOL_GROUP_DIM]
                ext = jnp.concatenate([utail_ref[:, cols], u_g], axis=0)
                utail_ref[:, cols] = u_g[ts - MAX_WINDOW:, :]
                acc = ext
                span = 1
                while span < w:
                    acc = acc + pltpu.roll(acc, span, 0)
                    span *= 2
                head = acc[MAX_WINDOW:2 * MAX_WINDOW, :] / jnp.minimum(pos, float(w))
                body = acc[2 * MAX_WINDOW:, :] * (1.0 / w)
                d = jnp.concatenate([head, body], axis=0) - u_g
                d_ref[:, cols] = d.astype(BF16)

        def pool_maps():
            for pair in range(2):
                cols = slice(pair * MXU_N, (pair + 1) * MXU_N)
                yp = jnp.dot(d_ref[:, cols], wpool_ref[pair], preferred_element_type=F32)
                y_new[:, cols] = (yp * pscale_ref[:, cols]).astype(BF16)

        def gate_stage():
            glow = pj[:, COL_G:COL_G + GATE_PAD].astype(BF16)
            logits = jnp.dot(glow, wgu_ref[...], preferred_element_type=F32) + bg_ref[...]
            log_a = ((jnp.minimum(logits, 0.0) - jnp.log(1.0 + jnp.exp(-jnp.abs(logits))))
                     * (1.0 / GATE_TAU))
            la_hi = log_a.astype(BF16)
            vals["la"] = (la_hi, (log_a - la_hi.astype(F32)).astype(BF16))

        def prefix_stage():
            la_hi, la_lo = vals["la"]
            ri = lax.broadcasted_iota(jnp.int32, (half, half), 0)
            ci = lax.broadcasted_iota(jnp.int32, (half, half), 1)
            blk_start = ri - lax.rem(ri, L)
            tri_bd = jnp.where(ci <= ri, jnp.where(ci >= blk_start, 1.0, 0.0), 0.0).astype(BF16)
            bcs = []
            for hb in range(ts // half):
                hrows = slice(hb * half, (hb + 1) * half)
                bc2 = (jnp.dot(tri_bd, la_hi[hrows, :], preferred_element_type=F32)
                       + jnp.dot(tri_bd, la_lo[hrows, :], preferred_element_type=F32))
                bcs += [bc2[0:L, :], bc2[L:half, :]]
            vals["bcs"] = bcs
            vals["st"] = [st_ref[0], st_ref[1]]

        def scores_stage(nb):
            lane_first = lax.broadcasted_iota(jnp.int32, (L, 2 * GLA_DK), 1) < GLA_DK
            ri2 = lax.broadcasted_iota(jnp.int32, (L, 2 * L), 0)
            ci2 = lax.rem(lax.broadcasted_iota(jnp.int32, (L, 2 * L), 1), L)
            st_row_first = lax.broadcasted_iota(jnp.int32, (2 * GLA_DV, 2 * GLA_DK), 0) < GLA_DV
            st_col_first = lax.broadcasted_iota(jnp.int32, (2 * GLA_DV, 2 * GLA_DK), 1) < GLA_DK
            st_mask = st_row_first == st_col_first
            rows = slice(nb * L, (nb + 1) * L)
            bc = vals["bcs"][nb]
            b_last = bc[L - 1:L, :]
            q = pj[rows, COL_Q:COL_Q + D_GLA_K] * (GLA_DK ** -0.5)
            k = pj[rows, COL_K:COL_K + D_GLA_K]
            q_inter = (q * jnp.exp(bc)).astype(BF16)
            k_hat = (k * jnp.exp(b_last - bc)).astype(BF16)

            def ref_rows(group, offset):
                parts = [jnp.broadcast_to(bc[g0 + offset:g0 + offset + 1, :], (group, D_GLA_K))
                         for g0 in range(0, L, group)]
                return parts[0] if len(parts) == 1 else jnp.concatenate(parts, axis=0)

            levels = []
            size = GLA_DIAG
            levels.append((size, ref_rows(size, size // 2 - 1)))
            while size < L:
                levels.append((2 * size, ref_rows(2 * size, size - 1)))
                size *= 2
            a_pairs = [None] * (GLA_HEADS // 2)
            finer = None
            for size, b_ref in levels:
                e_fwd = jnp.exp(bc - b_ref)
                q_l = (q * e_fwd).astype(BF16)
                k_l = k * (1.0 / e_fwd)
                for pair in range(GLA_HEADS // 2):
                    kcols = slice(pair * 2 * GLA_DK, (pair + 1) * 2 * GLA_DK)
                    k_p = k_l[:, kcols]
                    k_pair = jnp.concatenate([jnp.where(lane_first, k_p, 0.0),
                                              jnp.where(lane_first, 0.0, k_p)], axis=0).astype(BF16)
                    s_pair = lax.dot_general(q_l[:, kcols], k_pair, _NT_DIMS,
                                             preferred_element_type=F32)
                    if finer is None:
                        a_pairs[pair] = s_pair
                    else:
                        same_finer = (ri2 // finer) == (ci2 // finer)
                        a_pairs[pair] = jnp.where(same_finer, a_pairs[pair], s_pair)
                finer = size
            causal2 = ri2 >= ci2
            a_pairs = [jnp.where(causal2, a, 0.0).astype(BF16) for a in a_pairs]
            v_blk = pj[rows, COL_V:COL_V + D_GLA_V].astype(BF16)
            kv_t = lax.dot_general(v_blk, k_hat, _TN_DIMS, preferred_element_type=F32)
            decay = jnp.exp(b_last)
            st_in = vals["st"]
            st_out = []
            for pair in range(GLA_HEADS // 2):
                pcols = slice(pair * MXU_N, (pair + 1) * MXU_N)
                kcols = slice(pair * 2 * GLA_DK, (pair + 1) * 2 * GLA_DK)
                st_out.append(st_in[pair] * decay[:, kcols]
                              + jnp.where(st_mask, kv_t[pcols, kcols], 0.0))
            vals["st"] = st_out
            vals[("blk", nb)] = (a_pairs, q_inter, [st_p.astype(BF16) for st_p in st_in])

        def output_stage(nb):
            lane_low = lax.broadcasted_iota(jnp.int32, (L, MXU_N), 1) < GLA_DV
            ggla = ggla_ref[...]
            a_pairs, q_inter, st_b = vals[("blk", nb)]
            rows = slice(nb * L, (nb + 1) * L)
            for pair in range(GLA_HEADS // 2):
                pcols = slice(pair * MXU_N, (pair + 1) * MXU_N)
                kcols = slice(pair * 2 * GLA_DK, (pair + 1) * 2 * GLA_DK)
                v_pair = pj[rows, COL_V + pair * MXU_N:COL_V + (pair + 1) * MXU_N]
                v_bd = jnp.concatenate([jnp.where(lane_low, v_pair, 0.0),
                                        jnp.where(lane_low, 0.0, v_pair)], axis=0).astype(BF16)
                o_inter = lax.dot_general(q_inter[:, kcols], st_b[pair], _NT_DIMS,
                                          preferred_element_type=F32)
                o_pair = jnp.dot(a_pairs[pair], v_bd, preferred_element_type=F32) + o_inter
                for hh in range(2):
                    hcols = slice(hh * GLA_DV, (hh + 1) * GLA_DV)
                    o_h = _rmsnorm(o_pair[:, hcols], ggla)
                    c0 = COL_R + pair * MXU_N + hh * GLA_DV
                    r_h = pj[rows, c0:c0 + GLA_DV]
                    o_h = o_h * (r_h * jax.nn.sigmoid(r_h))
                    y0 = D_POOL + pair * MXU_N + hh * GLA_DV
                    y_new[rows, y0:y0 + GLA_DV] = o_h.astype(BF16)

        if do_b:
            gate_stage()
        if do_cf:
            out_proj()
        if do_a:
            prenorm()
        if do_b:
            pool_sums()
        if do_a:
            proj_piece(0)
        if do_cf:
            out_finish()
        if do_b:
            prefix_stage()
        if do_a:
            proj_piece(1)
        if do_cf:
            ffn_chunk(0)
        if do_b:
            pool_maps()
            for nb in range(nblk):
                scores_stage(nb)
            st_ref[0] = vals["st"][0]
            st_ref[1] = vals["st"][1]
        if do_cf:
            ffn_chunk(1)
            ffn_chunk(2)
            ffn_chunk(3)
        if do_b:
            for nb in range(nblk):
                output_stage(nb)
        if do_cf:
            for c in range(4, D_FF // FFN_CHUNK):
                ffn_chunk(c)
        if do_a:
            proj_piece(2)
        if do_cf:
            ffn_down()
        if do_a:
            proj_r()
        if do_cf:
            ffn_finish()

    @pl.when(i == 0)
    def _():
        pl.run_scoped(load_weights,
                      pltpu.VMEM((STAGE_SLOTS, W_IN_ROWS, D_MODEL), F32),
                      pltpu.VMEM((STAGE_SLOTS, STAGE_ROWS, D_MODEL), F32),
                      pltpu.VMEM((STAGE_SLOTS, W_UP_ROWS, 2 * D_FF), F32),
                      pltpu.SemaphoreType.DMA((STAGE_SLOTS,)))
        pack_small_weights()
        step(0, True, False, False)
        step(1, True, True, False)

    @pl.when(jnp.logical_and(i >= 1, i < n_blocks))
    def _():
        step(0, True, True, True)
        step(1, True, True, True)

    @pl.when(i == n_blocks)
    def _():
        step(0, False, True, True)
        step(1, False, False, True)


def _const_spec(shape):
    zeros = (0,) * len(shape)
    return pl.BlockSpec(shape, lambda i: zeros, pipeline_mode=pl.Buffered(1))


def _decoder_layer(x, g_pre, w_in_t, w_pool, pool_scale, w_gu, b_gate, g_gla, w_out, g_post,
                   g_pre2, w_ffn_in, conv_w, conv_b, w_ffn_out, g_post2):
    B, S, D = x.shape
    ts = TILE
    bs = TILES_PER_STEP * ts
    n_blocks = (B * S) // bs
    x2 = x.reshape(B * S, D)
    kern = functools.partial(_layer_kernel, tiles_per_seq=S // ts, n_blocks=n_blocks)
    cs = _const_spec
    hbm = pl.BlockSpec(memory_space=pl.ANY)
    out = pl.pallas_call(
        kern,
        grid=(n_blocks + 1,),
        in_specs=[
            pl.BlockSpec((bs, D), lambda i: (jnp.minimum(i, n_blocks - 1), 0)),
            pl.BlockSpec((bs, D), lambda i: (jnp.maximum(i - 1, 0), 0)),
            cs((1, D)),
            hbm,
            cs((len(POOL_WINDOWS), POOL_GROUP_DIM, POOL_GROUP_DIM)),
            cs((1, D_POOL)),
            cs((GATE_RANK, D_GLA_K)),
            cs((1, D_GLA_K)),
            cs((1, GLA_DV)),
            hbm,
            cs((1, D)),
            cs((1, D)),
            hbm,
            cs((3, D_FF)),
            cs((1, D_FF)),
            hbm,
            cs((1, D)),
        ],
        out_specs=pl.BlockSpec((bs, D), lambda i: (jnp.maximum(i - 1, 0), 0)),
        out_shape=jax.ShapeDtypeStruct((B * S, D), F32),
        scratch_shapes=[
            pltpu.VMEM((D, W_IN_COLS), BF16),
            pltpu.VMEM((D, D_GLA_V), BF16),
            pltpu.VMEM((D, D), BF16),
            pltpu.VMEM((D, 2 * D_FF), BF16),
            pltpu.VMEM((D_FF, D), BF16),
            pltpu.VMEM((2, MXU_N, MXU_N), BF16),
            pltpu.VMEM((GATE_PAD, D_GLA_K), BF16),
            pltpu.VMEM((ts, D), BF16),
            pltpu.VMEM((2, ts, D_PROJ), F32),
            pltpu.VMEM((MAX_WINDOW, D_POOL), F32),
            pltpu.VMEM((ts, D_POOL), BF16),
            pltpu.VMEM((GLA_HEADS // 2, 2 * GLA_DV, 2 * GLA_DK), F32),
            pltpu.VMEM((2, ts, D), BF16),
            pltpu.VMEM((ts, D), F32),
            pltpu.VMEM((ts, D), BF16),
            pltpu.VMEM((CONV_TAIL, D_FF), F32),
            pltpu.VMEM((ts, D_FF), BF16),
        ],
        compiler_params=pltpu.CompilerParams(
            dimension_semantics=("arbitrary",),
            vmem_limit_bytes=VMEM_LIMIT_BYTES),
        name="decoder_layer",
    )(x2, x2, g_pre, w_in_t, w_pool, pool_scale, w_gu, b_gate, g_gla, w_out, g_post,
      g_pre2, w_ffn_in, conv_w, conv_b, w_ffn_out, g_post2)
    return out.reshape(B, S, D)


def kernel(x, g_pre_mix, w_in, w_pool, pool_scale, w_gate_up, b_gate, g_gla_norm, w_out, g_post_mix,
           g_pre_ffn, w_ffn_in, conv_w, conv_b, w_ffn_out, g_post_ffn):
    row = lambda a: a.reshape(1, -1).astype(F32)
    return _decoder_layer(
        x, row(g_pre_mix), w_in.astype(F32).T, w_pool.astype(F32),
        row(pool_scale), w_gate_up.astype(F32),
        row(b_gate), row(g_gla_norm), w_out.astype(F32), row(g_post_mix), row(g_pre_ffn),
        w_ffn_in.astype(F32), conv_w.astype(F32), row(conv_b), w_ffn_out.astype(F32),
        row(g_post_ffn))
```
